```python
import math
import jax, jax.numpy as jnp
from jax import lax
import numpy as np

D_MODEL = 1024
BATCH = 8
SEQ = 8192
DEPTH = 1
DEC_BATCH = 8
DEC_SEQ = 16
PAST_LEN = 4096

CHUNK = 64
Q_BLOCK = 128
EPS = 1e-6
H_A = 4
DH_A = 64
H_B = 8
DH_B = 64
N_BUCKETS = 32
MAX_DISTANCE = 128
N_GROUPS = 4
EXPERTS_PER_GROUP = 4
N_EXPERTS = N_GROUPS * EXPERTS_PER_GROUP
TOP_K = 2
D_EXPERT = 512
QA = H_A * 2 * DH_A
VA = H_A * 2 * DH_A
QB = H_B * DH_B
IN_COLS = 2 * QA + VA + 3 * QB

kernel_name = 'hybrid_diffattn_stickbreak_hiermoe_stream_step'


def rms_norm(x, g):
    xf = x.astype(jnp.float32)
    y = xf * lax.rsqrt(jnp.mean(xf * xf, axis=-1, keepdims=True) + EPS)
    return (y * g.astype(jnp.float32)).astype(x.dtype)


def t5_bucket(rel):
    half = N_BUCKETS // 2
    max_exact = half // 2
    base = jnp.where(rel > 0, half, 0)
    n = jnp.abs(rel)
    nf = jnp.maximum(n, 1).astype(jnp.float32)
    large = max_exact + (jnp.log(nf / max_exact) / math.log(MAX_DISTANCE / max_exact)
                         * (half - max_exact)).astype(jnp.int32)
    large = jnp.minimum(large, half - 1)
    return base + jnp.where(n < max_exact, n, large)


def project_qkv(h, w_in, q_norm_g, k_norm_g):
    b, t, _ = h.shape
    proj = h @ w_in
    qa, ka, va, qb, kb, vb = jnp.split(
        proj, [QA, 2 * QA, 2 * QA + VA, 2 * QA + VA + QB, 2 * QA + VA + 2 * QB], axis=-1)
    qa = rms_norm(qa.reshape(b, t, H_A, 2, DH_A), q_norm_g)
    ka = rms_norm(ka.reshape(b, t, H_A, 2, DH_A), k_norm_g)
    va = va.reshape(b, t, H_A, 2 * DH_A)
    qb = qb.reshape(b, t, H_B, DH_B)
    kb = kb.reshape(b, t, H_B, DH_B)
    vb = vb.reshape(b, t, H_B, DH_B)
    return qa, ka, va, qb, kb, vb


def diff_attn_block(q, k, v, q_pos, k_pos, rel_table, lam, lam_init, subln_g):
    logits = jnp.einsum('bqhcd,bkhcd->bhcqk', q, k,
                        preferred_element_type=jnp.float32) / math.sqrt(DH_A)
    rel = k_pos[None, :] - q_pos[:, None]
    bias = rel_table.astype(jnp.float32)[t5_bucket(rel)]
    bias = jnp.transpose(bias, (2, 0, 1))[None, :, None]
    allowed = (k_pos[None, :] // CHUNK) <= (q_pos[:, None] // CHUNK)
    logits = jnp.where(allowed, logits + bias, -jnp.inf)
    p = jax.nn.softmax(logits, axis=-1)
    attn = p[:, :, 0] - lam * p[:, :, 1]
    out = jnp.einsum('bhqk,bkhe->bqhe', attn.astype(v.dtype), v)
    out = rms_norm(out, subln_g) * (1.0 - lam_init)
    return out.reshape(out.shape[0], out.shape[1], VA)


def stick_breaking_block(q, k, v, q_pos, k_pos):
    z = jnp.einsum('bqhd,bkhd->bhqk', q, k,
                   preferred_element_type=jnp.float32) / math.sqrt(DH_B)
    allowed = k_pos[None, :] < q_pos[:, None]
    log_beta = jax.nn.log_sigmoid(z)
    log_keep = jnp.where(allowed, jax.nn.log_sigmoid(-z), 0.0)
    later = lax.cumsum(log_keep, axis=3, reverse=True) - log_keep
    a = jnp.where(allowed, jnp.exp(log_beta + later), 0.0)
    out = jnp.einsum('bhqk,bkhd->bqhd', a.astype(v.dtype), v)
    return out.reshape(out.shape[0], out.shape[1], QB)


def hier_moe(h, w_rg, b_rg, w_re, b_re, mg, mu, md):
    hf = h.reshape(-1, D_MODEL)
    n = hf.shape[0]
    g_logits = (hf @ w_rg + b_rg).astype(jnp.float32)
    g_sel = jnp.argmax(g_logits, axis=-1)
    e_logits = (hf @ w_re + b_re).astype(jnp.float32).reshape(n, N_GROUPS, EXPERTS_PER_GROUP)
    e_in_group = e_logits[jnp.arange(n), g_sel]
    top_v, top_i = lax.top_k(e_in_group, TOP_K)
    top_p = jax.nn.softmax(top_v, axis=-1)
    expert_id = g_sel[:, None] * EXPERTS_PER_GROUP + top_i
    combine = jnp.sum(jax.nn.one_hot(expert_id, N_EXPERTS, dtype=jnp.float32)
                      * top_p[..., None], axis=1).astype(h.dtype)
    y = jnp.zeros_like(hf)
    for e in range(N_EXPERTS):
        act = jax.nn.silu(hf @ mg[e]) * (hf @ mu[e])
        y = y + combine[:, e:e + 1] * (act @ md[e])
    return y.reshape(h.shape)


def layer_tail(x, h, ya, yb, w_o_diff, w_o_sb, w_gate, b_gate, w_out,
               norm_ffn_g, w_rg, b_rg, w_re, b_re, mg, mu, md):
    gates = jax.nn.sigmoid(h @ w_gate + b_gate)
    g_a, g_b = jnp.split(gates, 2, axis=-1)
    x = x + (g_a * (ya @ w_o_diff) + g_b * (yb @ w_o_sb)) @ w_out
    return x + hier_moe(rms_norm(x, norm_ffn_g), w_rg, b_rg, w_re, b_re, mg, mu, md)


def setup_inputs(seed: int = 0) -> dict:
    key = jax.random.key(seed)
    ks = jax.random.split(key, 32)
    f32 = jnp.float32
    nrm = lambda k, shape, s: jax.random.normal(k, shape, f32) * s
    gain = lambda k, shape: 1.0 + 0.02 * jax.random.normal(k, shape, f32)
    return {
        'x_prompt': nrm(ks[0], (BATCH, SEQ, D_MODEL), 1.0),
        'x_sample': nrm(ks[1], (DEC_BATCH, DEC_SEQ, D_MODEL), 1.0),
        'cache_diff_k': nrm(ks[2], (DEPTH, DEC_BATCH, PAST_LEN, H_A, 2, DH_A), 1.0),
        'cache_diff_v': nrm(ks[3], (DEPTH, DEC_BATCH, PAST_LEN, H_A, 2 * DH_A), 1.0),
        'cache_sb_k': nrm(ks[4], (DEPTH, DEC_BATCH, PAST_LEN, H_B, DH_B), 1.0),
        'cache_sb_v': nrm(ks[5], (DEPTH, DEC_BATCH, PAST_LEN, H_B, DH_B), 1.0),
        'rel_bias_table': nrm(ks[6], (N_BUCKETS, H_A), 0.5),
        'norm_mix_g': gain(ks[7], (DEPTH, D_MODEL)),
        'w_in': nrm(ks[8], (DEPTH, D_MODEL, IN_COLS), D_MODEL ** -0.5),
        'q_norm_g': gain(ks[9], (DEPTH, DH_A)),
        'k_norm_g': gain(ks[10], (DEPTH, DH_A)),
        'lambda_q1': nrm(ks[11], (DEPTH, DH_A), 0.1),
        'lambda_k1': nrm(ks[12], (DEPTH, DH_A), 0.1),
        'lambda_q2': nrm(ks[13], (DEPTH, DH_A), 0.1),
        'lambda_k2': nrm(ks[14], (DEPTH, DH_A), 0.1),
        'subln_g': gain(ks[15], (DEPTH, 2 * DH_A)),
        'w_o_diff': nrm(ks[16], (DEPTH, VA, D_MODEL), VA ** -0.5),
        'w_o_sb': nrm(ks[17], (DEPTH, QB, D_MODEL), QB ** -0.5),
        'w_branch_gate': nrm(ks[18], (DEPTH, D_MODEL, 2 * D_MODEL), D_MODEL ** -0.5),
        'b_branch_gate': nrm(ks[19], (DEPTH, 2 * D_MODEL), 0.1),
        'w_out': nrm(ks[20], (DEPTH, D_MODEL, D_MODEL), D_MODEL ** -0.5),
        'norm_ffn_g': gain(ks[21], (DEPTH, D_MODEL)),
        'w_router_group': nrm(ks[22], (DEPTH, D_MODEL, N_GROUPS), D_MODEL ** -0.5),
        'b_router_group': nrm(ks[23], (DEPTH, N_GROUPS), 0.01),
        'w_router_expert': nrm(ks[24], (DEPTH, D_MODEL, N_EXPERTS), D_MODEL ** -0.5),
        'b_router_expert': nrm(ks[25], (DEPTH, N_EXPERTS), 0.01),
        'moe_w_gate': nrm(ks[26], (DEPTH, N_EXPERTS, D_MODEL, D_EXPERT), D_MODEL ** -0.5),
        'moe_w_up': nrm(ks[27], (DEPTH, N_EXPERTS, D_MODEL, D_EXPERT), D_MODEL ** -0.5),
        'moe_w_down': nrm(ks[28], (DEPTH, N_EXPERTS, D_EXPERT, D_MODEL), D_EXPERT ** -0.5),
    }


def reference(x_prompt, x_sample, cache_diff_k, cache_diff_v, cache_sb_k, cache_sb_v,
              rel_bias_table, norm_mix_g, w_in, q_norm_g, k_norm_g,
              lambda_q1, lambda_k1, lambda_q2, lambda_k2, subln_g,
              w_o_diff, w_o_sb, w_branch_gate, b_branch_gate, w_out,
              norm_ffn_g, w_router_group, b_router_group, w_router_expert, b_router_expert,
              moe_w_gate, moe_w_up, moe_w_down):
    n_blk = SEQ // Q_BLOCK
    pos_p = jnp.arange(SEQ, dtype=jnp.int32)
    pos_s_keys = jnp.arange(PAST_LEN + DEC_SEQ, dtype=jnp.int32)
    pos_s_q = pos_s_keys[PAST_LEN:]
    y_p, y_s = x_prompt, x_sample
    dk_p, dv_p, sk_p, sv_p = [], [], [], []
    dk_s, dv_s, sk_s, sv_s = [], [], [], []
    for l in range(DEPTH):
        lam_init = 0.8 - 0.6 * math.exp(-0.3 * l)
        lam = (jnp.exp(jnp.sum(lambda_q1[l].astype(jnp.float32) * lambda_k1[l].astype(jnp.float32)))
               - jnp.exp(jnp.sum(lambda_q2[l].astype(jnp.float32) * lambda_k2[l].astype(jnp.float32)))
               + lam_init)

        def mix(qa, qb, ka, va, kb, vb, q_pos, k_pos):
            ya = diff_attn_block(qa, ka, va, q_pos, k_pos, rel_bias_table, lam, lam_init, subln_g[l])
            yb = stick_breaking_block(qb, kb, vb, q_pos, k_pos)
            return ya, yb

        def tail(x, h, ya, yb):
            return layer_tail(x, h, ya, yb, w_o_diff[l], w_o_sb[l], w_branch_gate[l],
                              b_branch_gate[l], w_out[l], norm_ffn_g[l],
                              w_router_group[l], b_router_group[l],
                              w_router_expert[l], b_router_expert[l],
                              moe_w_gate[l], moe_w_up[l], moe_w_down[l])

        h = rms_norm(y_p, norm_mix_g[l])
        qa, ka, va, qb, kb, vb = project_qkv(h, w_in[l], q_norm_g[l], k_norm_g[l])

        def blocks(t):
            return jnp.moveaxis(t.reshape((t.shape[0], n_blk, Q_BLOCK) + t.shape[2:]), 1, 0)

        ya_b, yb_b = lax.map(lambda a: mix(a[0], a[1], ka, va, kb, vb, a[2], pos_p),
                             (blocks(qa), blocks(qb), pos_p.reshape(n_blk, Q_BLOCK)))
        ya = jnp.moveaxis(ya_b, 0, 1).reshape(y_p.shape[0], SEQ, VA)
        yb = jnp.moveaxis(yb_b, 0, 1).reshape(y_p.shape[0], SEQ, QB)
        y_p = tail(y_p, h, ya, yb)
        dk_p.append(ka)
        dv_p.append(va)
        sk_p.append(kb)
        sv_p.append(vb)

        h = rms_norm(y_s, norm_mix_g[l])
        qa, ka, va, qb, kb, vb = project_qkv(h, w_in[l], q_norm_g[l], k_norm_g[l])
        ya, yb = mix(qa, qb,
                     jnp.concatenate([cache_diff_k[l], ka], axis=1),
                     jnp.concatenate([cache_diff_v[l], va], axis=1),
                     jnp.concatenate([cache_sb_k[l], kb], axis=1),
                     jnp.concatenate([cache_sb_v[l], vb], axis=1),
                     pos_s_q, pos_s_keys)
        y_s = tail(y_s, h, ya, yb)
        dk_s.append(ka)
        dv_s.append(va)
        sk_s.append(kb)
        sv_s.append(vb)

    diff_k_prompt = jnp.stack(dk_p)
    diff_v_prompt = jnp.stack(dv_p)
    sb_k_prompt = jnp.stack(sk_p)
    sb_v_prompt = jnp.stack(sv_p)
    diff_k_sample = jnp.stack(dk_s)
    diff_v_sample = jnp.stack(dv_s)
    sb_k_sample = jnp.stack(sk_s)
    sb_v_sample = jnp.stack(sv_s)
    return (y_p, y_s, diff_k_prompt, diff_v_prompt, sb_k_prompt, sb_v_prompt,
            diff_k_sample, diff_v_sample, sb_k_sample, sb_v_sample)
```

```python
import functools
import math

import numpy as np
import jax
import jax.numpy as jnp
from jax import lax
from jax.experimental import pallas as pl
from jax.experimental.pallas import tpu as pltpu

F32 = jnp.float32
BF16 = jnp.bfloat16

EPS = 1e-6
CHUNK = 64
H_A = 4
DH = 64
H_B = 8
SLAB = 2 * DH
N_SLAB_A = H_A
N_SLAB_B = H_B // 2
COLS = H_A * SLAB
N_BUCKETS = 32
MAX_DISTANCE = 128
N_GROUPS = 4
EXPERTS_PER_GROUP = 4
N_EXPERTS = N_GROUPS * EXPERTS_PER_GROUP
ROUTER_LANES = 128
NEG = -1e30
SB_DEAD = -104.0
VMEM_LIMIT = 48 * 1024 * 1024


def _rms(x, g):
    return x * lax.rsqrt(jnp.mean(x * x, axis=-1, keepdims=True) + EPS) * g


def _sigmoid(x):
    return 1.0 / (1.0 + jnp.exp(-x))


def _dot(a, b):
    return jnp.dot(a, b, preferred_element_type=F32)


def _dot_nt(a, b):
    return lax.dot_general(a, b, (((1,), (1,)), ((), ())), preferred_element_type=F32)


def _split_bf16(x):
    hi = x.astype(BF16)
    lo = (x - hi.astype(F32)).astype(BF16)
    return hi, lo


def _proj_kernel(x_ref, g_ref, w_ref, gq_ref, gk_ref, gm_ref,
                 ka_o, va_o, kb_o, vb_o, qa_b, ka_b, va_b, qb_b, kb_b, vb_b):
    h = _rms(x_ref[...], g_ref[...]).astype(BF16)
    proj = _dot(h, w_ref[...])
    gm = gm_ref[...]

    def sub_head_norm(t, gain):
        outs = []
        for s in range(0, COLS, 256):
            ts = t[:, s:s + 256]
            hi, lo = _split_bf16(ts * ts)
            msq = _dot(hi, gm) + _dot(lo, gm)
            outs.append(ts * lax.rsqrt(msq + EPS))
        return jnp.concatenate(outs, axis=1) * gain

    qa = sub_head_norm(proj[:, 0 * COLS:1 * COLS], gq_ref[...])
    ka = sub_head_norm(proj[:, 1 * COLS:2 * COLS], gk_ref[...])
    va = proj[:, 2 * COLS:3 * COLS]
    qb = proj[:, 3 * COLS:4 * COLS]
    kb = proj[:, 4 * COLS:5 * COLS]
    vb = proj[:, 5 * COLS:6 * COLS]
    ka_o[...] = ka
    va_o[...] = va
    kb_o[...] = kb
    vb_o[...] = vb
    scale = 1.0 / math.sqrt(DH)
    qa_b[...] = (qa * scale).astype(BF16)
    ka_b[...] = ka.astype(BF16)
    va_b[...] = va.astype(BF16)
    qb_b[...] = (qb * scale).astype(BF16)
    kb_b[...] = kb.astype(BF16)
    vb_b[...] = vb.astype(BF16)


def _project(x2d, g_mix, w_in_b, gq, gk, gm, tm):
    n, d = x2d.shape
    row = lambda i: (i, 0)
    const = lambda i: (0, 0)
    out_f = jax.ShapeDtypeStruct((n, COLS), F32)
    out_b = jax.ShapeDtypeStruct((n, COLS), BF16)
    blk = pl.BlockSpec((tm, COLS), row)
    return pl.pallas_call(
        _proj_kernel,
        grid=(n // tm,),
        in_specs=[pl.BlockSpec((tm, d), row),
                  pl.BlockSpec((1, d), const),
                  pl.BlockSpec(w_in_b.shape, const),
                  pl.BlockSpec((1, COLS), const),
                  pl.BlockSpec((1, COLS), const),
                  pl.BlockSpec(gm.shape, const)],
        out_specs=[blk] * 10,
        out_shape=[out_f] * 4 + [out_b] * 6,
        compiler_params=pltpu.CompilerParams(
            dimension_semantics=("arbitrary",), vmem_limit_bytes=VMEM_LIMIT),
        name="proj",
    )(x2d, g_mix, w_in_b, gq, gk, gm)


def _t5_bucket_np(rel):
    half = N_BUCKETS // 2
    max_exact = half // 2
    base = np.where(rel > 0, half, 0)
    n = np.abs(rel)
    nf = np.maximum(n, 1).astype(np.float64)
    large = max_exact + (np.log(nf / max_exact) / math.log(MAX_DISTANCE / max_exact)
                         * (half - max_exact)).astype(np.int32)
    large = np.minimum(large, half - 1)
    return (base + np.where(n < max_exact, n, large)).astype(np.int32)


def _bucket_tile_np(q_pos, k_pos):
    rel = k_pos[None, :] - q_pos[:, None]
    allowed = (k_pos[None, :] // CHUNK) <= (q_pos[:, None] // CHUNK)
    return np.where(allowed, _t5_bucket_np(rel), -1).astype(np.int32)


def _bias_kernel(tbl_ref, bkt_ref, o_ref):
    h = pl.program_id(0)
    bkt = bkt_ref[...]
    out = jnp.full(bkt.shape, NEG, F32)
    for b in range(N_BUCKETS):
        out = jnp.where(bkt == b, tbl_ref[b, h], out)
    o_ref[...] = out


def _bias_tiles(rel_table, buckets):
    r, c = buckets.shape
    return pl.pallas_call(
        _bias_kernel,
        grid=(H_A,),
        in_specs=[pl.BlockSpec(memory_space=pltpu.SMEM),
                  pl.BlockSpec((r, c), lambda h: (0, 0))],
        out_specs=pl.BlockSpec((None, r, c), lambda h: (h, 0, 0)),
        out_shape=jax.ShapeDtypeStruct((H_A, r, c), F32),
        name="bias_tiles",
    )(rel_table, buckets)


def _two_maps(q):
    lane = lax.broadcasted_iota(jnp.int32, q.shape, 1)
    zero = jnp.zeros_like(q)
    return jnp.concatenate([jnp.where(lane < DH, q, zero), jnp.where(lane >= DH, q, zero)], axis=0)


def _diff_lambda(lam_ref, lam_init):
    p = lam_ref[...]
    s1 = jnp.sum(p[0:1] * p[1:2], axis=1, keepdims=True)
    s2 = jnp.sum(p[2:3] * p[3:4], axis=1, keepdims=True)
    return jnp.exp(s1) - jnp.exp(s2) + lam_init


def _diff_finish(acc, l, t, lam, sg, lam_init):
    o = acc[:t] / l[:t] - lam * (acc[t:] / l[t:])
    o = o * lax.rsqrt(jnp.mean(o * o, axis=-1, keepdims=True) + EPS) * sg
    return o * (1.0 - lam_init)


def _diff_prompt_kernel(tbl_ref, q_ref, k_ref, v_ref, bias_ref, lam_ref, sg_ref, o_ref,
                        m_ref, l_ref, acc_ref, *, tq, far_bucket, lam_init):
    h = pl.program_id(1)
    i = pl.program_id(2)
    qq = _two_maps(q_ref[...])
    m_ref[...] = jnp.full(m_ref.shape, NEG, F32)
    l_ref[...] = jnp.zeros(l_ref.shape, F32)
    acc_ref[...] = jnp.zeros(acc_ref.shape, F32)

    def update(j, bias):
        rows = pl.ds(pl.multiple_of(j * tq, tq), tq)
        s = _dot_nt(qq, k_ref[rows, :])
        if bias is not None:
            s = s + jnp.concatenate([bias, bias], axis=0)
        m_old = m_ref[...]
        m_new = jnp.maximum(m_old, jnp.max(s, axis=1, keepdims=True))
        alpha = jnp.exp(m_old - m_new)
        p = jnp.exp(s - m_new)
        l_ref[...] = alpha * l_ref[...] + jnp.sum(p, axis=1, keepdims=True)
        acc_ref[...] = alpha * acc_ref[...] + _dot(p.astype(BF16), v_ref[rows, :])
        m_ref[...] = m_new

    def far_body(j, carry):
        update(j, None)
        return carry

    lax.fori_loop(0, jnp.maximum(i - 1, 0), far_body, 0)
    m_ref[...] = m_ref[...] + tbl_ref[far_bucket, h]

    @pl.when(i >= 1)
    def _():
        update(i - 1, bias_ref[1])

    update(i, bias_ref[0])
    lam = _diff_lambda(lam_ref, lam_init)
    o = _diff_finish(acc_ref[...], l_ref[...], tq, lam, sg_ref[...], lam_init)
    o_ref[...] = o.astype(o_ref.dtype)


def _diff_prompt(rel_table, qa, ka, va, bias, lam_p, sg, tq, far_bucket, lam_init):
    b, t, _ = qa.shape
    kern = functools.partial(_diff_prompt_kernel, tq=tq, far_bucket=far_bucket, lam_init=lam_init)
    return pl.pallas_call(
        kern,
        grid=(b, H_A, t // tq),
        in_specs=[pl.BlockSpec(memory_space=pltpu.SMEM),
                  pl.BlockSpec((None, tq, SLAB), lambda b, h, i: (b, i, h)),
                  pl.BlockSpec((None, t, SLAB), lambda b, h, i: (b, 0, h)),
                  pl.BlockSpec((None, t, SLAB), lambda b, h, i: (b, 0, h)),
                  pl.BlockSpec((None, 2, tq, tq), lambda b, h, i: (h, 0, 0, 0)),
                  pl.BlockSpec(lam_p.shape, lambda b, h, i: (0, 0)),
                  pl.BlockSpec((1, SLAB), lambda b, h, i: (0, 0))],
        out_specs=pl.BlockSpec((None, tq, SLAB), lambda b, h, i: (b, i, h)),
        out_shape=jax.ShapeDtypeStruct((b, t, COLS), BF16),
        scratch_shapes=[pltpu.VMEM((2 * tq, 1), F32),
                        pltpu.VMEM((2 * tq, 1), F32),
                        pltpu.VMEM((2 * tq, SLAB), F32)],
        compiler_params=pltpu.CompilerParams(
            dimension_semantics=("arbitrary", "arbitrary", "arbitrary"),
            vmem_limit_bytes=VMEM_LIMIT),
        name="diff_prompt",
    )(rel_table, qa, ka, va, bias, lam_p, sg)


def _diff_sample_kernel(q_ref, kc_ref, vc_ref, kn_ref, vn_ref, bc_ref, bn_ref, lam_ref, sg_ref,
                        o_ref, *, lam_init):
    qq = _two_maps(q_ref[...])
    t = q_ref.shape[0]
    bc = bc_ref[...]
    bn = bn_ref[...]
    s_c = _dot_nt(qq, kc_ref[...].astype(BF16)) + jnp.concatenate([bc, bc], axis=0)
    s_n = _dot_nt(qq, kn_ref[...]) + jnp.concatenate([bn, bn], axis=0)
    m = jnp.maximum(jnp.max(s_c, axis=1, keepdims=True), jnp.max(s_n, axis=1, keepdims=True))
    p_c = jnp.exp(s_c - m)
    p_n = jnp.exp(s_n - m)
    l = jnp.sum(p_c, axis=1, keepdims=True) + jnp.sum(p_n, axis=1, keepdims=True)
    acc = _dot(p_c.astype(BF16), vc_ref[...].astype(BF16)) + _dot(p_n.astype(BF16), vn_ref[...])
    lam = _diff_lambda(lam_ref, lam_init)
    o = _diff_finish(acc, l, t, lam, sg_ref[...], lam_init)
    o_ref[...] = o.astype(o_ref.dtype)


def _diff_sample(qa, kc, vc, kn, vn, bias_c, bias_n, lam_p, sg, lam_init):
    b, t, _ = qa.shape
    p = kc.shape[1]
    tn = kn.shape[1]
    slab = lambda rows: pl.BlockSpec((None, rows, SLAB), lambda b, h: (b, 0, h))
    return pl.pallas_call(
        functools.partial(_diff_sample_kernel, lam_init=lam_init),
        grid=(b, H_A),
        in_specs=[slab(t), slab(p), slab(p), slab(tn), slab(tn),
                  pl.BlockSpec((None, t, p), lambda b, h: (h, 0, 0)),
                  pl.BlockSpec((None, t, tn), lambda b, h: (h, 0, 0)),
                  pl.BlockSpec(lam_p.shape, lambda b, h: (0, 0)),
                  pl.BlockSpec((1, SLAB), lambda b, h: (0, 0))],
        out_specs=slab(t),
        out_shape=jax.ShapeDtypeStruct((b, t, COLS), BF16),
        compiler_params=pltpu.CompilerParams(
            dimension_semantics=("arbitrary", "arbitrary"), vmem_limit_bytes=VMEM_LIMIT),
        name="diff_sample",
    )(qa, kc, vc, kn, vn, bias_c, bias_n, lam_p, sg)


def _one_head(q, hh):
    lane = lax.broadcasted_iota(jnp.int32, q.shape, 1)
    keep = (lane >= hh * DH) & (lane < (hh + 1) * DH)
    return jnp.where(keep, q, jnp.zeros_like(q))


def _sb_tile(qh, k, v, u, c, acc, allowed):
    z = _dot_nt(qh, k)
    sp = jnp.log1p(jnp.exp(-jnp.abs(z)))
    log_beta = jnp.minimum(z, 0.0) - sp
    log_keep = -jnp.maximum(z, 0.0) - sp
    if allowed is not None:
        log_keep = jnp.where(allowed, log_keep, 0.0)
    hi, lo = _split_bf16(log_keep)
    later = _dot(hi, u) + _dot(lo, u)
    a = jnp.exp(log_beta + later + c)
    if allowed is not None:
        a = jnp.where(allowed, a, 0.0)
    acc = acc + _dot(a.astype(BF16), v)
    c = c + jnp.sum(log_keep, axis=1, keepdims=True)
    return c, acc


def _strictly_causal(shape):
    row = lax.broadcasted_iota(jnp.int32, shape, 0)
    col = lax.broadcasted_iota(jnp.int32, shape, 1)
    return col < row


def _merge_heads(outs):
    lane = lax.broadcasted_iota(jnp.int32, outs[0].shape, 1)
    return jnp.where(lane < DH, outs[0], outs[1])


def _sb_prompt_kernel(q_ref, k_ref, v_ref, u_ref, o_ref, *, tq):
    i = pl.program_id(2)
    q = q_ref[...]
    u = u_ref[...]
    allowed = _strictly_causal((tq, tq))
    outs = []
    for hh in range(2):
        qh = _one_head(q, hh)

        def tile(j, c, acc, mask):
            rows = pl.ds(pl.multiple_of(j * tq, tq), tq)
            return _sb_tile(qh, k_ref[rows, :], v_ref[rows, :], u, c, acc, mask)

        c, acc = tile(i, jnp.zeros((tq, 1), F32), jnp.zeros((tq, SLAB), F32), allowed)

        def alive(c):
            return (jnp.max(c) > SB_DEAD).astype(jnp.int32)

        def cond(st):
            j, live, _, _ = st
            return (j >= 0) & (live > 0)

        def body(st):
            j, _, c, acc = st
            c, acc = tile(j, c, acc, None)
            return j - 1, alive(c), c, acc

        _, _, c, acc = lax.while_loop(cond, body, (i - 1, alive(c), c, acc))
        outs.append(acc)
    o_ref[...] = _merge_heads(outs).astype(o_ref.dtype)


def _sb_prompt(qb, kb, vb, u, tq):
    b, t, _ = qb.shape
    return pl.pallas_call(
        functools.partial(_sb_prompt_kernel, tq=tq),
        grid=(b, N_SLAB_B, t // tq),
        in_specs=[pl.BlockSpec((None, tq, SLAB), lambda b, h, i: (b, i, h)),
                  pl.BlockSpec((None, t, SLAB), lambda b, h, i: (b, 0, h)),
                  pl.BlockSpec((None, t, SLAB), lambda b, h, i: (b, 0, h)),
                  pl.BlockSpec(u.shape, lambda b, h, i: (0, 0))],
        out_specs=pl.BlockSpec((None, tq, SLAB), lambda b, h, i: (b, i, h)),
        out_shape=jax.ShapeDtypeStruct((b, t, COLS), BF16),
        compiler_params=pltpu.CompilerParams(
            dimension_semantics=("arbitrary", "arbitrary", "arbitrary"),
            vmem_limit_bytes=VMEM_LIMIT),
        name="sb_prompt",
    )(qb, kb, vb, u)


def _sb_sample_kernel(q_ref, kc_ref, vc_ref, kn_ref, vn_ref, u_ref, un_ref, o_ref, *, tk):
    q = q_ref[...]
    t = q.shape[0]
    tn = kn_ref.shape[0]
    n_tiles = kc_ref.shape[0] // tk
    u = u_ref[...]
    allowed = _strictly_causal((t, tn))
    outs = []
    for hh in range(2):
        qh = _one_head(q, hh)
        c, acc = _sb_tile(qh, kn_ref[...], vn_ref[...], un_ref[...],
                          jnp.zeros((t, 1), F32), jnp.zeros((t, SLAB), F32), allowed)

        def body(n, st):
            c, acc = st
            rows = pl.ds(pl.multiple_of((n_tiles - 1 - n) * tk, tk), tk)
            return _sb_tile(qh, kc_ref[rows, :].astype(BF16), vc_ref[rows, :].astype(BF16),
                            u, c, acc, None)

        c, acc = lax.fori_loop(0, n_tiles, body, (c, acc))
        outs.append(acc)
    o_ref[...] = _merge_heads(outs).astype(o_ref.dtype)


def _sb_sample(qb, kc, vc, kn, vn, u, un, tk):
    b, t, _ = qb.shape
    p = kc.shape[1]
    tn = kn.shape[1]
    slab = lambda rows: pl.BlockSpec((None, rows, SLAB), lambda b, h: (b, 0, h))
    return pl.pallas_call(
        functools.partial(_sb_sample_kernel, tk=tk),
        grid=(b, N_SLAB_B),
        in_specs=[slab(t), slab(p), slab(p), slab(tn), slab(tn),
                  pl.BlockSpec(u.shape, lambda b, h: (0, 0)),
                  pl.BlockSpec(un.shape, lambda b, h: (0, 0))],
        out_specs=slab(t),
        out_shape=jax.ShapeDtypeStruct((b, t, COLS), BF16),
        compiler_params=pltpu.CompilerParams(
            dimension_semantics=("arbitrary", "arbitrary"), vmem_limit_bytes=VMEM_LIMIT),
        name="sb_sample",
    )(qb, kc, vc, kn, vn, u, un)


def _upper_sum_matrix(n):
    j = np.arange(n)[:, None]
    s = np.arange(n)[None, :]
    return jnp.asarray((j > s).astype(np.float32), dtype=BF16)


def _route(logits):
    lane = lax.broadcasted_iota(jnp.int32, logits.shape, 1).astype(F32)
    big = float(ROUTER_LANES)

    def first_argmax(vals):
        top = jnp.max(vals, axis=1, keepdims=True)
        idx = jnp.min(jnp.where(vals == top, lane, big), axis=1, keepdims=True)
        return top, idx

    _, g_sel = first_argmax(jnp.where(lane < N_GROUPS, logits, NEG))
    first = N_GROUPS + g_sel * EXPERTS_PER_GROUP
    in_group = (lane >= first) & (lane < first + EXPERTS_PER_GROUP)
    el = jnp.where(in_group, logits, NEG)
    v1, i1 = first_argmax(el)
    v2, i2 = first_argmax(jnp.where(lane == i1, NEG, el))
    t = jnp.exp(v2 - v1)
    w1 = 1.0 / (1.0 + t)
    w2 = t / (1.0 + t)
    return jnp.where(lane == i1, w1, jnp.where(lane == i2, w2, 0.0))


def _tail_kernel(x_ref, ya_ref, yb_ref, gmix_ref, wg_ref, bg_ref, wod_ref, wos_ref, wout_ref,
                 gffn_ref, wrh_ref, wrl_ref, br_ref, x2_ref, h2_ref, comb_ref):
    x = x_ref[...]
    d = x.shape[1]
    h = _rms(x, gmix_ref[...]).astype(BF16)
    gates = _sigmoid(_dot(h, wg_ref[...]) + bg_ref[...])
    mix = gates[:, :d] * _dot(ya_ref[...], wod_ref[...]) + gates[:, d:] * _dot(yb_ref[...], wos_ref[...])
    x2 = x + _dot(mix.astype(BF16), wout_ref[...])
    x2_ref[...] = x2
    h2 = _rms(x2, gffn_ref[...])
    h2_hi, h2_lo = _split_bf16(h2)
    h2_ref[...] = h2_hi
    wrh = wrh_ref[...]
    logits = _dot(h2_hi, wrh) + _dot(h2_hi, wrl_ref[...]) + _dot(h2_lo, wrh) + br_ref[...]
    comb_ref[...] = _route(logits)


def _tail(x2d, ya, yb, g_mix, wg, bg, wod, wos, wout, g_ffn, wrh, wrl, br, tm):
    n, d = x2d.shape
    row = lambda i: (i, 0)
    const = lambda i: (0, 0)
    full = lambda a: pl.BlockSpec(a.shape, const)
    return pl.pallas_call(
        _tail_kernel,
        grid=(n // tm,),
        in_specs=[pl.BlockSpec((tm, d), row),
                  pl.BlockSpec((tm, COLS), row),
                  pl.BlockSpec((tm, COLS), row),
                  full(g_mix), full(wg), full(bg), full(wod), full(wos), full(wout),
                  full(g_ffn), full(wrh), full(wrl), full(br)],
        out_specs=[pl.BlockSpec((tm, d), row),
                   pl.BlockSpec((tm, d), row),
                   pl.BlockSpec((tm, ROUTER_LANES), row)],
        out_shape=[jax.ShapeDtypeStruct((n, d), F32),
                   jax.ShapeDtypeStruct((n, d), BF16),
                   jax.ShapeDtypeStruct((n, ROUTER_LANES), F32)],
        compiler_params=pltpu.CompilerParams(
            dimension_semantics=("arbitrary",), vmem_limit_bytes=VMEM_LIMIT),
        name="tail",
    )(x2d, ya, yb, g_mix, wg, bg, wod, wos, wout, g_ffn, wrh, wrl, br)


def _moe_kernel(x2_ref, h2_ref, comb_ref, mg_ref, mu_ref, md_ref, y_ref):
    e = pl.program_id(1)

    @pl.when(e == 0)
    def _():
        y_ref[...] = x2_ref[...]

    h2 = h2_ref[...]
    g = _dot(h2, mg_ref[...])
    act = g * _sigmoid(g) * _dot(h2, mu_ref[...])
    comb = comb_ref[...]
    lane = lax.broadcasted_iota(jnp.int32, comb.shape, 1)
    w = jnp.sum(jnp.where(lane == e + N_GROUPS, comb, 0.0), axis=1, keepdims=True)
    y_ref[...] += w * _dot(act.astype(BF16), md_ref[...])


def _moe(x2, h2, comb, mg, mu, md, tm):
    n, d = x2.shape
    de = mg.shape[2]
    tok = lambda i, e: (i, 0)
    return pl.pallas_call(
        _moe_kernel,
        grid=(n // tm, N_EXPERTS),
        in_specs=[pl.BlockSpec((tm, d), tok),
                  pl.BlockSpec((tm, d), tok),
                  pl.BlockSpec((tm, ROUTER_LANES), tok),
                  pl.BlockSpec((None, d, de), lambda i, e: (e, 0, 0)),
                  pl.BlockSpec((None, d, de), lambda i, e: (e, 0, 0)),
                  pl.BlockSpec((None, de, d), lambda i, e: (e, 0, 0))],
        out_specs=pl.BlockSpec((tm, d), tok),
        out_shape=jax.ShapeDtypeStruct((n, d), F32),
        compiler_params=pltpu.CompilerParams(
            dimension_semantics=("arbitrary", "arbitrary"), vmem_limit_bytes=VMEM_LIMIT),
        name="moe",
    )(x2, h2, comb, mg, mu, md)


def _tile(n, pref):
    return pref if n % pref == 0 else n


def kernel(x_prompt, x_sample, cache_diff_k, cache_diff_v, cache_sb_k, cache_sb_v, rel_bias_table, norm_mix_g, w_in, q_norm_g, k_norm_g, lambda_q1, lambda_k1, lambda_q2, lambda_k2, subln_g, w_o_diff, w_o_sb, w_branch_gate, b_branch_gate, w_out, norm_ffn_g, w_router_group, b_router_group, w_router_expert, b_router_expert, moe_w_gate, moe_w_up, moe_w_down):
    b, t, d = x_prompt.shape
    bs, ts, _ = x_sample.shape
    depth = norm_mix_g.shape[0]
    p = cache_diff_k.shape[2]
    tq = _tile(t, 256)
    tk_s = _tile(p, 256)
    tn = 128
    assert t % tq == 0 and tq % CHUNK == 0 and p % CHUNK == 0 and ts <= CHUNK and ts <= tn
    assert p % tk_s == 0

    q_loc = np.arange(tq)
    bkt_prompt = np.concatenate([_bucket_tile_np(q_loc + tq, np.arange(tq) + tq),
                                 _bucket_tile_np(q_loc + tq, np.arange(tq))], axis=0)
    far = np.unique(_t5_bucket_np(-np.arange(tq + 1, 4 * tq)))
    assert far.size == 1
    far_bucket = int(far[0])
    q_s = p + np.arange(ts)
    k_s = np.concatenate([np.arange(p), p + np.arange(tn)])
    bkt_sample = _bucket_tile_np(q_s, k_s)
    bkt_sample[:, p + ts:] = -1
    bias_prompt = _bias_tiles(rel_bias_table, jnp.asarray(bkt_prompt)).reshape(H_A, 2, tq, tq)
    bias_sample = _bias_tiles(rel_bias_table, jnp.asarray(bkt_sample))
    bias_sc, bias_sn = bias_sample[:, :, :p], bias_sample[:, :, p:]

    gm = np.kron(np.eye(256 // DH), np.full((DH, DH), 1.0 / DH)).astype(np.float32)
    gm = jnp.asarray(gm, dtype=BF16)
    u_p = _upper_sum_matrix(tq)
    u_s = _upper_sum_matrix(tk_s)
    u_n = _upper_sum_matrix(tn)

    y_p = x_prompt.reshape(b * t, d)
    y_s = x_sample.reshape(bs * ts, d)
    tm_p = _tile(b * t, 256)
    tm_s = _tile(bs * ts, 128)
    tm_moe_p = _tile(b * t, 1024)
    outs = [[] for _ in range(8)]
    for l in range(depth):
        lam_init = 0.8 - 0.6 * math.exp(-0.3 * l)
        w_in_b = w_in[l].astype(BF16)
        g_mix = norm_mix_g[l].reshape(1, d)
        gq = jnp.tile(q_norm_g[l], COLS // DH).reshape(1, COLS)
        gk = jnp.tile(k_norm_g[l], COLS // DH).reshape(1, COLS)
        lam_p = jnp.stack([lambda_q1[l], lambda_k1[l], lambda_q2[l], lambda_k2[l]])
        sg = subln_g[l].reshape(1, SLAB)
        wg = w_branch_gate[l].astype(BF16)
        bg = b_branch_gate[l].reshape(1, 2 * d)
        wod = w_o_diff[l].astype(BF16)
        wos = w_o_sb[l].astype(BF16)
        wout = w_out[l].astype(BF16)
        g_ffn = norm_ffn_g[l].reshape(1, d)
        n_r = N_GROUPS + N_EXPERTS
        wr = jnp.pad(jnp.concatenate([w_router_group[l], w_router_expert[l]], axis=1),
                     ((0, 0), (0, ROUTER_LANES - n_r)))
        wrh = wr.astype(BF16)
        wrl = (wr - wrh.astype(F32)).astype(BF16)
        br = jnp.pad(jnp.concatenate([b_router_group[l], b_router_expert[l]]),
                     (0, ROUTER_LANES - n_r)).reshape(1, ROUTER_LANES)
        mg = moe_w_gate[l].astype(BF16)
        mu = moe_w_up[l].astype(BF16)
        md = moe_w_down[l].astype(BF16)

        def tail_and_moe(x2d, ya, yb, tm, tm_moe):
            x2, h2, comb = _tail(x2d, ya, yb, g_mix, wg, bg, wod, wos, wout, g_ffn, wrh, wrl, br, tm)
            return _moe(x2, h2, comb, mg, mu, md, tm_moe)

        ka, va, kb, vb, qa_b, ka_b, va_b, qb_b, kb_b, vb_b = _project(y_p, g_mix, w_in_b, gq, gk, gm, tm_p)
        r3 = lambda a: a.reshape(b, t, COLS)
        ya = _diff_prompt(rel_bias_table, r3(qa_b), r3(ka_b), r3(va_b), bias_prompt, lam_p, sg,
                          tq, far_bucket, lam_init)
        yb = _sb_prompt(r3(qb_b), r3(kb_b), r3(vb_b), u_p, tq)
        y_p = tail_and_moe(y_p, ya.reshape(b * t, COLS), yb.reshape(b * t, COLS), tm_p, tm_moe_p)
        outs[0].append(ka.reshape(b, t, H_A, 2, DH))
        outs[1].append(va.reshape(b, t, H_A, 2 * DH))
        outs[2].append(kb.reshape(b, t, H_B, DH))
        outs[3].append(vb.reshape(b, t, H_B, DH))

        ka, va, kb, vb, qa_b, ka_b, va_b, qb_b, kb_b, vb_b = _project(y_s, g_mix, w_in_b, gq, gk, gm, tm_s)
        s3 = lambda a: a.reshape(bs, ts, COLS)
        padk = lambda a: jnp.pad(s3(a), ((0, 0), (0, tn - ts), (0, 0)))
        ya = _diff_sample(s3(qa_b), cache_diff_k[l].reshape(bs, p, COLS), cache_diff_v[l].reshape(bs, p, COLS),
                          padk(ka_b), padk(va_b), bias_sc, bias_sn, lam_p, sg, lam_init)
        yb = _sb_sample(s3(qb_b), cache_sb_k[l].reshape(bs, p, COLS), cache_sb_v[l].reshape(bs, p, COLS),
                        padk(kb_b), padk(vb_b), u_s, u_n, tk_s)
        y_s = tail_and_moe(y_s, ya.reshape(bs * ts, COLS), yb.reshape(bs * ts, COLS), tm_s, tm_s)
        outs[4].append(ka.reshape(bs, ts, H_A, 2, DH))
        outs[5].append(va.reshape(bs, ts, H_A, 2 * DH))
        outs[6].append(kb.reshape(bs, ts, H_B, DH))
        outs[7].append(vb.reshape(bs, ts, H_B, DH))

    return (y_p.reshape(b, t, d), y_s.reshape(bs, ts, d)) + tuple(jnp.stack(o) for o in outs)
```

```python
import functools
import math

import numpy as np
import jax
import jax.numpy as jnp
from jax import lax
from jax.experimental import pallas as pl
from jax.experimental.pallas import tpu as pltpu

F32 = jnp.float32
BF16 = jnp.bfloat16

EPS = 1e-6
CHUNK = 64
H_A = 4
DH = 64
H_B = 8
SLAB = 2 * DH
N_SLAB_A = H_A
N_SLAB_B = H_B // 2
COLS = H_A * SLAB
N_BUCKETS = 32
MAX_DISTANCE = 128
N_GROUPS = 4
EXPERTS_PER_GROUP = 4
N_EXPERTS = N_GROUPS * EXPERTS_PER_GROUP
ROUTER_LANES = 128
NEG = -1e30
SB_DEAD = -104.0
VMEM_LIMIT = 48 * 1024 * 1024


def _rms(x, g):
    return x * lax.rsqrt(jnp.mean(x * x, axis=-1, keepdims=True) + EPS) * g


def _sigmoid(x):
    return 1.0 / (1.0 + jnp.exp(-x))


def _dot(a, b):
    return jnp.dot(a, b, preferred_element_type=F32)


def _dot_nt(a, b):
    return lax.dot_general(a, b, (((1,), (1,)), ((), ())), preferred_element_type=F32)


def _split_bf16(x):
    hi = x.astype(BF16)
    lo = (x - hi.astype(F32)).astype(BF16)
    return hi, lo


def _proj_kernel(x_ref, g_ref, w_ref, gq_ref, gk_ref, gm_ref,
                 ka_o, va_o, kb_o, vb_o, qa_b, ka_b, va_b, qb_b, kb_b, vb_b):
    h = _rms(x_ref[...], g_ref[...]).astype(BF16)
    proj = _dot(h, w_ref[...])
    gm = gm_ref[...]

    def sub_head_norm(t, gain):
        outs = []
        for s in range(0, COLS, 256):
            ts = t[:, s:s + 256]
            hi, lo = _split_bf16(ts * ts)
            msq = _dot(hi, gm) + _dot(lo, gm)
            outs.append(ts * lax.rsqrt(msq + EPS))
        return jnp.concatenate(outs, axis=1) * gain

    qa = sub_head_norm(proj[:, 0 * COLS:1 * COLS], gq_ref[...])
    ka = sub_head_norm(proj[:, 1 * COLS:2 * COLS], gk_ref[...])
    va = proj[:, 2 * COLS:3 * COLS]
    qb = proj[:, 3 * COLS:4 * COLS]
    kb = proj[:, 4 * COLS:5 * COLS]
    vb = proj[:, 5 * COLS:6 * COLS]
    ka_o[...] = ka
    va_o[...] = va
    kb_o[...] = kb
    vb_o[...] = vb
    scale = 1.0 / math.sqrt(DH)
    qa_b[...] = (qa * scale).astype(BF16)
    ka_b[...] = ka.astype(BF16)
    va_b[...] = va.astype(BF16)
    qb_b[...] = (qb * scale).astype(BF16)
    kb_b[...] = kb.astype(BF16)
    vb_b[...] = vb.astype(BF16)


def _project(x2d, g_mix, w_in_b, gq, gk, gm, tm):
    n, d = x2d.shape
    row = lambda i: (i, 0)
    const = lambda i: (0, 0)
    out_f = jax.ShapeDtypeStruct((n, COLS), F32)
    out_b = jax.ShapeDtypeStruct((n, COLS), BF16)
    blk = pl.BlockSpec((tm, COLS), row)
    return pl.pallas_call(
        _proj_kernel,
        grid=(n // tm,),
        in_specs=[pl.BlockSpec((tm, d), row),
                  pl.BlockSpec((1, d), const),
                  pl.BlockSpec(w_in_b.shape, const),
                  pl.BlockSpec((1, COLS), const),
                  pl.BlockSpec((1, COLS), const),
                  pl.BlockSpec(gm.shape, const)],
        out_specs=[blk] * 10,
        out_shape=[out_f] * 4 + [out_b] * 6,
        compiler_params=pltpu.CompilerParams(
            dimension_semantics=("arbitrary",), vmem_limit_bytes=VMEM_LIMIT),
        name="proj",
    )(x2d, g_mix, w_in_b, gq, gk, gm)


def _t5_bucket_np(rel):
    half = N_BUCKETS // 2
    max_exact = half // 2
    base = np.where(rel > 0, half, 0)
    n = np.abs(rel)
    nf = np.maximum(n, 1).astype(np.float64)
    large = max_exact + (np.log(nf / max_exact) / math.log(MAX_DISTANCE / max_exact)
                         * (half - max_exact)).astype(np.int32)
    large = np.minimum(large, half - 1)
    return (base + np.where(n < max_exact, n, large)).astype(np.int32)


def _bucket_tile_np(q_pos, k_pos):
    rel = k_pos[None, :] - q_pos[:, None]
    allowed = (k_pos[None, :] // CHUNK) <= (q_pos[:, None] // CHUNK)
    return np.where(allowed, _t5_bucket_np(rel), -1).astype(np.int32)


def _bias_kernel(tbl_ref, bkt_ref, o_ref):
    h = pl.program_id(0)
    bkt = bkt_ref[...]
    out = jnp.full(bkt.shape, NEG, F32)
    for b in range(N_BUCKETS):
        out = jnp.where(bkt == b, tbl_ref[b, h], out)
    o_ref[...] = out


def _bias_tiles(rel_table, buckets):
    r, c = buckets.shape
    return pl.pallas_call(
        _bias_kernel,
        grid=(H_A,),
        in_specs=[pl.BlockSpec(memory_space=pltpu.SMEM),
                  pl.BlockSpec((r, c), lambda h: (0, 0))],
        out_specs=pl.BlockSpec((None, r, c), lambda h: (h, 0, 0)),
        out_shape=jax.ShapeDtypeStruct((H_A, r, c), F32),
        name="bias_tiles",
    )(rel_table, buckets)


def _two_maps(q):
    lane = lax.broadcasted_iota(jnp.int32, q.shape, 1)
    zero = jnp.zeros_like(q)
    return jnp.concatenate([jnp.where(lane < DH, q, zero), jnp.where(lane >= DH, q, zero)], axis=0)


def _diff_lambda(lam_ref, lam_init):
    p = lam_ref[...]
    s1 = jnp.sum(p[0:1] * p[1:2], axis=1, keepdims=True)
    s2 = jnp.sum(p[2:3] * p[3:4], axis=1, keepdims=True)
    return jnp.exp(s1) - jnp.exp(s2) + lam_init


def _diff_finish(acc, l, t, lam, sg, lam_init):
    o = acc[:t] / l[:t] - lam * (acc[t:] / l[t:])
    o = o * lax.rsqrt(jnp.mean(o * o, axis=-1, keepdims=True) + EPS) * sg
    return o * (1.0 - lam_init)


def _diff_prompt_kernel(tbl_ref, q_ref, k_ref, v_ref, bias_ref, lam_ref, sg_ref, o_ref,
                        vx_ref, m_ref, l_ref, acc_ref, *, tq, tk, far_bucket, lam_init):
    h = pl.program_id(1)
    i = pl.program_id(2)

    @pl.when(i == 0)
    def _():
        vx_ref[:, :SLAB] = v_ref[...]
        vx_ref[:, SLAB:] = jnp.ones((vx_ref.shape[0], SLAB), vx_ref.dtype)

    qq = _two_maps(q_ref[...])
    m_ref[...] = jnp.full(m_ref.shape, NEG, F32)
    l_ref[...] = jnp.zeros(l_ref.shape, F32)
    acc_ref[...] = jnp.zeros(acc_ref.shape, F32)

    n_sub = tq // tk

    def update(jk, bias):
        rows = pl.ds(pl.multiple_of(jk * tk, tk), tk)
        s = _dot_nt(qq, k_ref[rows, :])
        if bias is not None:
            s = s + jnp.concatenate([bias, bias], axis=0)
        m_old = m_ref[...]
        m_new = jnp.maximum(m_old, jnp.max(s, axis=1, keepdims=True))
        alpha = jnp.exp(m_old - m_new)
        p = jnp.exp(s - pltpu.repeat(m_new, tk // SLAB, axis=1))
        pv = _dot(p.astype(BF16), vx_ref[rows, :])
        l_ref[...] = alpha * l_ref[...] + pv[:, SLAB:]
        acc_ref[...] = alpha * acc_ref[...] + pv[:, :SLAB]
        m_ref[...] = m_new

    def far_body(j, carry):
        for sub in range(n_sub):
            update(j * n_sub + sub, None)
        return carry

    lax.fori_loop(0, jnp.maximum(i - 1, 0), far_body, 0)
    m_ref[...] = m_ref[...] + tbl_ref[far_bucket, h]

    @pl.when(i >= 1)
    def _():
        for sub in range(n_sub):
            update((i - 1) * n_sub + sub, bias_ref[1, :, sub * tk:(sub + 1) * tk])

    for sub in range(n_sub):
        update(i * n_sub + sub, bias_ref[0, :, sub * tk:(sub + 1) * tk])
    lam = _diff_lambda(lam_ref, lam_init)
    o = _diff_finish(acc_ref[...], l_ref[...], tq, lam, sg_ref[...], lam_init)
    o_ref[...] = o.astype(o_ref.dtype)


def _diff_prompt(rel_table, qa, ka, va, bias, lam_p, sg, tq, far_bucket, lam_init):
    b, t, _ = qa.shape
    kern = functools.partial(_diff_prompt_kernel, tq=tq, tk=_tile(tq, 256), far_bucket=far_bucket,
                             lam_init=lam_init)
    return pl.pallas_call(
        kern,
        grid=(b, H_A, t // tq),
        in_specs=[pl.BlockSpec(memory_space=pltpu.SMEM),
                  pl.BlockSpec((None, tq, SLAB), lambda b, h, i: (b, i, h)),
                  pl.BlockSpec((None, t, SLAB), lambda b, h, i: (b, 0, h)),
                  pl.BlockSpec((None, t, SLAB), lambda b, h, i: (b, 0, h)),
                  pl.BlockSpec((None, 2, tq, tq), lambda b, h, i: (h, 0, 0, 0)),
                  pl.BlockSpec(lam_p.shape, lambda b, h, i: (0, 0)),
                  pl.BlockSpec((1, SLAB), lambda b, h, i: (0, 0))],
        out_specs=pl.BlockSpec((None, tq, SLAB), lambda b, h, i: (b, i, h)),
        out_shape=jax.ShapeDtypeStruct((b, t, COLS), BF16),
        scratch_shapes=[pltpu.VMEM((t, 2 * SLAB), BF16),
                        pltpu.VMEM((2 * tq, SLAB), F32),
                        pltpu.VMEM((2 * tq, SLAB), F32),
                        pltpu.VMEM((2 * tq, SLAB), F32)],
        compiler_params=pltpu.CompilerParams(
            dimension_semantics=("arbitrary", "arbitrary", "arbitrary"),
            vmem_limit_bytes=VMEM_LIMIT),
        name="diff_prompt",
    )(rel_table, qa, ka, va, bias, lam_p, sg)


def _diff_sample_kernel(q_ref, kc_ref, vc_ref, kn_ref, vn_ref, bc_ref, bn_ref, lam_ref, sg_ref,
                        o_ref, *, lam_init):
    qq = _two_maps(q_ref[...])
    t = q_ref.shape[0]
    bc = bc_ref[...]
    bn = bn_ref[...]
    s_c = _dot_nt(qq, kc_ref[...].astype(BF16)) + jnp.concatenate([bc, bc], axis=0)
    s_n = _dot_nt(qq, kn_ref[...]) + jnp.concatenate([bn, bn], axis=0)
    m = jnp.maximum(jnp.max(s_c, axis=1, keepdims=True), jnp.max(s_n, axis=1, keepdims=True))
    p_c = jnp.exp(s_c - m)
    p_n = jnp.exp(s_n - m)
    l = jnp.sum(p_c, axis=1, keepdims=True) + jnp.sum(p_n, axis=1, keepdims=True)
    acc = _dot(p_c.astype(BF16), vc_ref[...].astype(BF16)) + _dot(p_n.astype(BF16), vn_ref[...])
    lam = _diff_lambda(lam_ref, lam_init)
    o = _diff_finish(acc, l, t, lam, sg_ref[...], lam_init)
    o_ref[...] = o.astype(o_ref.dtype)


def _diff_sample(qa, kc, vc, kn, vn, bias_c, bias_n, lam_p, sg, lam_init):
    b, t, _ = qa.shape
    p = kc.shape[1]
    tn = kn.shape[1]
    slab = lambda rows: pl.BlockSpec((None, rows, SLAB), lambda b, h: (b, 0, h))
    return pl.pallas_call(
        functools.partial(_diff_sample_kernel, lam_init=lam_init),
        grid=(b, H_A),
        in_specs=[slab(t), slab(p), slab(p), slab(tn), slab(tn),
                  pl.BlockSpec((None, t, p), lambda b, h: (h, 0, 0)),
                  pl.BlockSpec((None, t, tn), lambda b, h: (h, 0, 0)),
                  pl.BlockSpec(lam_p.shape, lambda b, h: (0, 0)),
                  pl.BlockSpec((1, SLAB), lambda b, h: (0, 0))],
        out_specs=slab(t),
        out_shape=jax.ShapeDtypeStruct((b, t, COLS), BF16),
        compiler_params=pltpu.CompilerParams(
            dimension_semantics=("arbitrary", "arbitrary"), vmem_limit_bytes=VMEM_LIMIT),
        name="diff_sample",
    )(qa, kc, vc, kn, vn, bias_c, bias_n, lam_p, sg)


def _sb_tile(qq, k, v, u, c, acc, allowed):
    z = _dot_nt(qq, k)
    sp = jnp.log1p(jnp.exp(-jnp.abs(z)))
    log_beta = jnp.minimum(z, 0.0) - sp
    log_keep = -jnp.maximum(z, 0.0) - sp
    if allowed is not None:
        log_keep = jnp.where(allowed, log_keep, 0.0)
    hi, lo = _split_bf16(log_keep)
    later = _dot(hi, u) + _dot(lo, u)
    a = jnp.exp(log_beta + later + pltpu.repeat(c, z.shape[1] // SLAB, axis=1))
    if allowed is not None:
        a = jnp.where(allowed, a, 0.0)
    acc = acc + _dot(a.astype(BF16), v)
    c = c + jnp.sum(log_keep, axis=1, keepdims=True)
    return c, acc


def _strictly_causal_two_heads(t, n_keys):
    row = lax.broadcasted_iota(jnp.int32, (2 * t, n_keys), 0)
    col = lax.broadcasted_iota(jnp.int32, (2 * t, n_keys), 1)
    return col < jnp.where(row >= t, row - t, row)


def _merge_heads(acc, t):
    lane = lax.broadcasted_iota(jnp.int32, (t, SLAB), 1)
    return jnp.where(lane < DH, acc[:t], acc[t:])


def _sb_prompt_kernel(q_ref, k_ref, v_ref, u_ref, o_ref, *, tq):
    i = pl.program_id(2)
    qq = _two_maps(q_ref[...])
    u = u_ref[...]

    def tile(j, c, acc, mask):
        rows = pl.ds(pl.multiple_of(j * tq, tq), tq)
        return _sb_tile(qq, k_ref[rows, :], v_ref[rows, :], u, c, acc, mask)

    zeros = jnp.zeros((2 * tq, SLAB), F32)
    c, acc = tile(i, zeros, zeros, _strictly_causal_two_heads(tq, tq))

    def alive(c):
        return (jnp.max(c) > SB_DEAD).astype(jnp.int32)

    def cond(st):
        j, live, _, _ = st
        return (j >= 0) & (live > 0)

    def body(st):
        j, _, c, acc = st
        c, acc = tile(j, c, acc, None)
        return j - 1, alive(c), c, acc

    _, _, c, acc = lax.while_loop(cond, body, (i - 1, alive(c), c, acc))
    o_ref[...] = _merge_heads(acc, tq).astype(o_ref.dtype)


def _sb_prompt(qb, kb, vb, u, tq):
    b, t, _ = qb.shape
    return pl.pallas_call(
        functools.partial(_sb_prompt_kernel, tq=tq),
        grid=(b, N_SLAB_B, t // tq),
        in_specs=[pl.BlockSpec((None, tq, SLAB), lambda b, h, i: (b, i, h)),
                  pl.BlockSpec((None, t, SLAB), lambda b, h, i: (b, 0, h)),
                  pl.BlockSpec((None, t, SLAB), lambda b, h, i: (b, 0, h)),
                  pl.BlockSpec(u.shape, lambda b, h, i: (0, 0))],
        out_specs=pl.BlockSpec((None, tq, SLAB), lambda b, h, i: (b, i, h)),
        out_shape=jax.ShapeDtypeStruct((b, t, COLS), BF16),
        compiler_params=pltpu.CompilerParams(
            dimension_semantics=("arbitrary", "arbitrary", "arbitrary"),
            vmem_limit_bytes=VMEM_LIMIT),
        name="sb_prompt",
    )(qb, kb, vb, u)


def _sb_sample_kernel(q_ref, kc_ref, vc_ref, kn_ref, vn_ref, u_ref, un_ref, o_ref, *, tk):
    t = q_ref.shape[0]
    tn = kn_ref.shape[0]
    n_tiles = kc_ref.shape[0] // tk
    qq = _two_maps(q_ref[...])
    u = u_ref[...]
    zeros = jnp.zeros((2 * t, SLAB), F32)
    c, acc = _sb_tile(qq, kn_ref[...], vn_ref[...], un_ref[...], zeros, zeros,
                      _strictly_causal_two_heads(t, tn))

    def body(n, st):
        c, acc = st
        rows = pl.ds(pl.multiple_of((n_tiles - 1 - n) * tk, tk), tk)
        return _sb_tile(qq, kc_ref[rows, :].astype(BF16), vc_ref[rows, :].astype(BF16),
                        u, c, acc, None)

    c, acc = lax.fori_loop(0, n_tiles, body, (c, acc))
    o_ref[...] = _merge_heads(acc, t).astype(o_ref.dtype)


def _sb_sample(qb, kc, vc, kn, vn, u, un, tk):
    b, t, _ = qb.shape
    p = kc.shape[1]
    tn = kn.shape[1]
    slab = lambda rows: pl.BlockSpec((None, rows, SLAB), lambda b, h: (b, 0, h))
    return pl.pallas_call(
        functools.partial(_sb_sample_kernel, tk=tk),
        grid=(b, N_SLAB_B),
        in_specs=[slab(t), slab(p), slab(p), slab(tn), slab(tn),
                  pl.BlockSpec(u.shape, lambda b, h: (0, 0)),
                  pl.BlockSpec(un.shape, lambda b, h: (0, 0))],
        out_specs=slab(t),
        out_shape=jax.ShapeDtypeStruct((b, t, COLS), BF16),
        compiler_params=pltpu.CompilerParams(
            dimension_semantics=("arbitrary", "arbitrary"), vmem_limit_bytes=VMEM_LIMIT),
        name="sb_sample",
    )(qb, kc, vc, kn, vn, u, un)


def _upper_sum_matrix(n):
    j = np.arange(n)[:, None]
    s = np.arange(n)[None, :]
    return jnp.asarray((j > s).astype(np.float32), dtype=BF16)


def _route(logits):
    lane = lax.broadcasted_iota(jnp.int32, logits.shape, 1).astype(F32)
    big = float(ROUTER_LANES)

    def first_argmax(vals):
        top = jnp.max(vals, axis=1, keepdims=True)
        idx = jnp.min(jnp.where(vals == top, lane, big), axis=1, keepdims=True)
        return top, idx

    _, g_sel = first_argmax(jnp.where(lane < N_GROUPS, logits, NEG))
    first = N_GROUPS + g_sel * EXPERTS_PER_GROUP
    in_group = (lane >= first) & (lane < first + EXPERTS_PER_GROUP)
    el = jnp.where(in_group, logits, NEG)
    v1, i1 = first_argmax(el)
    v2, i2 = first_argmax(jnp.where(lane == i1, NEG, el))
    t = jnp.exp(v2 - v1)
    w1 = 1.0 / (1.0 + t)
    w2 = t / (1.0 + t)
    return jnp.where(lane == i1, w1, jnp.where(lane == i2, w2, 0.0))


def _tail_kernel(x_ref, ya_ref, yb_ref, gmix_ref, wg_ref, bg_ref, wod_ref, wos_ref, wout_ref,
                 gffn_ref, wrh_ref, wrl_ref, br_ref, x2_ref, h2_ref, comb_ref):
    x = x_ref[...]
    d = x.shape[1]
    h = _rms(x, gmix_ref[...]).astype(BF16)
    gates = _sigmoid(_dot(h, wg_ref[...]) + bg_ref[...])
    mix = gates[:, :d] * _dot(ya_ref[...], wod_ref[...]) + gates[:, d:] * _dot(yb_ref[...], wos_ref[...])
    x2 = x + _dot(mix.astype(BF16), wout_ref[...])
    x2_ref[...] = x2
    h2 = _rms(x2, gffn_ref[...])
    h2_hi, h2_lo = _split_bf16(h2)
    h2_ref[...] = h2_hi
    wrh = wrh_ref[...]
    logits = _dot(h2_hi, wrh) + _dot(h2_hi, wrl_ref[...]) + _dot(h2_lo, wrh) + br_ref[...]
    comb_ref[...] = _route(logits)


def _tail(x2d, ya, yb, g_mix, wg, bg, wod, wos, wout, g_ffn, wrh, wrl, br, tm):
    n, d = x2d.shape
    row = lambda i: (i, 0)
    const = lambda i: (0, 0)
    full = lambda a: pl.BlockSpec(a.shape, const)
    return pl.pallas_call(
        _tail_kernel,
        grid=(n // tm,),
        in_specs=[pl.BlockSpec((tm, d), row),
                  pl.BlockSpec((tm, COLS), row),
                  pl.BlockSpec((tm, COLS), row),
                  full(g_mix), full(wg), full(bg), full(wod), full(wos), full(wout),
                  full(g_ffn), full(wrh), full(wrl), full(br)],
        out_specs=[pl.BlockSpec((tm, d), row),
                   pl.BlockSpec((tm, d), row),
                   pl.BlockSpec((tm, ROUTER_LANES), row)],
        out_shape=[jax.ShapeDtypeStruct((n, d), F32),
                   jax.ShapeDtypeStruct((n, d), BF16),
                   jax.ShapeDtypeStruct((n, ROUTER_LANES), F32)],
        compiler_params=pltpu.CompilerParams(
            dimension_semantics=("arbitrary",), vmem_limit_bytes=VMEM_LIMIT),
        name="tail",
    )(x2d, ya, yb, g_mix, wg, bg, wod, wos, wout, g_ffn, wrh, wrl, br)


def _moe_kernel(x2_ref, h2_ref, comb_ref, mg_ref, mu_ref, md_ref, y_ref):
    e = pl.program_id(1)

    @pl.when(e == 0)
    def _():
        y_ref[...] = x2_ref[...]

    h2 = h2_ref[...]
    g = _dot(h2, mg_ref[...])
    act = g * _sigmoid(g) * _dot(h2, mu_ref[...])
    comb = comb_ref[...]
    lane = lax.broadcasted_iota(jnp.int32, comb.shape, 1)
    w = jnp.sum(jnp.where(lane == e + N_GROUPS, comb, 0.0), axis=1, keepdims=True)
    y_ref[...] += w * _dot(act.astype(BF16), md_ref[...])


def _moe(x2, h2, comb, mg, mu, md, tm):
    n, d = x2.shape
    de = mg.shape[2]
    tok = lambda i, e: (i, 0)
    return pl.pallas_call(
        _moe_kernel,
        grid=(n // tm, N_EXPERTS),
        in_specs=[pl.BlockSpec((tm, d), tok),
                  pl.BlockSpec((tm, d), tok),
                  pl.BlockSpec((tm, ROUTER_LANES), tok),
                  pl.BlockSpec((None, d, de), lambda i, e: (e, 0, 0)),
                  pl.BlockSpec((None, d, de), lambda i, e: (e, 0, 0)),
                  pl.BlockSpec((None, de, d), lambda i, e: (e, 0, 0))],
        out_specs=pl.BlockSpec((tm, d), tok),
        out_shape=jax.ShapeDtypeStruct((n, d), F32),
        compiler_params=pltpu.CompilerParams(
            dimension_semantics=("arbitrary", "arbitrary"), vmem_limit_bytes=VMEM_LIMIT),
        name="moe",
    )(x2, h2, comb, mg, mu, md)


def _tile(n, pref):
    return pref if n % pref == 0 else n


def kernel(x_prompt, x_sample, cache_diff_k, cache_diff_v, cache_sb_k, cache_sb_v, rel_bias_table, norm_mix_g, w_in, q_norm_g, k_norm_g, lambda_q1, lambda_k1, lambda_q2, lambda_k2, subln_g, w_o_diff, w_o_sb, w_branch_gate, b_branch_gate, w_out, norm_ffn_g, w_router_group, b_router_group, w_router_expert, b_router_expert, moe_w_gate, moe_w_up, moe_w_down):
    b, t, d = x_prompt.shape
    bs, ts, _ = x_sample.shape
    depth = norm_mix_g.shape[0]
    p = cache_diff_k.shape[2]
    tq = _tile(t, 512)
    tq_b = _tile(t, 256)
    tk_s = _tile(p, 256)
    tn = 128
    assert t % tq == 0 and t % tq_b == 0 and tq % CHUNK == 0 and p % CHUNK == 0
    assert ts <= CHUNK and ts <= tn
    assert p % tk_s == 0

    q_loc = np.arange(tq)
    bkt_prompt = np.concatenate([_bucket_tile_np(q_loc + tq, np.arange(tq) + tq),
                                 _bucket_tile_np(q_loc + tq, np.arange(tq))], axis=0)
    far = np.unique(_t5_bucket_np(-np.arange(tq + 1, 4 * tq)))
    assert far.size == 1
    far_bucket = int(far[0])
    q_s = p + np.arange(ts)
    k_s = np.concatenate([np.arange(p), p + np.arange(tn)])
    bkt_sample = _bucket_tile_np(q_s, k_s)
    bkt_sample[:, p + ts:] = -1
    bias_prompt = _bias_tiles(rel_bias_table, jnp.asarray(bkt_prompt)).reshape(H_A, 2, tq, tq)
    bias_sample = _bias_tiles(rel_bias_table, jnp.asarray(bkt_sample))
    bias_sc, bias_sn = bias_sample[:, :, :p], bias_sample[:, :, p:]

    gm = np.kron(np.eye(256 // DH), np.full((DH, DH), 1.0 / DH)).astype(np.float32)
    gm = jnp.asarray(gm, dtype=BF16)
    u_p = _upper_sum_matrix(tq_b)
    u_s = _upper_sum_matrix(tk_s)
    u_n = _upper_sum_matrix(tn)

    y_p = x_prompt.reshape(b * t, d)
    y_s = x_sample.reshape(bs * ts, d)
    tm_p = _tile(b * t, 256)
    tm_s = _tile(bs * ts, 128)
    tm_moe_p = _tile(b * t, 1024)
    outs = [[] for _ in range(8)]
    for l in range(depth):
        lam_init = 0.8 - 0.6 * math.exp(-0.3 * l)
        w_in_b = w_in[l].astype(BF16)
        g_mix = norm_mix_g[l].reshape(1, d)
        gq = jnp.tile(q_norm_g[l], COLS // DH).reshape(1, COLS)
        gk = jnp.tile(k_norm_g[l], COLS // DH).reshape(1, COLS)
        lam_p = jnp.stack([lambda_q1[l], lambda_k1[l], lambda_q2[l], lambda_k2[l]])
        sg = subln_g[l].reshape(1, SLAB)
        wg = w_branch_gate[l].astype(BF16)
        bg = b_branch_gate[l].reshape(1, 2 * d)
        wod = w_o_diff[l].astype(BF16)
        wos = w_o_sb[l].astype(BF16)
        wout = w_out[l].astype(BF16)
        g_ffn = norm_ffn_g[l].reshape(1, d)
        n_r = N_GROUPS + N_EXPERTS
        wr = jnp.pad(jnp.concatenate([w_router_group[l], w_router_expert[l]], axis=1),
                     ((0, 0), (0, ROUTER_LANES - n_r)))
        wrh = wr.astype(BF16)
        wrl = (wr - wrh.astype(F32)).astype(BF16)
        br = jnp.pad(jnp.concatenate([b_router_group[l], b_router_expert[l]]),
                     (0, ROUTER_LANES - n_r)).reshape(1, ROUTER_LANES)
        mg = moe_w_gate[l].astype(BF16)
        mu = moe_w_up[l].astype(BF16)
        md = moe_w_down[l].astype(BF16)

        def tail_and_moe(x2d, ya, yb, tm, tm_moe):
            x2, h2, comb = _tail(x2d, ya, yb, g_mix, wg, bg, wod, wos, wout, g_ffn, wrh, wrl, br, tm)
            return _moe(x2, h2, comb, mg, mu, md, tm_moe)

        ka, va, kb, vb, qa_b, ka_b, va_b, qb_b, kb_b, vb_b = _project(y_p, g_mix, w_in_b, gq, gk, gm, tm_p)
        r3 = lambda a: a.reshape(b, t, COLS)
        ya = _diff_prompt(rel_bias_table, r3(qa_b), r3(ka_b), r3(va_b), bias_prompt, lam_p, sg,
                          tq, far_bucket, lam_init)
        yb = _sb_prompt(r3(qb_b), r3(kb_b), r3(vb_b), u_p, tq_b)
        y_p = tail_and_moe(y_p, ya.reshape(b * t, COLS), yb.reshape(b * t, COLS), tm_p, tm_moe_p)
        outs[0].append(ka.reshape(b, t, H_A, 2, DH))
        outs[1].append(va.reshape(b, t, H_A, 2 * DH))
        outs[2].append(kb.reshape(b, t, H_B, DH))
        outs[3].append(vb.reshape(b, t, H_B, DH))

        ka, va, kb, vb, qa_b, ka_b, va_b, qb_b, kb_b, vb_b = _project(y_s, g_mix, w_in_b, gq, gk, gm, tm_s)
        s3 = lambda a: a.reshape(bs, ts, COLS)
        padk = lambda a: jnp.pad(s3(a), ((0, 0), (0, tn - ts), (0, 0)))
        ya = _diff_sample(s3(qa_b), cache_diff_k[l].reshape(bs, p, COLS), cache_diff_v[l].reshape(bs, p, COLS),
                          padk(ka_b), padk(va_b), bias_sc, bias_sn, lam_p, sg, lam_init)
        yb = _sb_sample(s3(qb_b), cache_sb_k[l].reshape(bs, p, COLS), cache_sb_v[l].reshape(bs, p, COLS),
                        padk(kb_b), padk(vb_b), u_s, u_n, tk_s)
        y_s = tail_and_moe(y_s, ya.reshape(bs * ts, COLS), yb.reshape(bs * ts, COLS), tm_s, tm_s)
        outs[4].append(ka.reshape(bs, ts, H_A, 2, DH))
        outs[5].append(va.reshape(bs, ts, H_A, 2 * DH))
        outs[6].append(kb.reshape(bs, ts, H_B, DH))
        outs[7].append(vb.reshape(bs, ts, H_B, DH))

    return (y_p.reshape(b, t, d), y_s.reshape(bs, ts, d)) + tuple(jnp.stack(o) for o in outs)
```

```python
import functools
import math

import numpy as np
import jax
import jax.numpy as jnp
from jax import lax
from jax.experimental import pallas as pl
from jax.experimental.pallas import tpu as pltpu

F32 = jnp.float32
BF16 = jnp.bfloat16

EPS = 1e-6
CHUNK = 64
H_A = 4
DH = 64
H_B = 8
SLAB = 2 * DH
N_SLAB_A = H_A
N_SLAB_B = H_B // 2
COLS = H_A * SLAB
N_BUCKETS = 32
MAX_DISTANCE = 128
N_GROUPS = 4
EXPERTS_PER_GROUP = 4
N_EXPERTS = N_GROUPS * EXPERTS_PER_GROUP
N_PAIRS = EXPERTS_PER_GROUP * (EXPERTS_PER_GROUP - 1) // 2
N_CLASSES = N_GROUPS * N_PAIRS
ROUTER_LANES = 128
NEG = -1e30
SB_DEAD = -104.0
VMEM_LIMIT = 48 * 1024 * 1024


def _rms(x, g):
    return x * lax.rsqrt(jnp.mean(x * x, axis=-1, keepdims=True) + EPS) * g


def _sigmoid(x):
    return 1.0 / (1.0 + jnp.exp(-x))


def _dot(a, b):
    return jnp.dot(a, b, preferred_element_type=F32)


def _dot_nt(a, b):
    return lax.dot_general(a, b, (((1,), (1,)), ((), ())), preferred_element_type=F32)


def _lane_tiles(x, n):
    return x if n == 1 else jnp.concatenate([x] * n, axis=1)


def _split_bf16(x):
    hi = x.astype(BF16)
    lo = (x - hi.astype(F32)).astype(BF16)
    return hi, lo


def _proj_kernel(x_ref, g_ref, w_ref, gq_ref, gk_ref, gm_ref,
                 ka_o, va_o, kb_o, vb_o, qa_b, ka_b, va_b, qb_b, kb_b, vb_b):
    h = _rms(x_ref[...], g_ref[...]).astype(BF16)
    proj = _dot(h, w_ref[...])
    gm = gm_ref[...]

    def sub_head_norm(t, gain):
        outs = []
        for s in range(0, COLS, 256):
            ts = t[:, s:s + 256]
            hi, lo = _split_bf16(ts * ts)
            msq = _dot(hi, gm) + _dot(lo, gm)
            outs.append(ts * lax.rsqrt(msq + EPS))
        return jnp.concatenate(outs, axis=1) * gain

    qa = sub_head_norm(proj[:, 0 * COLS:1 * COLS], gq_ref[...])
    ka = sub_head_norm(proj[:, 1 * COLS:2 * COLS], gk_ref[...])
    va = proj[:, 2 * COLS:3 * COLS]
    qb = proj[:, 3 * COLS:4 * COLS]
    kb = proj[:, 4 * COLS:5 * COLS]
    vb = proj[:, 5 * COLS:6 * COLS]
    ka_o[...] = ka
    va_o[...] = va
    kb_o[...] = kb
    vb_o[...] = vb
    scale = 1.0 / math.sqrt(DH)
    qa_b[...] = (qa * scale).astype(BF16)
    ka_b[...] = ka.astype(BF16)
    va_b[...] = va.astype(BF16)
    qb_b[...] = (qb * scale).astype(BF16)
    kb_b[...] = kb.astype(BF16)
    vb_b[...] = vb.astype(BF16)


def _project(x2d, g_mix, w_in_b, gq, gk, gm, tm):
    n, d = x2d.shape
    row = lambda i: (i, 0)
    const = lambda i: (0, 0)
    out_f = jax.ShapeDtypeStruct((n, COLS), F32)
    out_b = jax.ShapeDtypeStruct((n, COLS), BF16)
    blk = pl.BlockSpec((tm, COLS), row)
    return pl.pallas_call(
        _proj_kernel,
        grid=(n // tm,),
        in_specs=[pl.BlockSpec((tm, d), row),
                  pl.BlockSpec((1, d), const),
                  pl.BlockSpec(w_in_b.shape, const),
                  pl.BlockSpec((1, COLS), const),
                  pl.BlockSpec((1, COLS), const),
                  pl.BlockSpec(gm.shape, const)],
        out_specs=[blk] * 10,
        out_shape=[out_f] * 4 + [out_b] * 6,
        compiler_params=pltpu.CompilerParams(
            dimension_semantics=("arbitrary",), vmem_limit_bytes=VMEM_LIMIT),
        name="proj",
    )(x2d, g_mix, w_in_b, gq, gk, gm)


def _t5_bucket_np(rel):
    half = N_BUCKETS // 2
    max_exact = half // 2
    base = np.where(rel > 0, half, 0)
    n = np.abs(rel)
    nf = np.maximum(n, 1).astype(np.float64)
    large = max_exact + (np.log(nf / max_exact) / math.log(MAX_DISTANCE / max_exact)
                         * (half - max_exact)).astype(np.int32)
    large = np.minimum(large, half - 1)
    return (base + np.where(n < max_exact, n, large)).astype(np.int32)


def _bucket_tile_np(q_pos, k_pos):
    rel = k_pos[None, :] - q_pos[:, None]
    allowed = (k_pos[None, :] // CHUNK) <= (q_pos[:, None] // CHUNK)
    return np.where(allowed, _t5_bucket_np(rel), -1).astype(np.int32)


def _bias_kernel(tbl_ref, bkt_ref, o_ref):
    h = pl.program_id(0)
    bkt = bkt_ref[...]
    out = jnp.full(bkt.shape, NEG, F32)
    for b in range(N_BUCKETS):
        out = jnp.where(bkt == b, tbl_ref[b, h], out)
    o_ref[...] = out


def _bias_tiles(rel_table, buckets):
    r, c = buckets.shape
    return pl.pallas_call(
        _bias_kernel,
        grid=(H_A,),
        in_specs=[pl.BlockSpec(memory_space=pltpu.SMEM),
                  pl.BlockSpec((r, c), lambda h: (0, 0))],
        out_specs=pl.BlockSpec((None, r, c), lambda h: (h, 0, 0)),
        out_shape=jax.ShapeDtypeStruct((H_A, r, c), F32),
        name="bias_tiles",
    )(rel_table, buckets)


def _two_maps(q):
    lane = lax.broadcasted_iota(jnp.int32, q.shape, 1)
    zero = jnp.zeros_like(q)
    return jnp.concatenate([jnp.where(lane < DH, q, zero), jnp.where(lane >= DH, q, zero)], axis=0)


def _diff_lambda(lam_ref, lam_init):
    p = lam_ref[...]
    s1 = jnp.sum(p[0:1] * p[1:2], axis=1, keepdims=True)
    s2 = jnp.sum(p[2:3] * p[3:4], axis=1, keepdims=True)
    return jnp.exp(s1) - jnp.exp(s2) + lam_init


def _diff_finish(acc, l, t, lam, sg, lam_init):
    o = acc[:t] / l[:t] - lam * (acc[t:] / l[t:])
    o = o * lax.rsqrt(jnp.mean(o * o, axis=-1, keepdims=True) + EPS) * sg
    return o * (1.0 - lam_init)


def _diff_prompt_kernel(tbl_ref, q_ref, k_ref, v_ref, bias_ref, lam_ref, sg_ref, o_ref,
                        vx_ref, m_ref, l_ref, acc_ref, *, tq, tk, far_bucket, lam_init):
    h = pl.program_id(1)
    i = pl.program_id(2)

    @pl.when(i == 0)
    def _():
        vx_ref[:, :SLAB] = v_ref[...]
        vx_ref[:, SLAB:] = jnp.ones((vx_ref.shape[0], SLAB), vx_ref.dtype)

    qq = _two_maps(q_ref[...])
    m_ref[...] = jnp.full(m_ref.shape, NEG, F32)
    l_ref[...] = jnp.zeros(l_ref.shape, F32)
    acc_ref[...] = jnp.zeros(acc_ref.shape, F32)

    n_sub = tq // tk

    def update(jk, bias):
        rows = pl.ds(pl.multiple_of(jk * tk, tk), tk)
        s = _dot_nt(qq, k_ref[rows, :])
        if bias is not None:
            s = s + jnp.concatenate([bias, bias], axis=0)
        m_old = m_ref[...]
        m_new = jnp.maximum(m_old, jnp.max(s, axis=1, keepdims=True))
        alpha = jnp.exp(m_old - m_new)
        p = jnp.exp(s - _lane_tiles(m_new, tk // SLAB))
        pv = _dot(p.astype(BF16), vx_ref[rows, :])
        l_ref[...] = alpha * l_ref[...] + pv[:, SLAB:]
        acc_ref[...] = alpha * acc_ref[...] + pv[:, :SLAB]
        m_ref[...] = m_new

    def far_body(j, carry):
        for sub in range(n_sub):
            update(j * n_sub + sub, None)
        return carry

    lax.fori_loop(0, jnp.maximum(i - 1, 0), far_body, 0)
    m_ref[...] = m_ref[...] + tbl_ref[far_bucket, h]

    @pl.when(i >= 1)
    def _():
        for sub in range(n_sub):
            update((i - 1) * n_sub + sub, bias_ref[1, :, sub * tk:(sub + 1) * tk])

    for sub in range(n_sub):
        update(i * n_sub + sub, bias_ref[0, :, sub * tk:(sub + 1) * tk])
    lam = _diff_lambda(lam_ref, lam_init)
    o = _diff_finish(acc_ref[...], l_ref[...], tq, lam, sg_ref[...], lam_init)
    o_ref[...] = o.astype(o_ref.dtype)


def _diff_prompt(rel_table, qa, ka, va, bias, lam_p, sg, tq, far_bucket, lam_init):
    b, t, _ = qa.shape
    kern = functools.partial(_diff_prompt_kernel, tq=tq, tk=_tile(tq, 256), far_bucket=far_bucket,
                             lam_init=lam_init)
    return pl.pallas_call(
        kern,
        grid=(b, H_A, t // tq),
        in_specs=[pl.BlockSpec(memory_space=pltpu.SMEM),
                  pl.BlockSpec((None, tq, SLAB), lambda b, h, i: (b, i, h)),
                  pl.BlockSpec((None, t, SLAB), lambda b, h, i: (b, 0, h)),
                  pl.BlockSpec((None, t, SLAB), lambda b, h, i: (b, 0, h)),
                  pl.BlockSpec((None, 2, tq, tq), lambda b, h, i: (h, 0, 0, 0)),
                  pl.BlockSpec(lam_p.shape, lambda b, h, i: (0, 0)),
                  pl.BlockSpec((1, SLAB), lambda b, h, i: (0, 0))],
        out_specs=pl.BlockSpec((None, tq, SLAB), lambda b, h, i: (b, i, h)),
        out_shape=jax.ShapeDtypeStruct((b, t, COLS), BF16),
        scratch_shapes=[pltpu.VMEM((t, 2 * SLAB), BF16),
                        pltpu.VMEM((2 * tq, SLAB), F32),
                        pltpu.VMEM((2 * tq, SLAB), F32),
                        pltpu.VMEM((2 * tq, SLAB), F32)],
        compiler_params=pltpu.CompilerParams(
            dimension_semantics=("arbitrary", "arbitrary", "arbitrary"),
            vmem_limit_bytes=VMEM_LIMIT),
        name="diff_prompt",
    )(rel_table, qa, ka, va, bias, lam_p, sg)


def _diff_sample_kernel(q_ref, kc_ref, vc_ref, kn_ref, vn_ref, bc_ref, bn_ref, lam_ref, sg_ref,
                        o_ref, *, lam_init):
    qq = _two_maps(q_ref[...])
    t = q_ref.shape[0]
    bc = bc_ref[...]
    bn = bn_ref[...]
    s_c = _dot_nt(qq, kc_ref[...].astype(BF16)) + jnp.concatenate([bc, bc], axis=0)
    s_n = _dot_nt(qq, kn_ref[...]) + jnp.concatenate([bn, bn], axis=0)
    m = jnp.maximum(jnp.max(s_c, axis=1, keepdims=True), jnp.max(s_n, axis=1, keepdims=True))
    p_c = jnp.exp(s_c - m)
    p_n = jnp.exp(s_n - m)
    l = jnp.sum(p_c, axis=1, keepdims=True) + jnp.sum(p_n, axis=1, keepdims=True)
    acc = _dot(p_c.astype(BF16), vc_ref[...].astype(BF16)) + _dot(p_n.astype(BF16), vn_ref[...])
    lam = _diff_lambda(lam_ref, lam_init)
    o = _diff_finish(acc, l, t, lam, sg_ref[...], lam_init)
    o_ref[...] = o.astype(o_ref.dtype)


def _diff_sample(qa, kc, vc, kn, vn, bias_c, bias_n, lam_p, sg, lam_init):
    b, t, _ = qa.shape
    p = kc.shape[1]
    tn = kn.shape[1]
    slab = lambda rows: pl.BlockSpec((None, rows, SLAB), lambda b, h: (b, 0, h))
    return pl.pallas_call(
        functools.partial(_diff_sample_kernel, lam_init=lam_init),
        grid=(b, H_A),
        in_specs=[slab(t), slab(p), slab(p), slab(tn), slab(tn),
                  pl.BlockSpec((None, t, p), lambda b, h: (h, 0, 0)),
                  pl.BlockSpec((None, t, tn), lambda b, h: (h, 0, 0)),
                  pl.BlockSpec(lam_p.shape, lambda b, h: (0, 0)),
                  pl.BlockSpec((1, SLAB), lambda b, h: (0, 0))],
        out_specs=slab(t),
        out_shape=jax.ShapeDtypeStruct((b, t, COLS), BF16),
        compiler_params=pltpu.CompilerParams(
            dimension_semantics=("arbitrary", "arbitrary"), vmem_limit_bytes=VMEM_LIMIT),
        name="diff_sample",
    )(qa, kc, vc, kn, vn, bias_c, bias_n, lam_p, sg)


def _sb_tile(qq, k, v, u, c, acc, allowed):
    z = _dot_nt(qq, k)
    sp = jnp.log1p(jnp.exp(-jnp.abs(z)))
    log_beta = jnp.minimum(z, 0.0) - sp
    log_keep = -jnp.maximum(z, 0.0) - sp
    if allowed is not None:
        log_keep = jnp.where(allowed, log_keep, 0.0)
    hi, lo = _split_bf16(log_keep)
    later = _dot(hi, u) + _dot(lo, u)
    a = jnp.exp(log_beta + later + _lane_tiles(c, z.shape[1] // SLAB))
    if allowed is not None:
        a = jnp.where(allowed, a, 0.0)
    acc = acc + _dot(a.astype(BF16), v)
    c = c + jnp.sum(log_keep, axis=1, keepdims=True)
    return c, acc


def _strictly_causal_two_heads(t, n_keys):
    row = lax.broadcasted_iota(jnp.int32, (2 * t, n_keys), 0)
    col = lax.broadcasted_iota(jnp.int32, (2 * t, n_keys), 1)
    return col < jnp.where(row >= t, row - t, row)


def _merge_heads(acc, t):
    lane = lax.broadcasted_iota(jnp.int32, (t, SLAB), 1)
    return jnp.where(lane < DH, acc[:t], acc[t:])


def _sb_prompt_kernel(q_ref, k_ref, v_ref, u_ref, o_ref, *, tq):
    i = pl.program_id(2)
    qq = _two_maps(q_ref[...])
    u = u_ref[...]

    def tile(j, c, acc, mask):
        rows = pl.ds(pl.multiple_of(j * tq, tq), tq)
        return _sb_tile(qq, k_ref[rows, :], v_ref[rows, :], u, c, acc, mask)

    zeros = jnp.zeros((2 * tq, SLAB), F32)
    c, acc = tile(i, zeros, zeros, _strictly_causal_two_heads(tq, tq))

    def alive(c):
        return (jnp.max(c) > SB_DEAD).astype(jnp.int32)

    def cond(st):
        j, live, _, _ = st
        return (j >= 0) & (live > 0)

    def body(st):
        j, _, c, acc = st
        c, acc = tile(j, c, acc, None)
        return j - 1, alive(c), c, acc

    _, _, c, acc = lax.while_loop(cond, body, (i - 1, alive(c), c, acc))
    o_ref[...] = _merge_heads(acc, tq).astype(o_ref.dtype)


def _sb_prompt(qb, kb, vb, u, tq):
    b, t, _ = qb.shape
    return pl.pallas_call(
        functools.partial(_sb_prompt_kernel, tq=tq),
        grid=(b, N_SLAB_B, t // tq),
        in_specs=[pl.BlockSpec((None, tq, SLAB), lambda b, h, i: (b, i, h)),
                  pl.BlockSpec((None, t, SLAB), lambda b, h, i: (b, 0, h)),
                  pl.BlockSpec((None, t, SLAB), lambda b, h, i: (b, 0, h)),
                  pl.BlockSpec(u.shape, lambda b, h, i: (0, 0))],
        out_specs=pl.BlockSpec((None, tq, SLAB), lambda b, h, i: (b, i, h)),
        out_shape=jax.ShapeDtypeStruct((b, t, COLS), BF16),
        compiler_params=pltpu.CompilerParams(
            dimension_semantics=("arbitrary", "arbitrary", "arbitrary"),
            vmem_limit_bytes=VMEM_LIMIT),
        name="sb_prompt",
    )(qb, kb, vb, u)


def _sb_sample_kernel(q_ref, kc_ref, vc_ref, kn_ref, vn_ref, u_ref, un_ref, o_ref, *, tk):
    t = q_ref.shape[0]
    tn = kn_ref.shape[0]
    n_tiles = kc_ref.shape[0] // tk
    qq = _two_maps(q_ref[...])
    u = u_ref[...]
    zeros = jnp.zeros((2 * t, SLAB), F32)
    c, acc = _sb_tile(qq, kn_ref[...], vn_ref[...], un_ref[...], zeros, zeros,
                      _strictly_causal_two_heads(t, tn))

    def body(n, st):
        c, acc = st
        rows = pl.ds(pl.multiple_of((n_tiles - 1 - n) * tk, tk), tk)
        return _sb_tile(qq, kc_ref[rows, :].astype(BF16), vc_ref[rows, :].astype(BF16),
                        u, c, acc, None)

    c, acc = lax.fori_loop(0, n_tiles, body, (c, acc))
    o_ref[...] = _merge_heads(acc, t).astype(o_ref.dtype)


def _sb_sample(qb, kc, vc, kn, vn, u, un, tk):
    b, t, _ = qb.shape
    p = kc.shape[1]
    tn = kn.shape[1]
    slab = lambda rows: pl.BlockSpec((None, rows, SLAB), lambda b, h: (b, 0, h))
    return pl.pallas_call(
        functools.partial(_sb_sample_kernel, tk=tk),
        grid=(b, N_SLAB_B),
        in_specs=[slab(t), slab(p), slab(p), slab(tn), slab(tn),
                  pl.BlockSpec(u.shape, lambda b, h: (0, 0)),
                  pl.BlockSpec(un.shape, lambda b, h: (0, 0))],
        out_specs=slab(t),
        out_shape=jax.ShapeDtypeStruct((b, t, COLS), BF16),
        compiler_params=pltpu.CompilerParams(
            dimension_semantics=("arbitrary", "arbitrary"), vmem_limit_bytes=VMEM_LIMIT),
        name="sb_sample",
    )(qb, kc, vc, kn, vn, u, un)


def _upper_sum_matrix(n):
    j = np.arange(n)[:, None]
    s = np.arange(n)[None, :]
    return jnp.asarray((j > s).astype(np.float32), dtype=BF16)


def _route_class(logits):
    lane = lax.broadcasted_iota(jnp.int32, logits.shape, 1).astype(F32)
    big = float(ROUTER_LANES)

    def first_argmax(vals):
        top = jnp.max(vals, axis=1, keepdims=True)
        return jnp.min(jnp.where(vals == top, lane, big), axis=1, keepdims=True)

    g_sel = first_argmax(jnp.where(lane < N_GROUPS, logits, NEG))
    first = N_GROUPS + g_sel * EXPERTS_PER_GROUP
    in_group = (lane >= first) & (lane < first + EXPERTS_PER_GROUP)
    el = jnp.where(in_group, logits, NEG)
    i1 = first_argmax(el)
    i2 = first_argmax(jnp.where(lane == i1, NEG, el))
    a = jnp.minimum(i1, i2) - first
    b = jnp.maximum(i1, i2) - first
    pair = a * (2 * EXPERTS_PER_GROUP - 1 - a) * 0.5 + (b - a - 1.0)
    return g_sel * N_PAIRS + pair


def _tail_kernel(x_ref, ya_ref, yb_ref, gmix_ref, wg_ref, bg_ref, wod_ref, wos_ref, wout_ref,
                 gffn_ref, wr_ref, br_ref, low_ref, x2_ref, route_ref, counts_ref, run_ref):
    @pl.when(pl.program_id(0) == 0)
    def _():
        run_ref[...] = jnp.zeros(run_ref.shape, F32)

    x = x_ref[...]
    d = x.shape[1]
    h = _rms(x, gmix_ref[...]).astype(BF16)
    gates = _sigmoid(_dot(h, wg_ref[...]) + bg_ref[...])
    mix = gates[:, :d] * _dot(ya_ref[...], wod_ref[...]) + gates[:, d:] * _dot(yb_ref[...], wos_ref[...])
    x2 = x + _dot(mix.astype(BF16), wout_ref[...])
    x2_ref[...] = x2
    h2 = _rms(x2, gffn_ref[...]).astype(BF16)
    logits = _dot(h2, wr_ref[...]) + br_ref[...]
    cls = _route_class(logits)
    lane = lax.broadcasted_iota(jnp.int32, logits.shape, 1).astype(F32)
    onehot = jnp.where(lane == cls, 1.0, 0.0)
    earlier = _dot(low_ref[...], onehot.astype(BF16)) + run_ref[...]
    rank = jnp.sum(earlier * onehot, axis=1, keepdims=True)
    route_ref[...] = jnp.where(lane == 0.0, cls, jnp.where(lane == 1.0, rank, 0.0))
    run = run_ref[...] + jnp.sum(onehot, axis=0, keepdims=True)
    run_ref[...] = run
    counts_ref[...] = run


def _tail(x2d, ya, yb, g_mix, wg, bg, wod, wos, wout, g_ffn, wr, br, tm):
    n, d = x2d.shape
    row = lambda i: (i, 0)
    const = lambda i: (0, 0)
    full = lambda a: pl.BlockSpec(a.shape, const)
    r = np.arange(tm)
    low = jnp.asarray((r[None, :] < r[:, None]).astype(np.float32), dtype=BF16)
    return pl.pallas_call(
        _tail_kernel,
        grid=(n // tm,),
        in_specs=[pl.BlockSpec((tm, d), row),
                  pl.BlockSpec((tm, COLS), row),
                  pl.BlockSpec((tm, COLS), row),
                  full(g_mix), full(wg), full(bg), full(wod), full(wos), full(wout),
                  full(g_ffn), full(wr), full(br), full(low)],
        out_specs=[pl.BlockSpec((tm, d), row),
                   pl.BlockSpec((tm, ROUTER_LANES), row),
                   pl.BlockSpec((1, ROUTER_LANES), const)],
        out_shape=[jax.ShapeDtypeStruct((n, d), F32),
                   jax.ShapeDtypeStruct((n, ROUTER_LANES), F32),
                   jax.ShapeDtypeStruct((1, ROUTER_LANES), F32)],
        scratch_shapes=[pltpu.VMEM((1, ROUTER_LANES), F32)],
        compiler_params=pltpu.CompilerParams(
            dimension_semantics=("arbitrary",), vmem_limit_bytes=VMEM_LIMIT),
        name="tail",
    )(x2d, ya, yb, g_mix, wg, bg, wod, wos, wout, g_ffn, wr, br, low)


def _class_experts_np():
    ea, eb = [], []
    for g in range(N_GROUPS):
        for a in range(EXPERTS_PER_GROUP):
            for b in range(a + 1, EXPERTS_PER_GROUP):
                ea.append(g * EXPERTS_PER_GROUP + a)
                eb.append(g * EXPERTS_PER_GROUP + b)
    return np.asarray(ea, np.int32), np.asarray(eb, np.int32)


def _dispatch_plan(route, counts, tm):
    n = route.shape[0]
    cls = route[:, 0].astype(jnp.int32)
    rank = route[:, 1].astype(jnp.int32)
    cnt = counts[0, :N_CLASSES].astype(jnp.int32)
    tiles = (cnt + tm - 1) // tm
    tile_end = jnp.cumsum(tiles)
    pos = (tile_end - tiles)[cls] * tm + rank
    n_tiles = n // tm + N_CLASSES
    t_idx = jnp.arange(n_tiles, dtype=jnp.int32)
    used = t_idx < tile_end[-1]
    tile_cls = jnp.sum((t_idx[:, None] >= tile_end[None, :]).astype(jnp.int32), axis=1)
    tile_cls = jnp.where(used, tile_cls, jnp.max(jnp.where(used, tile_cls, 0)))
    ea_np, eb_np = _class_experts_np()
    return pos, jnp.asarray(ea_np)[tile_cls], jnp.asarray(eb_np)[tile_cls], used.astype(jnp.int32)


def _row_copies(n_rows, start_one, wait_one):
    def issue(r, carry):
        start_one(r)
        return carry

    def drain(r, carry):
        wait_one()
        return carry

    lax.fori_loop(0, n_rows, issue, 0, unroll=8)
    lax.fori_loop(0, n_rows, drain, 0, unroll=8)


def _row(ref, r):
    return ref.at[pl.ds(r, 1), :]


def _dispatch_kernel(pos_ref, x_ref, init_hbm, xs_hbm, sem):
    del init_hbm
    _row_copies(
        x_ref.shape[0],
        lambda r: pltpu.make_async_copy(_row(x_ref, r), _row(xs_hbm, pos_ref[0, r]), sem).start(),
        lambda: pltpu.make_async_copy(_row(x_ref, 0), _row(xs_hbm, 0), sem).wait())


def _dispatch(pos, x2, n_slots, tm):
    n, d = x2.shape
    return pl.pallas_call(
        _dispatch_kernel,
        grid=(n // tm,),
        in_specs=[pl.BlockSpec((None, 1, tm), lambda i: (i, 0, 0), memory_space=pltpu.SMEM),
                  pl.BlockSpec((tm, d), lambda i: (i, 0)),
                  pl.BlockSpec(memory_space=pl.ANY)],
        out_specs=pl.BlockSpec(memory_space=pl.ANY),
        out_shape=jax.ShapeDtypeStruct((n_slots, d), F32),
        scratch_shapes=[pltpu.SemaphoreType.DMA],
        input_output_aliases={2: 0},
        compiler_params=pltpu.CompilerParams(dimension_semantics=("arbitrary",)),
        name="moe_dispatch",
    )(pos.reshape(n // tm, 1, tm), x2, jnp.zeros((n_slots, d), F32))


def _combine_kernel(pos_ref, ys_hbm, y_ref, sem):
    _row_copies(
        y_ref.shape[0],
        lambda r: pltpu.make_async_copy(_row(ys_hbm, pos_ref[0, r]), _row(y_ref, r), sem).start(),
        lambda: pltpu.make_async_copy(_row(ys_hbm, 0), _row(y_ref, 0), sem).wait())


def _combine(pos, ys, n, tm):
    d = ys.shape[1]
    return pl.pallas_call(
        _combine_kernel,
        grid=(n // tm,),
        in_specs=[pl.BlockSpec((None, 1, tm), lambda i: (i, 0, 0), memory_space=pltpu.SMEM),
                  pl.BlockSpec(memory_space=pl.ANY)],
        out_specs=pl.BlockSpec((tm, d), lambda i: (i, 0)),
        out_shape=jax.ShapeDtypeStruct((n, d), F32),
        scratch_shapes=[pltpu.SemaphoreType.DMA],
        compiler_params=pltpu.CompilerParams(dimension_semantics=("arbitrary",)),
        name="moe_combine",
    )(pos.reshape(n // tm, 1, tm), ys)


def _moe_kernel(ea_ref, eb_ref, used_ref, xs_ref, g_ref, wr_ref, br_ref,
                mga_ref, mua_ref, mda_ref, mgb_ref, mub_ref, mdb_ref, ys_ref):
    t = pl.program_id(0)

    @pl.when(used_ref[t] == 0)
    def _():
        ys_ref[...] = jnp.zeros(ys_ref.shape, F32)

    @pl.when(used_ref[t] != 0)
    def _():
        x = xs_ref[...]
        h2 = _rms(x, g_ref[...]).astype(BF16)
        logits = _dot(h2, wr_ref[...]) + br_ref[...]
        lane = lax.broadcasted_iota(jnp.int32, logits.shape, 1)
        pick = lambda e: jnp.sum(jnp.where(lane == N_GROUPS + e, logits, 0.0), axis=1, keepdims=True)
        la = pick(ea_ref[t])
        lb = pick(eb_ref[t])
        top = jnp.maximum(la, lb)
        pa = jnp.exp(la - top)
        pb = jnp.exp(lb - top)

        def expert(mg_ref, mu_ref, md_ref):
            g = _dot(h2, mg_ref[...])
            act = g * _sigmoid(g) * _dot(h2, mu_ref[...])
            return _dot(act.astype(BF16), md_ref[...])

        ya = expert(mga_ref, mua_ref, mda_ref)
        yb = expert(mgb_ref, mub_ref, mdb_ref)
        ys_ref[...] = x + ((pa / (pa + pb)) * ya + (pb / (pa + pb)) * yb)


def _moe(ea, eb, used, xs, g_ffn, wr, br, mg, mu, md, tm):
    n_slots, d = xs.shape
    de = mg.shape[2]
    const = lambda t, ea, eb, used: (0, 0)
    w_a = lambda shape: pl.BlockSpec((None,) + shape, lambda t, ea, eb, used: (ea[t], 0, 0))
    w_b = lambda shape: pl.BlockSpec((None,) + shape, lambda t, ea, eb, used: (eb[t], 0, 0))
    grid_spec = pltpu.PrefetchScalarGridSpec(
        num_scalar_prefetch=3,
        grid=(n_slots // tm,),
        in_specs=[pl.BlockSpec((tm, d), lambda t, ea, eb, used: (t, 0)),
                  pl.BlockSpec(g_ffn.shape, const),
                  pl.BlockSpec(wr.shape, const),
                  pl.BlockSpec(br.shape, const),
                  w_a((d, de)), w_a((d, de)), w_a((de, d)),
                  w_b((d, de)), w_b((d, de)), w_b((de, d))],
        out_specs=pl.BlockSpec((tm, d), lambda t, ea, eb, used: (t, 0)))
    return pl.pallas_call(
        _moe_kernel,
        grid_spec=grid_spec,
        out_shape=jax.ShapeDtypeStruct((n_slots, d), F32),
        compiler_params=pltpu.CompilerParams(
            dimension_semantics=("arbitrary",), vmem_limit_bytes=VMEM_LIMIT),
        name="moe",
    )(ea, eb, used, xs, g_ffn, wr, br, mg, mu, md, mg, mu, md)


def _tile(n, pref):
    return pref if n % pref == 0 else n


def kernel(x_prompt, x_sample, cache_diff_k, cache_diff_v, cache_sb_k, cache_sb_v, rel_bias_table, norm_mix_g, w_in, q_norm_g, k_norm_g, lambda_q1, lambda_k1, lambda_q2, lambda_k2, subln_g, w_o_diff, w_o_sb, w_branch_gate, b_branch_gate, w_out, norm_ffn_g, w_router_group, b_router_group, w_router_expert, b_router_expert, moe_w_gate, moe_w_up, moe_w_down):
    b, t, d = x_prompt.shape
    bs, ts, _ = x_sample.shape
    depth = norm_mix_g.shape[0]
    p = cache_diff_k.shape[2]
    tq = _tile(t, 512)
    tq_b = _tile(t, 256)
    tk_s = _tile(p, 256)
    tn = 128
    assert t % tq == 0 and t % tq_b == 0 and tq % CHUNK == 0 and p % CHUNK == 0
    assert ts <= CHUNK and ts <= tn
    assert p % tk_s == 0

    q_loc = np.arange(tq)
    bkt_prompt = np.concatenate([_bucket_tile_np(q_loc + tq, np.arange(tq) + tq),
                                 _bucket_tile_np(q_loc + tq, np.arange(tq))], axis=0)
    far = np.unique(_t5_bucket_np(-np.arange(tq + 1, 4 * tq)))
    assert far.size == 1
    far_bucket = int(far[0])
    q_s = p + np.arange(ts)
    k_s = np.concatenate([np.arange(p), p + np.arange(tn)])
    bkt_sample = _bucket_tile_np(q_s, k_s)
    bkt_sample[:, p + ts:] = -1
    bias_prompt = _bias_tiles(rel_bias_table, jnp.asarray(bkt_prompt)).reshape(H_A, 2, tq, tq)
    bias_sample = _bias_tiles(rel_bias_table, jnp.asarray(bkt_sample))
    bias_sc, bias_sn = bias_sample[:, :, :p], bias_sample[:, :, p:]

    gm = np.kron(np.eye(256 // DH), np.full((DH, DH), 1.0 / DH)).astype(np.float32)
    gm = jnp.asarray(gm, dtype=BF16)
    u_p = _upper_sum_matrix(tq_b)
    u_s = _upper_sum_matrix(tk_s)
    u_n = _upper_sum_matrix(tn)

    y_p = x_prompt.reshape(b * t, d)
    y_s = x_sample.reshape(bs * ts, d)
    tm_p = _tile(b * t, 256)
    tm_s = _tile(bs * ts, 128)
    tm_moe_p = _tile(b * t, 512)
    outs = [[] for _ in range(8)]
    for l in range(depth):
        lam_init = 0.8 - 0.6 * math.exp(-0.3 * l)
        w_in_b = w_in[l].astype(BF16)
        g_mix = norm_mix_g[l].reshape(1, d)
        gq = jnp.tile(q_norm_g[l], COLS // DH).reshape(1, COLS)
        gk = jnp.tile(k_norm_g[l], COLS // DH).reshape(1, COLS)
        lam_p = jnp.stack([lambda_q1[l], lambda_k1[l], lambda_q2[l], lambda_k2[l]])
        sg = subln_g[l].reshape(1, SLAB)
        wg = w_branch_gate[l].astype(BF16)
        bg = b_branch_gate[l].reshape(1, 2 * d)
        wod = w_o_diff[l].astype(BF16)
        wos = w_o_sb[l].astype(BF16)
        wout = w_out[l].astype(BF16)
        g_ffn = norm_ffn_g[l].reshape(1, d)
        n_r = N_GROUPS + N_EXPERTS
        wr = jnp.pad(jnp.concatenate([w_router_group[l], w_router_expert[l]], axis=1),
                     ((0, 0), (0, ROUTER_LANES - n_r)))
        wr = wr.astype(BF16)
        br = jnp.pad(jnp.concatenate([b_router_group[l], b_router_expert[l]]),
                     (0, ROUTER_LANES - n_r)).reshape(1, ROUTER_LANES)
        mg = moe_w_gate[l].astype(BF16)
        mu = moe_w_up[l].astype(BF16)
        md = moe_w_down[l].astype(BF16)

        def tail_and_moe(x2d, ya, yb, tm, tm_moe):
            n = x2d.shape[0]
            x2, route, counts = _tail(x2d, ya, yb, g_mix, wg, bg, wod, wos, wout, g_ffn, wr, br, tm)
            pos, ea, eb, used = _dispatch_plan(route, counts, tm_moe)
            xs = _dispatch(pos, x2, (n // tm_moe + N_CLASSES) * tm_moe, tm_moe)
            ys = _moe(ea, eb, used, xs, g_ffn, wr, br, mg, mu, md, tm_moe)
            return _combine(pos, ys, n, tm_moe)

        ka, va, kb, vb, qa_b, ka_b, va_b, qb_b, kb_b, vb_b = _project(y_p, g_mix, w_in_b, gq, gk, gm, tm_p)
        r3 = lambda a: a.reshape(b, t, COLS)
        ya = _diff_prompt(rel_bias_table, r3(qa_b), r3(ka_b), r3(va_b), bias_prompt, lam_p, sg,
                          tq, far_bucket, lam_init)
        yb = _sb_prompt(r3(qb_b), r3(kb_b), r3(vb_b), u_p, tq_b)
        y_p = tail_and_moe(y_p, ya.reshape(b * t, COLS), yb.reshape(b * t, COLS), tm_p, tm_moe_p)
        outs[0].append(ka.reshape(b, t, H_A, 2, DH))
        outs[1].append(va.reshape(b, t, H_A, 2 * DH))
        outs[2].append(kb.reshape(b, t, H_B, DH))
        outs[3].append(vb.reshape(b, t, H_B, DH))

        ka, va, kb, vb, qa_b, ka_b, va_b, qb_b, kb_b, vb_b = _project(y_s, g_mix, w_in_b, gq, gk, gm, tm_s)
        s3 = lambda a: a.reshape(bs, ts, COLS)
        padk = lambda a: jnp.pad(s3(a), ((0, 0), (0, tn - ts), (0, 0)))
        ya = _diff_sample(s3(qa_b), cache_diff_k[l].reshape(bs, p, COLS), cache_diff_v[l].reshape(bs, p, COLS),
                          padk(ka_b), padk(va_b), bias_sc, bias_sn, lam_p, sg, lam_init)
        yb = _sb_sample(s3(qb_b), cache_sb_k[l].reshape(bs, p, COLS), cache_sb_v[l].reshape(bs, p, COLS),
                        padk(kb_b), padk(vb_b), u_s, u_n, tk_s)
        y_s = tail_and_moe(y_s, ya.reshape(bs * ts, COLS), yb.reshape(bs * ts, COLS), tm_s, tm_s)
        outs[4].append(ka.reshape(bs, ts, H_A, 2, DH))
        outs[5].append(va.reshape(bs, ts, H_A, 2 * DH))
        outs[6].append(kb.reshape(bs, ts, H_B, DH))
        outs[7].append(vb.reshape(bs, ts, H_B, DH))

    return (y_p.reshape(b, t, d), y_s.reshape(bs, ts, d)) + tuple(jnp.stack(o) for o in outs)
```

```python
import functools
import math

import numpy as np
import jax
import jax.numpy as jnp
from jax import lax
from jax.experimental import pallas as pl
from jax.experimental.pallas import tpu as pltpu

F32 = jnp.float32
BF16 = jnp.bfloat16

EPS = 1e-6
CHUNK = 64
H_A = 4
DH = 64
H_B = 8
SLAB = 2 * DH
N_SLAB_A = H_A
N_SLAB_B = H_B // 2
COLS = H_A * SLAB
N_BUCKETS = 32
MAX_DISTANCE = 128
N_GROUPS = 4
EXPERTS_PER_GROUP = 4
N_EXPERTS = N_GROUPS * EXPERTS_PER_GROUP
N_PAIRS = EXPERTS_PER_GROUP * (EXPERTS_PER_GROUP - 1) // 2
N_CLASSES = N_GROUPS * N_PAIRS
ROUTER_LANES = 128
NEG = -1e30
SB_DEAD = -104.0
VMEM_LIMIT = 56 * 1024 * 1024


def _rms(x, g):
    return x * lax.rsqrt(jnp.mean(x * x, axis=-1, keepdims=True) + EPS) * g


def _sigmoid(x):
    return 1.0 / (1.0 + jnp.exp(-x))


def _dot(a, b):
    return jnp.dot(a, b, preferred_element_type=F32)


def _dot_nt(a, b):
    return lax.dot_general(a, b, (((1,), (1,)), ((), ())), preferred_element_type=F32)


def _lane_tiles(x, n):
    return x if n == 1 else jnp.concatenate([x] * n, axis=1)


def _split_bf16(x):
    hi = x.astype(BF16)
    lo = (x - hi.astype(F32)).astype(BF16)
    return hi, lo


def _cols(j, width):
    return pl.ds(pl.multiple_of(j * width, width), width)


def _sub_head_norm(t, gain, gm):
    outs = []
    for s in range(0, COLS, 256):
        ts = t[:, s:s + 256]
        hi, lo = _split_bf16(ts * ts)
        outs.append(ts * lax.rsqrt(_dot(hi, gm) + _dot(lo, gm) + EPS))
    return jnp.concatenate(outs, axis=1) * gain


def _sub_head_norm_t(t, gain, gm):
    outs = []
    for s in range(0, COLS, 256):
        ts = t[s:s + 256, :]
        hi, lo = _split_bf16(ts * ts)
        outs.append(ts * lax.rsqrt(_dot(gm, hi) + _dot(gm, lo) + EPS))
    return jnp.concatenate(outs, axis=0) * gain


def _proj_kernel(x_ref, g_ref, w_ref, gq_ref, gk_ref, gm_ref,
                 ka_o, va_o, kb_o, vb_o, qa_b, ka_b, va_b, qb_b, kb_b, vb_b):
    h = _rms(x_ref[...], g_ref[...]).astype(BF16)
    proj = _dot(h, w_ref[...])
    gm = gm_ref[...]
    qa = _sub_head_norm(proj[:, 0 * COLS:1 * COLS], gq_ref[...], gm)
    ka = _sub_head_norm(proj[:, 1 * COLS:2 * COLS], gk_ref[...], gm)
    va = proj[:, 2 * COLS:3 * COLS]
    qb = proj[:, 3 * COLS:4 * COLS]
    kb = proj[:, 4 * COLS:5 * COLS]
    vb = proj[:, 5 * COLS:6 * COLS]
    ka_o[...] = ka
    va_o[...] = va
    kb_o[...] = kb
    vb_o[...] = vb
    scale = 1.0 / math.sqrt(DH)
    qa_b[...] = (qa * scale).astype(BF16)
    ka_b[...] = ka.astype(BF16)
    va_b[...] = va.astype(BF16)
    qb_b[...] = (qb * scale).astype(BF16)
    kb_b[...] = kb.astype(BF16)
    vb_b[...] = vb.astype(BF16)


def _project(x2d, g_mix, w_in_b, gq, gk, gm, tm):
    n, d = x2d.shape
    row = lambda i: (i, 0)
    const = lambda i: (0, 0)
    out_f = jax.ShapeDtypeStruct((n, COLS), F32)
    out_b = jax.ShapeDtypeStruct((n, COLS), BF16)
    blk = pl.BlockSpec((tm, COLS), row)
    return pl.pallas_call(
        _proj_kernel,
        grid=(n // tm,),
        in_specs=[pl.BlockSpec((tm, d), row),
                  pl.BlockSpec((1, d), const),
                  pl.BlockSpec(w_in_b.shape, const),
                  pl.BlockSpec((1, COLS), const),
                  pl.BlockSpec((1, COLS), const),
                  pl.BlockSpec(gm.shape, const)],
        out_specs=[blk] * 10,
        out_shape=[out_f] * 4 + [out_b] * 6,
        compiler_params=pltpu.CompilerParams(
            dimension_semantics=("arbitrary",), vmem_limit_bytes=VMEM_LIMIT),
        name="proj",
    )(x2d, g_mix, w_in_b, gq, gk, gm)


def _proj_t_kernel(x_ref, g_ref, wn_ref, wt_ref, gq_ref, gkt_ref, gm_ref,
                   kat_o, va_o, kbt_o, vbt_o, qa_b, kat_b, va_b, qb_b, kbt_b, vbt_b):
    h = _rms(x_ref[...], g_ref[...]).astype(BF16)
    tm = h.shape[0]
    nat = _dot(h, wn_ref[...])
    tr = _dot_nt(wt_ref[...], h)
    gm = gm_ref[...]
    qa = _sub_head_norm(nat[:, 0 * COLS:1 * COLS], gq_ref[...], gm)
    va = nat[:, 1 * COLS:2 * COLS]
    qb = nat[:, 2 * COLS:3 * COLS]
    kat = _sub_head_norm_t(tr[0 * COLS:1 * COLS], _lane_tiles(gkt_ref[...], tm // SLAB), gm)
    kbt = tr[1 * COLS:2 * COLS]
    vbt = tr[2 * COLS:3 * COLS]
    kat_o[...] = kat
    va_o[...] = va
    kbt_o[...] = kbt
    vbt_o[...] = vbt
    scale = 1.0 / math.sqrt(DH)
    qa_b[...] = (qa * scale).astype(BF16)
    kat_b[...] = kat.astype(BF16)
    va_b[...] = va.astype(BF16)
    qb_b[...] = (qb * scale).astype(BF16)
    kbt_b[...] = kbt.astype(BF16)
    vbt_b[...] = vbt.astype(BF16)


def _project_t(x, g_mix, wn, wt, gq, gkt, gm, tm):
    b, t, d = x.shape
    const = lambda b, i: (0, 0)
    nat = pl.BlockSpec((None, tm, COLS), lambda b, i: (b, i, 0))
    fm = pl.BlockSpec((None, COLS, tm), lambda b, i: (b, 0, i))
    nat_s = lambda dt: jax.ShapeDtypeStruct((b, t, COLS), dt)
    fm_s = lambda dt: jax.ShapeDtypeStruct((b, COLS, t), dt)
    return pl.pallas_call(
        _proj_t_kernel,
        grid=(b, t // tm),
        in_specs=[pl.BlockSpec((None, tm, d), lambda b, i: (b, i, 0)),
                  pl.BlockSpec((1, d), const),
                  pl.BlockSpec(wn.shape, const),
                  pl.BlockSpec(wt.shape, const),
                  pl.BlockSpec((1, COLS), const),
                  pl.BlockSpec(gkt.shape, const),
                  pl.BlockSpec(gm.shape, const)],
        out_specs=[fm, nat, fm, fm, nat, fm, nat, nat, fm, fm],
        out_shape=[fm_s(F32), nat_s(F32), fm_s(F32), fm_s(F32),
                   nat_s(BF16), fm_s(BF16), nat_s(BF16), nat_s(BF16), fm_s(BF16), fm_s(BF16)],
        compiler_params=pltpu.CompilerParams(
            dimension_semantics=("arbitrary", "arbitrary"), vmem_limit_bytes=VMEM_LIMIT),
        name="proj_t",
    )(x, g_mix, wn, wt, gq, gkt, gm)


def _t5_bucket_np(rel):
    half = N_BUCKETS // 2
    max_exact = half // 2
    base = np.where(rel > 0, half, 0)
    n = np.abs(rel)
    nf = np.maximum(n, 1).astype(np.float64)
    large = max_exact + (np.log(nf / max_exact) / math.log(MAX_DISTANCE / max_exact)
                         * (half - max_exact)).astype(np.int32)
    large = np.minimum(large, half - 1)
    return (base + np.where(n < max_exact, n, large)).astype(np.int32)


def _bucket_tile_np(q_pos, k_pos):
    rel = k_pos[None, :] - q_pos[:, None]
    allowed = (k_pos[None, :] // CHUNK) <= (q_pos[:, None] // CHUNK)
    return np.where(allowed, _t5_bucket_np(rel), -1).astype(np.int32)


def _bias_kernel(tbl_ref, bkt_ref, o_ref):
    h = pl.program_id(0)
    bkt = bkt_ref[...]
    out = jnp.full(bkt.shape, NEG, F32)
    for b in range(N_BUCKETS):
        out = jnp.where(bkt == b, tbl_ref[b, h], out)
    o_ref[...] = out


def _bias_tiles(rel_table, buckets):
    r, c = buckets.shape
    return pl.pallas_call(
        _bias_kernel,
        grid=(H_A,),
        in_specs=[pl.BlockSpec(memory_space=pltpu.SMEM),
                  pl.BlockSpec((r, c), lambda h: (0, 0))],
        out_specs=pl.BlockSpec((None, r, c), lambda h: (h, 0, 0)),
        out_shape=jax.ShapeDtypeStruct((H_A, r, c), F32),
        name="bias_tiles",
    )(rel_table, buckets)


def _two_maps(q):
    lane = lax.broadcasted_iota(jnp.int32, q.shape, 1)
    zero = jnp.zeros_like(q)
    return jnp.concatenate([jnp.where(lane < DH, q, zero), jnp.where(lane >= DH, q, zero)], axis=0)


def _diff_lambda(lam_ref, lam_init):
    p = lam_ref[...]
    s1 = jnp.sum(p[0:1] * p[1:2], axis=1, keepdims=True)
    s2 = jnp.sum(p[2:3] * p[3:4], axis=1, keepdims=True)
    return jnp.exp(s1) - jnp.exp(s2) + lam_init


def _diff_finish(acc, l, t, lam, sg, lam_init):
    o = acc[:t] / l[:t] - lam * (acc[t:] / l[t:])
    o = o * lax.rsqrt(jnp.mean(o * o, axis=-1, keepdims=True) + EPS) * sg
    return o * (1.0 - lam_init)


def _diff_prompt_kernel(tbl_ref, q_ref, kt_ref, v_ref, bias_ref, lam_ref, sg_ref, o_ref,
                        vx_ref, m_ref, l_ref, acc_ref, *, tq, tk, far_bucket, lam_init):
    h = pl.program_id(1)
    i = pl.program_id(2)

    @pl.when(i == 0)
    def _():
        vx_ref[:, :SLAB] = v_ref[...]
        vx_ref[:, SLAB:] = jnp.ones((vx_ref.shape[0], SLAB), vx_ref.dtype)

    qq = _two_maps(q_ref[...])
    m_ref[...] = jnp.full(m_ref.shape, NEG, F32)
    l_ref[...] = jnp.zeros(l_ref.shape, F32)
    acc_ref[...] = jnp.zeros(acc_ref.shape, F32)
    n_sub = tq // tk

    def update(jk, bias):
        s = _dot(qq, kt_ref[:, _cols(jk, tk)])
        if bias is not None:
            s = s + jnp.concatenate([bias, bias], axis=0)
        m_old = m_ref[...]
        m_new = jnp.maximum(m_old, jnp.max(s, axis=1, keepdims=True))
        alpha = jnp.exp(m_old - m_new)
        p = jnp.exp(s - _lane_tiles(m_new, tk // SLAB))
        pv = _dot(p.astype(BF16), vx_ref[_cols(jk, tk), :])
        l_ref[...] = alpha * l_ref[...] + pv[:, SLAB:]
        acc_ref[...] = alpha * acc_ref[...] + pv[:, :SLAB]
        m_ref[...] = m_new

    def far_body(j, carry):
        for sub in range(n_sub):
            update(j * n_sub + sub, None)
        return carry

    lax.fori_loop(0, jnp.maximum(i - 1, 0), far_body, 0)
    m_ref[...] = m_ref[...] + tbl_ref[far_bucket, h]

    @pl.when(i >= 1)
    def _():
        for sub in range(n_sub):
            update((i - 1) * n_sub + sub, bias_ref[1, :, sub * tk:(sub + 1) * tk])

    for sub in range(n_sub):
        update(i * n_sub + sub, bias_ref[0, :, sub * tk:(sub + 1) * tk])
    lam = _diff_lambda(lam_ref, lam_init)
    o = _diff_finish(acc_ref[...], l_ref[...], tq, lam, sg_ref[...], lam_init)
    o_ref[...] = o.astype(o_ref.dtype)


def _diff_prompt(rel_table, qa, kat, va, bias, lam_p, sg, tq, far_bucket, lam_init):
    b, t, _ = qa.shape
    kern = functools.partial(_diff_prompt_kernel, tq=tq, tk=_tile(tq, 256), far_bucket=far_bucket,
                             lam_init=lam_init)
    return pl.pallas_call(
        kern,
        grid=(b, H_A, t // tq),
        in_specs=[pl.BlockSpec(memory_space=pltpu.SMEM),
                  pl.BlockSpec((None, tq, SLAB), lambda b, h, i: (b, i, h)),
                  pl.BlockSpec((None, SLAB, t), lambda b, h, i: (b, h, 0)),
                  pl.BlockSpec((None, t, SLAB), lambda b, h, i: (b, 0, h)),
                  pl.BlockSpec((None, 2, tq, tq), lambda b, h, i: (h, 0, 0, 0)),
                  pl.BlockSpec(lam_p.shape, lambda b, h, i: (0, 0)),
                  pl.BlockSpec((1, SLAB), lambda b, h, i: (0, 0))],
        out_specs=pl.BlockSpec((None, tq, SLAB), lambda b, h, i: (b, i, h)),
        out_shape=jax.ShapeDtypeStruct((b, t, COLS), BF16),
        scratch_shapes=[pltpu.VMEM((t, 2 * SLAB), BF16),
                        pltpu.VMEM((2 * tq, SLAB), F32),
                        pltpu.VMEM((2 * tq, SLAB), F32),
                        pltpu.VMEM((2 * tq, SLAB), F32)],
        compiler_params=pltpu.CompilerParams(
            dimension_semantics=("arbitrary", "arbitrary", "arbitrary"),
            vmem_limit_bytes=VMEM_LIMIT),
        name="diff_prompt",
    )(rel_table, qa, kat, va, bias, lam_p, sg)


def _diff_sample_kernel(q_ref, kct_ref, vc_ref, kn_ref, vn_ref, bc_ref, bn_ref, lam_ref, sg_ref,
                        o_ref, *, lam_init):
    qq = _two_maps(q_ref[...])
    t = q_ref.shape[0]
    bc = bc_ref[...]
    bn = bn_ref[...]
    s_c = _dot(qq, kct_ref[...].astype(BF16)) + jnp.concatenate([bc, bc], axis=0)
    s_n = _dot_nt(qq, kn_ref[...]) + jnp.concatenate([bn, bn], axis=0)
    m = jnp.maximum(jnp.max(s_c, axis=1, keepdims=True), jnp.max(s_n, axis=1, keepdims=True))
    p_c = jnp.exp(s_c - m)
    p_n = jnp.exp(s_n - m)
    l = jnp.sum(p_c, axis=1, keepdims=True) + jnp.sum(p_n, axis=1, keepdims=True)
    acc = _dot(p_c.astype(BF16), vc_ref[...].astype(BF16)) + _dot(p_n.astype(BF16), vn_ref[...])
    lam = _diff_lambda(lam_ref, lam_init)
    o = _diff_finish(acc, l, t, lam, sg_ref[...], lam_init)
    o_ref[...] = o.astype(o_ref.dtype)


def _diff_sample(qa, kct, vc, kn, vn, bias_c, bias_n, lam_p, sg, lam_init):
    b, t, _ = qa.shape
    p = vc.shape[1]
    tn = kn.shape[1]
    slab = lambda rows: pl.BlockSpec((None, rows, SLAB), lambda b, h: (b, 0, h))
    return pl.pallas_call(
        functools.partial(_diff_sample_kernel, lam_init=lam_init),
        grid=(b, H_A),
        in_specs=[slab(t),
                  pl.BlockSpec((None, SLAB, p), lambda b, h: (b, h, 0)),
                  slab(p), slab(tn), slab(tn),
                  pl.BlockSpec((None, t, p), lambda b, h: (h, 0, 0)),
                  pl.BlockSpec((None, t, tn), lambda b, h: (h, 0, 0)),
                  pl.BlockSpec(lam_p.shape, lambda b, h: (0, 0)),
                  pl.BlockSpec((1, SLAB), lambda b, h: (0, 0))],
        out_specs=slab(t),
        out_shape=jax.ShapeDtypeStruct((b, t, COLS), BF16),
        compiler_params=pltpu.CompilerParams(
            dimension_semantics=("arbitrary", "arbitrary"), vmem_limit_bytes=VMEM_LIMIT),
        name="diff_sample",
    )(qa, kct, vc, kn, vn, bias_c, bias_n, lam_p, sg)


def _sb_weights(z, u, c, allowed):
    sp = jnp.log(1.0 + jnp.exp(-jnp.abs(z)))
    log_beta = jnp.minimum(z, 0.0) - sp
    log_keep = -jnp.maximum(z, 0.0) - sp
    if allowed is not None:
        log_keep = jnp.where(allowed, log_keep, 0.0)
    hi, lo = _split_bf16(log_keep)
    later = _dot(hi, u) + _dot(lo, u)
    a = jnp.exp(log_beta + later + _lane_tiles(c, z.shape[1] // SLAB))
    if allowed is not None:
        a = jnp.where(allowed, a, 0.0)
    return a.astype(BF16), c + jnp.sum(log_keep, axis=1, keepdims=True)


def _strictly_causal_two_heads(t, n_keys):
    row = lax.broadcasted_iota(jnp.int32, (2 * t, n_keys), 0)
    col = lax.broadcasted_iota(jnp.int32, (2 * t, n_keys), 1)
    return col < jnp.where(row >= t, row - t, row)


def _merge_heads(acc, t):
    lane = lax.broadcasted_iota(jnp.int32, (t, SLAB), 1)
    return jnp.where(lane < DH, acc[:t], acc[t:])


def _sb_sweep_left(j0, c, acc, tile):
    def alive(c):
        return (jnp.max(c) > SB_DEAD).astype(jnp.int32)

    def cond(st):
        j, live, _, _ = st
        return (j >= 0) & (live > 0)

    def body(st):
        j, _, c, acc = st
        c, acc = tile(j, c, acc)
        return j - 1, alive(c), c, acc

    _, _, _, acc = lax.while_loop(cond, body, (j0, alive(c), c, acc))
    return acc


def _sb_prompt_kernel(q_ref, kt_ref, vt_ref, u_ref, o_ref, *, tq):
    i = pl.program_id(2)
    qq = _two_maps(q_ref[...])
    u = u_ref[...]

    def tile(j, c, acc, mask=None):
        a, c = _sb_weights(_dot(qq, kt_ref[:, _cols(j, tq)]), u, c, mask)
        return c, acc + _dot_nt(a, vt_ref[:, _cols(j, tq)])

    zeros = jnp.zeros((2 * tq, SLAB), F32)
    c, acc = tile(i, zeros, zeros, _strictly_causal_two_heads(tq, tq))
    acc = _sb_sweep_left(i - 1, c, acc, tile)
    o_ref[...] = _merge_heads(acc, tq).astype(o_ref.dtype)


def _sb_prompt(qb, kbt, vbt, u, tq):
    b, t, _ = qb.shape
    return pl.pallas_call(
        functools.partial(_sb_prompt_kernel, tq=tq),
        grid=(b, N_SLAB_B, t // tq),
        in_specs=[pl.BlockSpec((None, tq, SLAB), lambda b, h, i: (b, i, h)),
                  pl.BlockSpec((None, SLAB, t), lambda b, h, i: (b, h, 0)),
                  pl.BlockSpec((None, SLAB, t), lambda b, h, i: (b, h, 0)),
                  pl.BlockSpec(u.shape, lambda b, h, i: (0, 0))],
        out_specs=pl.BlockSpec((None, tq, SLAB), lambda b, h, i: (b, i, h)),
        out_shape=jax.ShapeDtypeStruct((b, t, COLS), BF16),
        compiler_params=pltpu.CompilerParams(
            dimension_semantics=("arbitrary", "arbitrary", "arbitrary"),
            vmem_limit_bytes=VMEM_LIMIT),
        name="sb_prompt",
    )(qb, kbt, vbt, u)


def _sb_sample_kernel(q_ref, kct_ref, vct_ref, kn_ref, vn_ref, u_ref, un_ref, o_ref, *, tk):
    t = q_ref.shape[0]
    tn = kn_ref.shape[0]
    n_tiles = kct_ref.shape[1] // tk
    qq = _two_maps(q_ref[...])
    u = u_ref[...]
    zeros = jnp.zeros((2 * t, SLAB), F32)
    a, c = _sb_weights(_dot_nt(qq, kn_ref[...]), un_ref[...], zeros,
                       _strictly_causal_two_heads(t, tn))
    acc = _dot(a, vn_ref[...])

    def tile(j, c, acc):
        a, c = _sb_weights(_dot(qq, kct_ref[:, _cols(j, tk)].astype(BF16)), u, c, None)
        return c, acc + _dot_nt(a, vct_ref[:, _cols(j, tk)].astype(BF16))

    acc = _sb_sweep_left(n_tiles - 1, c, acc, tile)
    o_ref[...] = _merge_heads(acc, t).astype(o_ref.dtype)


def _sb_sample(qb, kct, vct, kn, vn, u, un, tk):
    b, t, _ = qb.shape
    p = kct.shape[2]
    tn = kn.shape[1]
    slab = lambda rows: pl.BlockSpec((None, rows, SLAB), lambda b, h: (b, 0, h))
    slab_t = pl.BlockSpec((None, SLAB, p), lambda b, h: (b, h, 0))
    return pl.pallas_call(
        functools.partial(_sb_sample_kernel, tk=tk),
        grid=(b, N_SLAB_B),
        in_specs=[slab(t), slab_t, slab_t, slab(tn), slab(tn),
                  pl.BlockSpec(u.shape, lambda b, h: (0, 0)),
                  pl.BlockSpec(un.shape, lambda b, h: (0, 0))],
        out_specs=slab(t),
        out_shape=jax.ShapeDtypeStruct((b, t, COLS), BF16),
        compiler_params=pltpu.CompilerParams(
            dimension_semantics=("arbitrary", "arbitrary"), vmem_limit_bytes=VMEM_LIMIT),
        name="sb_sample",
    )(qb, kct, vct, kn, vn, u, un)


def _upper_sum_matrix(n):
    j = np.arange(n)[:, None]
    s = np.arange(n)[None, :]
    return jnp.asarray((j > s).astype(np.float32), dtype=BF16)


def _route_class(logits):
    lane = lax.broadcasted_iota(jnp.int32, logits.shape, 1).astype(F32)
    big = float(ROUTER_LANES)

    def first_argmax(vals):
        top = jnp.max(vals, axis=1, keepdims=True)
        return jnp.min(jnp.where(vals == top, lane, big), axis=1, keepdims=True)

    g_sel = first_argmax(jnp.where(lane < N_GROUPS, logits, NEG))
    first = N_GROUPS + g_sel * EXPERTS_PER_GROUP
    in_group = (lane >= first) & (lane < first + EXPERTS_PER_GROUP)
    el = jnp.where(in_group, logits, NEG)
    i1 = first_argmax(el)
    i2 = first_argmax(jnp.where(lane == i1, NEG, el))
    a = jnp.minimum(i1, i2) - first
    b = jnp.maximum(i1, i2) - first
    pair = a * (2 * EXPERTS_PER_GROUP - 1 - a) * 0.5 + (b - a - 1.0)
    return g_sel * N_PAIRS + pair


def _tail_kernel(x_ref, ya_ref, yb_ref, gmix_ref, wg_ref, bg_ref, wod_ref, wos_ref, wout_ref,
                 gffn_ref, wr_ref, br_ref, low_ref, x2_ref, route_ref, counts_ref, run_ref):
    @pl.when(pl.program_id(0) == 0)
    def _():
        run_ref[...] = jnp.zeros(run_ref.shape, F32)

    x = x_ref[...]
    d = x.shape[1]
    h = _rms(x, gmix_ref[...]).astype(BF16)
    gates = _sigmoid(_dot(h, wg_ref[...]) + bg_ref[...])
    mix = gates[:, :d] * _dot(ya_ref[...], wod_ref[...]) + gates[:, d:] * _dot(yb_ref[...], wos_ref[...])
    x2 = x + _dot(mix.astype(BF16), wout_ref[...])
    x2_ref[...] = x2
    h2 = _rms(x2, gffn_ref[...]).astype(BF16)
    logits = _dot(h2, wr_ref[...]) + br_ref[...]
    cls = _route_class(logits)
    lane = lax.broadcasted_iota(jnp.int32, logits.shape, 1).astype(F32)
    onehot = jnp.where(lane == cls, 1.0, 0.0)
    earlier = _dot(low_ref[...], onehot.astype(BF16)) + run_ref[...]
    rank = jnp.sum(earlier * onehot, axis=1, keepdims=True)
    route_ref[...] = jnp.where(lane == 0.0, cls, jnp.where(lane == 1.0, rank, 0.0))
    run = run_ref[...] + jnp.sum(onehot, axis=0, keepdims=True)
    run_ref[...] = run
    counts_ref[...] = run


def _tail(x2d, ya, yb, g_mix, wg, bg, wod, wos, wout, g_ffn, wr, br, tm):
    n, d = x2d.shape
    row = lambda i: (i, 0)
    const = lambda i: (0, 0)
    full = lambda a: pl.BlockSpec(a.shape, const)
    r = np.arange(tm)
    low = jnp.asarray((r[None, :] < r[:, None]).astype(np.float32), dtype=BF16)
    return pl.pallas_call(
        _tail_kernel,
        grid=(n // tm,),
        in_specs=[pl.BlockSpec((tm, d), row),
                  pl.BlockSpec((tm, COLS), row),
                  pl.BlockSpec((tm, COLS), row),
                  full(g_mix), full(wg), full(bg), full(wod), full(wos), full(wout),
                  full(g_ffn), full(wr), full(br), full(low)],
        out_specs=[pl.BlockSpec((tm, d), row),
                   pl.BlockSpec((tm, ROUTER_LANES), row),
                   pl.BlockSpec((1, ROUTER_LANES), const)],
        out_shape=[jax.ShapeDtypeStruct((n, d), F32),
                   jax.ShapeDtypeStruct((n, ROUTER_LANES), F32),
                   jax.ShapeDtypeStruct((1, ROUTER_LANES), F32)],
        scratch_shapes=[pltpu.VMEM((1, ROUTER_LANES), F32)],
        compiler_params=pltpu.CompilerParams(
            dimension_semantics=("arbitrary",), vmem_limit_bytes=VMEM_LIMIT),
        name="tail",
    )(x2d, ya, yb, g_mix, wg, bg, wod, wos, wout, g_ffn, wr, br, low)


def _class_experts_np():
    ea, eb = [], []
    for g in range(N_GROUPS):
        for a in range(EXPERTS_PER_GROUP):
            for b in range(a + 1, EXPERTS_PER_GROUP):
                ea.append(g * EXPERTS_PER_GROUP + a)
                eb.append(g * EXPERTS_PER_GROUP + b)
    return np.asarray(ea, np.int32), np.asarray(eb, np.int32)


def _dispatch_plan(route, counts, tm):
    n = route.shape[0]
    cls = route[:, 0].astype(jnp.int32)
    rank = route[:, 1].astype(jnp.int32)
    cnt = counts[0, :N_CLASSES].astype(jnp.int32)
    tiles = (cnt + tm - 1) // tm
    tile_end = jnp.cumsum(tiles)
    pos = (tile_end - tiles)[cls] * tm + rank
    n_tiles = n // tm + N_CLASSES
    t_idx = jnp.arange(n_tiles, dtype=jnp.int32)
    used = t_idx < tile_end[-1]
    tile_cls = jnp.sum((t_idx[:, None] >= tile_end[None, :]).astype(jnp.int32), axis=1)
    tile_cls = jnp.where(used, tile_cls, jnp.max(jnp.where(used, tile_cls, 0)))
    ea_np, eb_np = _class_experts_np()
    return pos, jnp.asarray(ea_np)[tile_cls], jnp.asarray(eb_np)[tile_cls], used.astype(jnp.int32)


def _row_copies(n_rows, start_one, wait_one):
    def issue(r, carry):
        start_one(r)
        return carry

    def drain(r, carry):
        wait_one()
        return carry

    lax.fori_loop(0, n_rows, issue, 0, unroll=8)
    lax.fori_loop(0, n_rows, drain, 0, unroll=8)


def _row(ref, r):
    return ref.at[pl.ds(r, 1), :]


def _dispatch_kernel(pos_ref, x_ref, init_hbm, xs_hbm, sem):
    del init_hbm
    _row_copies(
        x_ref.shape[0],
        lambda r: pltpu.make_async_copy(_row(x_ref, r), _row(xs_hbm, pos_ref[0, r]), sem).start(),
        lambda: pltpu.make_async_copy(_row(x_ref, 0), _row(xs_hbm, 0), sem).wait())


def _dispatch(pos, x2, n_slots, tm):
    n, d = x2.shape
    return pl.pallas_call(
        _dispatch_kernel,
        grid=(n // tm,),
        in_specs=[pl.BlockSpec((None, 1, tm), lambda i: (i, 0, 0), memory_space=pltpu.SMEM),
                  pl.BlockSpec((tm, d), lambda i: (i, 0)),
                  pl.BlockSpec(memory_space=pl.ANY)],
        out_specs=pl.BlockSpec(memory_space=pl.ANY),
        out_shape=jax.ShapeDtypeStruct((n_slots, d), F32),
        scratch_shapes=[pltpu.SemaphoreType.DMA],
        input_output_aliases={2: 0},
        compiler_params=pltpu.CompilerParams(dimension_semantics=("arbitrary",)),
        name="moe_dispatch",
    )(pos.reshape(n // tm, 1, tm), x2, jnp.zeros((n_slots, d), F32))


def _combine_kernel(pos_ref, ys_hbm, y_ref, sem):
    _row_copies(
        y_ref.shape[0],
        lambda r: pltpu.make_async_copy(_row(ys_hbm, pos_ref[0, r]), _row(y_ref, r), sem).start(),
        lambda: pltpu.make_async_copy(_row(ys_hbm, 0), _row(y_ref, 0), sem).wait())


def _combine(pos, ys, n, tm):
    d = ys.shape[1]
    return pl.pallas_call(
        _combine_kernel,
        grid=(n // tm,),
        in_specs=[pl.BlockSpec((None, 1, tm), lambda i: (i, 0, 0), memory_space=pltpu.SMEM),
                  pl.BlockSpec(memory_space=pl.ANY)],
        out_specs=pl.BlockSpec((tm, d), lambda i: (i, 0)),
        out_shape=jax.ShapeDtypeStruct((n, d), F32),
        scratch_shapes=[pltpu.SemaphoreType.DMA],
        compiler_params=pltpu.CompilerParams(dimension_semantics=("arbitrary",)),
        name="moe_combine",
    )(pos.reshape(n // tm, 1, tm), ys)


def _moe_kernel(ea_ref, eb_ref, used_ref, xs_ref, g_ref, wr_ref, br_ref,
                mga_ref, mua_ref, mda_ref, mgb_ref, mub_ref, mdb_ref, ys_ref):
    t = pl.program_id(0)

    @pl.when(used_ref[t] == 0)
    def _():
        ys_ref[...] = jnp.zeros(ys_ref.shape, F32)

    @pl.when(used_ref[t] != 0)
    def _():
        x = xs_ref[...]
        h2 = _rms(x, g_ref[...]).astype(BF16)
        logits = _dot(h2, wr_ref[...]) + br_ref[...]
        lane = lax.broadcasted_iota(jnp.int32, logits.shape, 1)
        pick = lambda e: jnp.sum(jnp.where(lane == N_GROUPS + e, logits, 0.0), axis=1, keepdims=True)
        la = pick(ea_ref[t])
        lb = pick(eb_ref[t])
        top = jnp.maximum(la, lb)
        pa = jnp.exp(la - top)
        pb = jnp.exp(lb - top)

        def expert(mg_ref, mu_ref, md_ref):
            g = _dot(h2, mg_ref[...])
            act = g * _sigmoid(g) * _dot(h2, mu_ref[...])
            return _dot(act.astype(BF16), md_ref[...])

        ya = expert(mga_ref, mua_ref, mda_ref)
        yb = expert(mgb_ref, mub_ref, mdb_ref)
        ys_ref[...] = x + ((pa / (pa + pb)) * ya + (pb / (pa + pb)) * yb)


def _moe(ea, eb, used, xs, g_ffn, wr, br, mg, mu, md, tm):
    n_slots, d = xs.shape
    de = mg.shape[2]
    const = lambda t, ea, eb, used: (0, 0)
    w_a = lambda shape: pl.BlockSpec((None,) + shape, lambda t, ea, eb, used: (ea[t], 0, 0))
    w_b = lambda shape: pl.BlockSpec((None,) + shape, lambda t, ea, eb, used: (eb[t], 0, 0))
    grid_spec = pltpu.PrefetchScalarGridSpec(
        num_scalar_prefetch=3,
        grid=(n_slots // tm,),
        in_specs=[pl.BlockSpec((tm, d), lambda t, ea, eb, used: (t, 0)),
                  pl.BlockSpec(g_ffn.shape, const),
                  pl.BlockSpec(wr.shape, const),
                  pl.BlockSpec(br.shape, const),
                  w_a((d, de)), w_a((d, de)), w_a((de, d)),
                  w_b((d, de)), w_b((d, de)), w_b((de, d))],
        out_specs=pl.BlockSpec((tm, d), lambda t, ea, eb, used: (t, 0)))
    return pl.pallas_call(
        _moe_kernel,
        grid_spec=grid_spec,
        out_shape=jax.ShapeDtypeStruct((n_slots, d), F32),
        compiler_params=pltpu.CompilerParams(
            dimension_semantics=("arbitrary",), vmem_limit_bytes=VMEM_LIMIT),
        name="moe",
    )(ea, eb, used, xs, g_ffn, wr, br, mg, mu, md, mg, mu, md)


def _tile(n, pref):
    return pref if n % pref == 0 else n


def _feature_major(cache):
    b, p = cache.shape[:2]
    nd = cache.ndim
    return jnp.transpose(cache, (0,) + tuple(range(2, nd)) + (1,)).reshape(b, COLS, p)


def _token_major(xt, head_dims):
    b, _, t = xt.shape
    nd = len(head_dims)
    return jnp.transpose(xt.reshape((b,) + head_dims + (t,)), (0, nd + 1) + tuple(range(1, nd + 1)))


def kernel(x_prompt, x_sample, cache_diff_k, cache_diff_v, cache_sb_k, cache_sb_v, rel_bias_table, norm_mix_g, w_in, q_norm_g, k_norm_g, lambda_q1, lambda_k1, lambda_q2, lambda_k2, subln_g, w_o_diff, w_o_sb, w_branch_gate, b_branch_gate, w_out, norm_ffn_g, w_router_group, b_router_group, w_router_expert, b_router_expert, moe_w_gate, moe_w_up, moe_w_down):
    b, t, d = x_prompt.shape
    bs, ts, _ = x_sample.shape
    depth = norm_mix_g.shape[0]
    p = cache_diff_k.shape[2]
    tq = _tile(t, 512)
    tq_b = _tile(t, 256)
    tk_s = _tile(p, 256)
    tn = 128
    assert t % tq == 0 and t % tq_b == 0 and tq % CHUNK == 0 and p % CHUNK == 0
    assert ts <= CHUNK and ts <= tn
    assert p % tk_s == 0

    q_loc = np.arange(tq)
    bkt_prompt = np.concatenate([_bucket_tile_np(q_loc + tq, np.arange(tq) + tq),
                                 _bucket_tile_np(q_loc + tq, np.arange(tq))], axis=0)
    far = np.unique(_t5_bucket_np(-np.arange(tq + 1, 4 * tq)))
    assert far.size == 1
    far_bucket = int(far[0])
    q_s = p + np.arange(ts)
    k_s = np.concatenate([np.arange(p), p + np.arange(tn)])
    bkt_sample = _bucket_tile_np(q_s, k_s)
    bkt_sample[:, p + ts:] = -1
    bias_prompt = _bias_tiles(rel_bias_table, jnp.asarray(bkt_prompt)).reshape(H_A, 2, tq, tq)
    bias_sample = _bias_tiles(rel_bias_table, jnp.asarray(bkt_sample))
    bias_sc, bias_sn = bias_sample[:, :, :p], bias_sample[:, :, p:]

    gm = np.kron(np.eye(256 // DH), np.full((DH, DH), 1.0 / DH)).astype(np.float32)
    gm = jnp.asarray(gm, dtype=BF16)
    u_p = _upper_sum_matrix(tq_b)
    u_s = _upper_sum_matrix(tk_s)
    u_n = _upper_sum_matrix(tn)

    y_p = x_prompt
    y_s = x_sample.reshape(bs * ts, d)
    tm_proj = _tile(t, 512)
    tm_p = _tile(b * t, 512)
    tm_s = _tile(bs * ts, 128)
    tm_moe_p = _tile(b * t, 512)
    outs = [[] for _ in range(8)]
    for l in range(depth):
        lam_init = 0.8 - 0.6 * math.exp(-0.3 * l)
        w_in_b = w_in[l].astype(BF16)
        col = lambda g: w_in_b[:, g * COLS:(g + 1) * COLS]
        wn = jnp.concatenate([col(0), col(2), col(3)], axis=1)
        wt = jnp.concatenate([col(1), col(4), col(5)], axis=1).T
        g_mix = norm_mix_g[l].reshape(1, d)
        gq = jnp.tile(q_norm_g[l], COLS // DH).reshape(1, COLS)
        gk = jnp.tile(k_norm_g[l], COLS // DH).reshape(1, COLS)
        gkt = jnp.broadcast_to(gk.reshape(COLS, 1), (COLS, SLAB))
        lam_p = jnp.stack([lambda_q1[l], lambda_k1[l], lambda_q2[l], lambda_k2[l]])
        sg = subln_g[l].reshape(1, SLAB)
        wg = w_branch_gate[l].astype(BF16)
        bg = b_branch_gate[l].reshape(1, 2 * d)
        wod = w_o_diff[l].astype(BF16)
        wos = w_o_sb[l].astype(BF16)
        wout = w_out[l].astype(BF16)
        g_ffn = norm_ffn_g[l].reshape(1, d)
        n_r = N_GROUPS + N_EXPERTS
        wr = jnp.pad(jnp.concatenate([w_router_group[l], w_router_expert[l]], axis=1),
                     ((0, 0), (0, ROUTER_LANES - n_r)))
        wr = wr.astype(BF16)
        br = jnp.pad(jnp.concatenate([b_router_group[l], b_router_expert[l]]),
                     (0, ROUTER_LANES - n_r)).reshape(1, ROUTER_LANES)
        mg = moe_w_gate[l].astype(BF16)
        mu = moe_w_up[l].astype(BF16)
        md = moe_w_down[l].astype(BF16)

        def tail_and_moe(x2d, ya, yb, tm, tm_moe):
            n = x2d.shape[0]
            x2, route, counts = _tail(x2d, ya, yb, g_mix, wg, bg, wod, wos, wout, g_ffn, wr, br, tm)
            pos, ea, eb, used = _dispatch_plan(route, counts, tm_moe)
            xs = _dispatch(pos, x2, (n // tm_moe + N_CLASSES) * tm_moe, tm_moe)
            ys = _moe(ea, eb, used, xs, g_ffn, wr, br, mg, mu, md, tm_moe)
            return _combine(pos, ys, n, tm_moe)

        kat, va, kbt, vbt, qa_b, kat_b, va_b, qb_b, kbt_b, vbt_b = _project_t(
            y_p, g_mix, wn, wt, gq, gkt, gm, tm_proj)
        ya = _diff_prompt(rel_bias_table, qa_b, kat_b, va_b, bias_prompt, lam_p, sg,
                          tq, far_bucket, lam_init)
        yb = _sb_prompt(qb_b, kbt_b, vbt_b, u_p, tq_b)
        y_p = tail_and_moe(y_p.reshape(b * t, d), ya.reshape(b * t, COLS), yb.reshape(b * t, COLS),
                           tm_p, tm_moe_p).reshape(b, t, d)
        outs[0].append(_token_major(kat, (H_A, 2, DH)))
        outs[1].append(va.reshape(b, t, H_A, 2 * DH))
        outs[2].append(_token_major(kbt, (H_B, DH)))
        outs[3].append(_token_major(vbt, (H_B, DH)))

        ka, va, kb, vb, qa_b, ka_b, va_b, qb_b, kb_b, vb_b = _project(y_s, g_mix, w_in_b, gq, gk, gm, tm_s)
        s3 = lambda a: a.reshape(bs, ts, COLS)
        padk = lambda a: jnp.pad(s3(a), ((0, 0), (0, tn - ts), (0, 0)))
        ya = _diff_sample(s3(qa_b), _feature_major(cache_diff_k[l]), cache_diff_v[l].reshape(bs, p, COLS),
                          padk(ka_b), padk(va_b), bias_sc, bias_sn, lam_p, sg, lam_init)
        yb = _sb_sample(s3(qb_b), _feature_major(cache_sb_k[l]), _feature_major(cache_sb_v[l]),
                        padk(kb_b), padk(vb_b), u_s, u_n, tk_s)
        y_s = tail_and_moe(y_s, ya.reshape(bs * ts, COLS), yb.reshape(bs * ts, COLS), tm_s, tm_s)
        outs[4].append(ka.reshape(bs, ts, H_A, 2, DH))
        outs[5].append(va.reshape(bs, ts, H_A, 2 * DH))
        outs[6].append(kb.reshape(bs, ts, H_B, DH))
        outs[7].append(vb.reshape(bs, ts, H_B, DH))

    return (y_p, y_s.reshape(bs, ts, d)) + tuple(jnp.stack(o) for o in outs)
```

```python
import functools
import math

import numpy as np
import jax
import jax.numpy as jnp
from jax import lax
from jax.experimental import pallas as pl
from jax.experimental.pallas import tpu as pltpu

F32 = jnp.float32
BF16 = jnp.bfloat16

EPS = 1e-6
CHUNK = 64
H_A = 4
DH = 64
H_B = 8
SLAB = 2 * DH
N_SLAB_A = H_A
N_SLAB_B = H_B // 2
COLS = H_A * SLAB
N_BUCKETS = 32
MAX_DISTANCE = 128
N_GROUPS = 4
EXPERTS_PER_GROUP = 4
N_EXPERTS = N_GROUPS * EXPERTS_PER_GROUP
N_PAIRS = EXPERTS_PER_GROUP * (EXPERTS_PER_GROUP - 1) // 2
N_CLASSES = N_GROUPS * N_PAIRS
ROUTER_LANES = 128
NEG = -1e30
SB_DEAD = -104.0
VMEM_LIMIT = 56 * 1024 * 1024
LOG2E = math.log2(math.e)
FAR_UNROLL = 4


def _rms(x, g):
    return x * lax.rsqrt(jnp.mean(x * x, axis=-1, keepdims=True) + EPS) * g


def _sigmoid(x):
    return 1.0 / (1.0 + jnp.exp(-x))


def _dot(a, b):
    return jnp.dot(a, b, preferred_element_type=F32)


def _dot_nt(a, b):
    return lax.dot_general(a, b, (((1,), (1,)), ((), ())), preferred_element_type=F32)


def _lane_tiles(x, n):
    return x if n == 1 else jnp.concatenate([x] * n, axis=1)


def _split_bf16(x):
    hi = x.astype(BF16)
    lo = (x - hi.astype(F32)).astype(BF16)
    return hi, lo


def _cols(j, width):
    return pl.ds(pl.multiple_of(j * width, width), width)


def _sub_head_norm(t, gain, gm):
    outs = []
    for s in range(0, COLS, 256):
        ts = t[:, s:s + 256]
        hi, lo = _split_bf16(ts * ts)
        outs.append(ts * lax.rsqrt(_dot(hi, gm) + _dot(lo, gm) + EPS))
    return jnp.concatenate(outs, axis=1) * gain


def _sub_head_norm_t(t, gain, gm):
    outs = []
    for s in range(0, COLS, 256):
        ts = t[s:s + 256, :]
        hi, lo = _split_bf16(ts * ts)
        outs.append(ts * lax.rsqrt(_dot(gm, hi) + _dot(gm, lo) + EPS))
    return jnp.concatenate(outs, axis=0) * gain


def _proj_kernel(x_ref, g_ref, w_ref, gq_ref, gk_ref, gm_ref,
                 ka_o, va_o, kb_o, vb_o, qa_b, ka_b, va_b, qb_b, kb_b, vb_b):
    h = _rms(x_ref[...], g_ref[...]).astype(BF16)
    proj = _dot(h, w_ref[...])
    gm = gm_ref[...]
    qa = _sub_head_norm(proj[:, 0 * COLS:1 * COLS], gq_ref[...], gm)
    ka = _sub_head_norm(proj[:, 1 * COLS:2 * COLS], gk_ref[...], gm)
    va = proj[:, 2 * COLS:3 * COLS]
    qb = proj[:, 3 * COLS:4 * COLS]
    kb = proj[:, 4 * COLS:5 * COLS]
    vb = proj[:, 5 * COLS:6 * COLS]
    ka_o[...] = ka
    va_o[...] = va
    kb_o[...] = kb
    vb_o[...] = vb
    scale = 1.0 / math.sqrt(DH)
    qa_b[...] = (qa * (scale * LOG2E)).astype(BF16)
    ka_b[...] = ka.astype(BF16)
    va_b[...] = va.astype(BF16)
    qb_b[...] = (qb * scale).astype(BF16)
    kb_b[...] = kb.astype(BF16)
    vb_b[...] = vb.astype(BF16)


def _project(x2d, g_mix, w_in_b, gq, gk, gm, tm):
    n, d = x2d.shape
    row = lambda i: (i, 0)
    const = lambda i: (0, 0)
    out_f = jax.ShapeDtypeStruct((n, COLS), F32)
    out_b = jax.ShapeDtypeStruct((n, COLS), BF16)
    blk = pl.BlockSpec((tm, COLS), row)
    return pl.pallas_call(
        _proj_kernel,
        grid=(n // tm,),
        in_specs=[pl.BlockSpec((tm, d), row),
                  pl.BlockSpec((1, d), const),
                  pl.BlockSpec(w_in_b.shape, const),
                  pl.BlockSpec((1, COLS), const),
                  pl.BlockSpec((1, COLS), const),
                  pl.BlockSpec(gm.shape, const)],
        out_specs=[blk] * 10,
        out_shape=[out_f] * 4 + [out_b] * 6,
        compiler_params=pltpu.CompilerParams(
            dimension_semantics=("arbitrary",), vmem_limit_bytes=VMEM_LIMIT),
        name="proj",
    )(x2d, g_mix, w_in_b, gq, gk, gm)


def _proj_t_kernel(x_ref, g_ref, wn_ref, wt_ref, gq_ref, gkt_ref, gm_ref,
                   kat_o, va_o, kbt_o, vbt_o, qa_b, kat_b, va_b, qb_b, kbt_b, vbt_b):
    h = _rms(x_ref[...], g_ref[...]).astype(BF16)
    tm = h.shape[0]
    nat = _dot(h, wn_ref[...])
    tr = _dot_nt(wt_ref[...], h)
    gm = gm_ref[...]
    qa = _sub_head_norm(nat[:, 0 * COLS:1 * COLS], gq_ref[...], gm)
    va = nat[:, 1 * COLS:2 * COLS]
    qb = nat[:, 2 * COLS:3 * COLS]
    kat = _sub_head_norm_t(tr[0 * COLS:1 * COLS], _lane_tiles(gkt_ref[...], tm // SLAB), gm)
    kbt = tr[1 * COLS:2 * COLS]
    vbt = tr[2 * COLS:3 * COLS]
    kat_o[...] = kat
    va_o[...] = va
    kbt_o[...] = kbt
    vbt_o[...] = vbt
    scale = 1.0 / math.sqrt(DH)
    qa_b[...] = (qa * (scale * LOG2E)).astype(BF16)
    kat_b[...] = kat.astype(BF16)
    va_b[...] = va.astype(BF16)
    qb_b[...] = (qb * scale).astype(BF16)
    kbt_b[...] = kbt.astype(BF16)
    vbt_b[...] = vbt.astype(BF16)


def _project_t(x, g_mix, wn, wt, gq, gkt, gm, tm):
    b, t, d = x.shape
    const = lambda b, i: (0, 0)
    nat = pl.BlockSpec((None, tm, COLS), lambda b, i: (b, i, 0))
    fm = pl.BlockSpec((None, COLS, tm), lambda b, i: (b, 0, i))
    nat_s = lambda dt: jax.ShapeDtypeStruct((b, t, COLS), dt)
    fm_s = lambda dt: jax.ShapeDtypeStruct((b, COLS, t), dt)
    return pl.pallas_call(
        _proj_t_kernel,
        grid=(b, t // tm),
        in_specs=[pl.BlockSpec((None, tm, d), lambda b, i: (b, i, 0)),
                  pl.BlockSpec((1, d), const),
                  pl.BlockSpec(wn.shape, const),
                  pl.BlockSpec(wt.shape, const),
                  pl.BlockSpec((1, COLS), const),
                  pl.BlockSpec(gkt.shape, const),
                  pl.BlockSpec(gm.shape, const)],
        out_specs=[fm, nat, fm, fm, nat, fm, nat, nat, fm, fm],
        out_shape=[fm_s(F32), nat_s(F32), fm_s(F32), fm_s(F32),
                   nat_s(BF16), fm_s(BF16), nat_s(BF16), nat_s(BF16), fm_s(BF16), fm_s(BF16)],
        compiler_params=pltpu.CompilerParams(
            dimension_semantics=("arbitrary", "arbitrary"), vmem_limit_bytes=VMEM_LIMIT),
        name="proj_t",
    )(x, g_mix, wn, wt, gq, gkt, gm)


def _t5_bucket_np(rel):
    half = N_BUCKETS // 2
    max_exact = half // 2
    base = np.where(rel > 0, half, 0)
    n = np.abs(rel)
    nf = np.maximum(n, 1).astype(np.float64)
    large = max_exact + (np.log(nf / max_exact) / math.log(MAX_DISTANCE / max_exact)
                         * (half - max_exact)).astype(np.int32)
    large = np.minimum(large, half - 1)
    return (base + np.where(n < max_exact, n, large)).astype(np.int32)


def _bucket_tile_np(q_pos, k_pos):
    rel = k_pos[None, :] - q_pos[:, None]
    allowed = (k_pos[None, :] // CHUNK) <= (q_pos[:, None] // CHUNK)
    return np.where(allowed, _t5_bucket_np(rel), -1).astype(np.int32)


def _bias_kernel(tbl_ref, bkt_ref, o_ref):
    h = pl.program_id(0)
    bkt = bkt_ref[...]
    out = jnp.full(bkt.shape, NEG, F32)
    for b in range(N_BUCKETS):
        out = jnp.where(bkt == b, tbl_ref[b, h] * LOG2E, out)
    o_ref[...] = out


def _bias_tiles(rel_table, buckets):
    r, c = buckets.shape
    return pl.pallas_call(
        _bias_kernel,
        grid=(H_A,),
        in_specs=[pl.BlockSpec(memory_space=pltpu.SMEM),
                  pl.BlockSpec((r, c), lambda h: (0, 0))],
        out_specs=pl.BlockSpec((None, r, c), lambda h: (h, 0, 0)),
        out_shape=jax.ShapeDtypeStruct((H_A, r, c), F32),
        name="bias_tiles",
    )(rel_table, buckets)


def _two_maps(q):
    lane = lax.broadcasted_iota(jnp.int32, q.shape, 1)
    zero = jnp.zeros_like(q)
    return jnp.concatenate([jnp.where(lane < DH, q, zero), jnp.where(lane >= DH, q, zero)], axis=0)


def _diff_lambda(lam_ref, lam_init):
    p = lam_ref[...]
    s1 = jnp.sum(p[0:1] * p[1:2], axis=1, keepdims=True)
    s2 = jnp.sum(p[2:3] * p[3:4], axis=1, keepdims=True)
    return jnp.exp(s1) - jnp.exp(s2) + lam_init


def _diff_finish(acc, l, t, lam, sg, lam_init):
    o = acc[:t] / l[:t] - lam * (acc[t:] / l[t:])
    o = o * lax.rsqrt(jnp.mean(o * o, axis=-1, keepdims=True) + EPS) * sg
    return o * (1.0 - lam_init)


def _diff_prompt_kernel(tbl_ref, q_ref, kt_ref, v_ref, bias_ref, lam_ref, sg_ref, o_ref,
                        vx_ref, m_ref, l_ref, acc_ref, *, tq, tk, far_bucket, lam_init):
    h = pl.program_id(1)
    i = pl.program_id(2)

    @pl.when(i == 0)
    def _():
        vx_ref[:, :SLAB] = v_ref[...]
        vx_ref[:, SLAB:] = jnp.ones((vx_ref.shape[0], SLAB), vx_ref.dtype)

    qq = _two_maps(q_ref[...])
    m_ref[...] = jnp.full(m_ref.shape, NEG, F32)
    l_ref[...] = jnp.zeros(l_ref.shape, F32)
    acc_ref[...] = jnp.zeros(acc_ref.shape, F32)
    n_sub = tq // tk

    def update(jk, bias):
        s = _dot(qq, kt_ref[:, _cols(jk, tk)])
        if bias is not None:
            s = s + jnp.concatenate([bias, bias], axis=0)
        m_old = m_ref[...]
        m_new = jnp.maximum(m_old, jnp.max(s, axis=1, keepdims=True))
        alpha = jnp.exp2(m_old - m_new)
        p = jnp.exp2(s - _lane_tiles(m_new, tk // SLAB))
        pv = _dot(p.astype(BF16), vx_ref[_cols(jk, tk), :])
        l_ref[...] = alpha * l_ref[...] + pv[:, SLAB:]
        acc_ref[...] = alpha * acc_ref[...] + pv[:, :SLAB]
        m_ref[...] = m_new

    n_far = jnp.maximum(i - 1, 0)

    def far_tiles(first, count):
        for sub in range(count * n_sub):
            update(first * n_sub + sub, None)

    def far_body(j, carry):
        far_tiles(j * FAR_UNROLL, FAR_UNROLL)
        return carry

    lax.fori_loop(0, n_far // FAR_UNROLL, far_body, 0)
    done = n_far - n_far % FAR_UNROLL
    step = FAR_UNROLL // 2
    while step >= 1:
        take = (n_far % (2 * step)) >= step

        @pl.when(take)
        def _(done=done, step=step):
            far_tiles(done, step)

        done = done + jnp.where(take, step, 0)
        step //= 2
    m_ref[...] = m_ref[...] + tbl_ref[far_bucket, h] * LOG2E

    @pl.when(i >= 1)
    def _():
        for sub in range(n_sub):
            update((i - 1) * n_sub + sub, bias_ref[1, :, sub * tk:(sub + 1) * tk])

    for sub in range(n_sub):
        update(i * n_sub + sub, bias_ref[0, :, sub * tk:(sub + 1) * tk])
    lam = _diff_lambda(lam_ref, lam_init)
    o = _diff_finish(acc_ref[...], l_ref[...], tq, lam, sg_ref[...], lam_init)
    o_ref[...] = o.astype(o_ref.dtype)


def _diff_prompt(rel_table, qa, kat, va, bias, lam_p, sg, tq, far_bucket, lam_init):
    b, t, _ = qa.shape
    kern = functools.partial(_diff_prompt_kernel, tq=tq, tk=_tile(tq, 256), far_bucket=far_bucket,
                             lam_init=lam_init)
    return pl.pallas_call(
        kern,
        grid=(b, H_A, t // tq),
        in_specs=[pl.BlockSpec(memory_space=pltpu.SMEM),
                  pl.BlockSpec((None, tq, SLAB), lambda b, h, i: (b, i, h)),
                  pl.BlockSpec((None, SLAB, t), lambda b, h, i: (b, h, 0)),
                  pl.BlockSpec((None, t, SLAB), lambda b, h, i: (b, 0, h)),
                  pl.BlockSpec((None, 2, tq, tq), lambda b, h, i: (h, 0, 0, 0)),
                  pl.BlockSpec(lam_p.shape, lambda b, h, i: (0, 0)),
                  pl.BlockSpec((1, SLAB), lambda b, h, i: (0, 0))],
        out_specs=pl.BlockSpec((None, tq, SLAB), lambda b, h, i: (b, i, h)),
        out_shape=jax.ShapeDtypeStruct((b, t, COLS), BF16),
        scratch_shapes=[pltpu.VMEM((t, 2 * SLAB), BF16),
                        pltpu.VMEM((2 * tq, SLAB), F32),
                        pltpu.VMEM((2 * tq, SLAB), F32),
                        pltpu.VMEM((2 * tq, SLAB), F32)],
        compiler_params=pltpu.CompilerParams(
            dimension_semantics=("arbitrary", "arbitrary", "arbitrary"),
            vmem_limit_bytes=VMEM_LIMIT),
        name="diff_prompt",
    )(rel_table, qa, kat, va, bias, lam_p, sg)


def _diff_sample_kernel(q_ref, kct_ref, vc_ref, kn_ref, vn_ref, bc_ref, bn_ref, lam_ref, sg_ref,
                        o_ref, *, lam_init):
    qq = _two_maps(q_ref[...])
    t = q_ref.shape[0]
    bc = bc_ref[...]
    bn = bn_ref[...]
    s_c = _dot(qq, kct_ref[...].astype(BF16)) + jnp.concatenate([bc, bc], axis=0)
    s_n = _dot_nt(qq, kn_ref[...]) + jnp.concatenate([bn, bn], axis=0)
    m = jnp.maximum(jnp.max(s_c, axis=1, keepdims=True), jnp.max(s_n, axis=1, keepdims=True))
    p_c = jnp.exp2(s_c - m)
    p_n = jnp.exp2(s_n - m)
    l = jnp.sum(p_c, axis=1, keepdims=True) + jnp.sum(p_n, axis=1, keepdims=True)
    acc = _dot(p_c.astype(BF16), vc_ref[...].astype(BF16)) + _dot(p_n.astype(BF16), vn_ref[...])
    lam = _diff_lambda(lam_ref, lam_init)
    o = _diff_finish(acc, l, t, lam, sg_ref[...], lam_init)
    o_ref[...] = o.astype(o_ref.dtype)


def _diff_sample(qa, kct, vc, kn, vn, bias_c, bias_n, lam_p, sg, lam_init):
    b, t, _ = qa.shape
    p = vc.shape[1]
    tn = kn.shape[1]
    slab = lambda rows: pl.BlockSpec((None, rows, SLAB), lambda b, h: (b, 0, h))
    return pl.pallas_call(
        functools.partial(_diff_sample_kernel, lam_init=lam_init),
        grid=(b, H_A),
        in_specs=[slab(t),
                  pl.BlockSpec((None, SLAB, p), lambda b, h: (b, h, 0)),
                  slab(p), slab(tn), slab(tn),
                  pl.BlockSpec((None, t, p), lambda b, h: (h, 0, 0)),
                  pl.BlockSpec((None, t, tn), lambda b, h: (h, 0, 0)),
                  pl.BlockSpec(lam_p.shape, lambda b, h: (0, 0)),
                  pl.BlockSpec((1, SLAB), lambda b, h: (0, 0))],
        out_specs=slab(t),
        out_shape=jax.ShapeDtypeStruct((b, t, COLS), BF16),
        compiler_params=pltpu.CompilerParams(
            dimension_semantics=("arbitrary", "arbitrary"), vmem_limit_bytes=VMEM_LIMIT),
        name="diff_sample",
    )(qa, kct, vc, kn, vn, bias_c, bias_n, lam_p, sg)


def _sb_weights(z, u, c, allowed):
    sp = jnp.log(1.0 + jnp.exp(-jnp.abs(z)))
    log_beta = jnp.minimum(z, 0.0) - sp
    log_keep = -jnp.maximum(z, 0.0) - sp
    if allowed is not None:
        log_keep = jnp.where(allowed, log_keep, 0.0)
    hi, lo = _split_bf16(log_keep)
    later = _dot(hi, u) + _dot(lo, u)
    a = jnp.exp(log_beta + later + _lane_tiles(c, z.shape[1] // SLAB))
    if allowed is not None:
        a = jnp.where(allowed, a, 0.0)
    return a.astype(BF16), c + jnp.sum(log_keep, axis=1, keepdims=True)


def _strictly_causal_two_heads(t, n_keys):
    row = lax.broadcasted_iota(jnp.int32, (2 * t, n_keys), 0)
    col = lax.broadcasted_iota(jnp.int32, (2 * t, n_keys), 1)
    return col < jnp.where(row >= t, row - t, row)


def _merge_heads(acc, t):
    lane = lax.broadcasted_iota(jnp.int32, (t, SLAB), 1)
    return jnp.where(lane < DH, acc[:t], acc[t:])


def _sb_sweep_left(j0, c, acc, tile):
    def alive(c):
        return (jnp.max(c) > SB_DEAD).astype(jnp.int32)

    def cond(st):
        j, live, _, _ = st
        return (j >= 0) & (live > 0)

    def body(st):
        j, _, c, acc = st
        c, acc = tile(j, c, acc)
        return j - 1, alive(c), c, acc

    _, _, _, acc = lax.while_loop(cond, body, (j0, alive(c), c, acc))
    return acc


def _sb_prompt_kernel(q_ref, kt_ref, vt_ref, u_ref, o_ref, *, tq):
    i = pl.program_id(2)
    qq = _two_maps(q_ref[...])
    u = u_ref[...]

    def tile(j, c, acc, mask=None, valid=None):
        a, c_new = _sb_weights(_dot(qq, kt_ref[:, _cols(j, tq)]), u, c, mask)
        if valid is not None:
            a = jnp.where(valid, a, jnp.zeros_like(a))
            c_new = jnp.where(valid, c_new, c)
        return c_new, acc + _dot_nt(a, vt_ref[:, _cols(j, tq)])

    zeros = jnp.zeros((2 * tq, SLAB), F32)
    c, acc = tile(i, zeros, zeros, _strictly_causal_two_heads(tq, tq))
    c, acc = tile(jnp.maximum(i - 1, 0), c, acc, valid=i >= 1)
    acc = _sb_sweep_left(i - 2, c, acc, tile)
    o_ref[...] = _merge_heads(acc, tq).astype(o_ref.dtype)


def _sb_prompt(qb, kbt, vbt, u, tq):
    b, t, _ = qb.shape
    return pl.pallas_call(
        functools.partial(_sb_prompt_kernel, tq=tq),
        grid=(b, N_SLAB_B, t // tq),
        in_specs=[pl.BlockSpec((None, tq, SLAB), lambda b, h, i: (b, i, h)),
                  pl.BlockSpec((None, SLAB, t), lambda b, h, i: (b, h, 0)),
                  pl.BlockSpec((None, SLAB, t), lambda b, h, i: (b, h, 0)),
                  pl.BlockSpec(u.shape, lambda b, h, i: (0, 0))],
        out_specs=pl.BlockSpec((None, tq, SLAB), lambda b, h, i: (b, i, h)),
        out_shape=jax.ShapeDtypeStruct((b, t, COLS), BF16),
        compiler_params=pltpu.CompilerParams(
            dimension_semantics=("arbitrary", "arbitrary", "arbitrary"),
            vmem_limit_bytes=VMEM_LIMIT),
        name="sb_prompt",
    )(qb, kbt, vbt, u)


def _sb_sample_kernel(q_ref, kct_ref, vct_ref, kn_ref, vn_ref, u_ref, un_ref, o_ref, *, tk):
    t = q_ref.shape[0]
    tn = kn_ref.shape[0]
    n_tiles = kct_ref.shape[1] // tk
    qq = _two_maps(q_ref[...])
    u = u_ref[...]
    zeros = jnp.zeros((2 * t, SLAB), F32)
    a, c = _sb_weights(_dot_nt(qq, kn_ref[...]), un_ref[...], zeros,
                       _strictly_causal_two_heads(t, tn))
    acc = _dot(a, vn_ref[...])

    def tile(j, c, acc):
        a, c = _sb_weights(_dot(qq, kct_ref[:, _cols(j, tk)].astype(BF16)), u, c, None)
        return c, acc + _dot_nt(a, vct_ref[:, _cols(j, tk)].astype(BF16))

    acc = _sb_sweep_left(n_tiles - 1, c, acc, tile)
    o_ref[...] = _merge_heads(acc, t).astype(o_ref.dtype)


def _sb_sample(qb, kct, vct, kn, vn, u, un, tk):
    b, t, _ = qb.shape
    p = kct.shape[2]
    tn = kn.shape[1]
    slab = lambda rows: pl.BlockSpec((None, rows, SLAB), lambda b, h: (b, 0, h))
    slab_t = pl.BlockSpec((None, SLAB, p), lambda b, h: (b, h, 0))
    return pl.pallas_call(
        functools.partial(_sb_sample_kernel, tk=tk),
        grid=(b, N_SLAB_B),
        in_specs=[slab(t), slab_t, slab_t, slab(tn), slab(tn),
                  pl.BlockSpec(u.shape, lambda b, h: (0, 0)),
                  pl.BlockSpec(un.shape, lambda b, h: (0, 0))],
        out_specs=slab(t),
        out_shape=jax.ShapeDtypeStruct((b, t, COLS), BF16),
        compiler_params=pltpu.CompilerParams(
            dimension_semantics=("arbitrary", "arbitrary"), vmem_limit_bytes=VMEM_LIMIT),
        name="sb_sample",
    )(qb, kct, vct, kn, vn, u, un)


def _upper_sum_matrix(n):
    j = np.arange(n)[:, None]
    s = np.arange(n)[None, :]
    return jnp.asarray((j > s).astype(np.float32), dtype=BF16)


def _route_class(logits):
    lane = lax.broadcasted_iota(jnp.int32, logits.shape, 1).astype(F32)
    big = float(ROUTER_LANES)

    def first_argmax(vals):
        top = jnp.max(vals, axis=1, keepdims=True)
        return jnp.min(jnp.where(vals == top, lane, big), axis=1, keepdims=True)

    g_sel = first_argmax(jnp.where(lane < N_GROUPS, logits, NEG))
    first = N_GROUPS + g_sel * EXPERTS_PER_GROUP
    in_group = (lane >= first) & (lane < first + EXPERTS_PER_GROUP)
    el = jnp.where(in_group, logits, NEG)
    i1 = first_argmax(el)
    i2 = first_argmax(jnp.where(lane == i1, NEG, el))
    a = jnp.minimum(i1, i2) - first
    b = jnp.maximum(i1, i2) - first
    pair = a * (2 * EXPERTS_PER_GROUP - 1 - a) * 0.5 + (b - a - 1.0)
    return g_sel * N_PAIRS + pair


def _tail_kernel(x_ref, ya_ref, yb_ref, gmix_ref, wg_ref, bg_ref, wod_ref, wos_ref, wout_ref,
                 gffn_ref, wr_ref, br_ref, low_ref, x2_ref, route_ref, counts_ref, run_ref):
    @pl.when(pl.program_id(0) == 0)
    def _():
        run_ref[...] = jnp.zeros(run_ref.shape, F32)

    x = x_ref[...]
    d = x.shape[1]
    h = _rms(x, gmix_ref[...]).astype(BF16)
    gates = _sigmoid(_dot(h, wg_ref[...]) + bg_ref[...])
    mix = gates[:, :d] * _dot(ya_ref[...], wod_ref[...]) + gates[:, d:] * _dot(yb_ref[...], wos_ref[...])
    x2 = x + _dot(mix.astype(BF16), wout_ref[...])
    x2_ref[...] = x2
    h2 = _rms(x2, gffn_ref[...]).astype(BF16)
    logits = _dot(h2, wr_ref[...]) + br_ref[...]
    cls = _route_class(logits)
    lane = lax.broadcasted_iota(jnp.int32, logits.shape, 1).astype(F32)
    onehot = jnp.where(lane == cls, 1.0, 0.0)
    earlier = _dot(low_ref[...], onehot.astype(BF16)) + run_ref[...]
    rank = jnp.sum(earlier * onehot, axis=1, keepdims=True)
    route = jnp.where(lane == 0.0, cls, jnp.where(lane == 1.0, rank, 0.0))
    route_ref[...] = jnp.transpose(route)[:route_ref.shape[0], :]
    run = run_ref[...] + jnp.sum(onehot, axis=0, keepdims=True)
    run_ref[...] = run
    counts_ref[...] = run


def _tail(x2d, ya, yb, g_mix, wg, bg, wod, wos, wout, g_ffn, wr, br, tm):
    n, d = x2d.shape
    row = lambda i: (i, 0)
    const = lambda i: (0, 0)
    full = lambda a: pl.BlockSpec(a.shape, const)
    r = np.arange(tm)
    low = jnp.asarray((r[None, :] < r[:, None]).astype(np.float32), dtype=BF16)
    return pl.pallas_call(
        _tail_kernel,
        grid=(n // tm,),
        in_specs=[pl.BlockSpec((tm, d), row),
                  pl.BlockSpec((tm, COLS), row),
                  pl.BlockSpec((tm, COLS), row),
                  full(g_mix), full(wg), full(bg), full(wod), full(wos), full(wout),
                  full(g_ffn), full(wr), full(br), full(low)],
        out_specs=[pl.BlockSpec((tm, d), row),
                   pl.BlockSpec((8, tm), lambda i: (0, i)),
                   pl.BlockSpec((1, ROUTER_LANES), const)],
        out_shape=[jax.ShapeDtypeStruct((n, d), F32),
                   jax.ShapeDtypeStruct((8, n), F32),
                   jax.ShapeDtypeStruct((1, ROUTER_LANES), F32)],
        scratch_shapes=[pltpu.VMEM((1, ROUTER_LANES), F32)],
        compiler_params=pltpu.CompilerParams(
            dimension_semantics=("arbitrary",), vmem_limit_bytes=VMEM_LIMIT),
        name="tail",
    )(x2d, ya, yb, g_mix, wg, bg, wod, wos, wout, g_ffn, wr, br, low)


def _class_experts_np():
    ea, eb = [], []
    for g in range(N_GROUPS):
        for a in range(EXPERTS_PER_GROUP):
            for b in range(a + 1, EXPERTS_PER_GROUP):
                ea.append(g * EXPERTS_PER_GROUP + a)
                eb.append(g * EXPERTS_PER_GROUP + b)
    return np.asarray(ea, np.int32), np.asarray(eb, np.int32)


def _dispatch_plan(route, counts, tm):
    n = route.shape[1]
    cls = route[0].astype(jnp.int32)
    rank = route[1].astype(jnp.int32)
    cnt = counts[0, :N_CLASSES].astype(jnp.int32)
    tiles = (cnt + tm - 1) // tm
    tile_end = jnp.cumsum(tiles)
    pos = (tile_end - tiles)[cls] * tm + rank
    n_tiles = n // tm + N_CLASSES
    t_idx = jnp.arange(n_tiles, dtype=jnp.int32)
    used = t_idx < tile_end[-1]
    tile_cls = jnp.sum((t_idx[:, None] >= tile_end[None, :]).astype(jnp.int32), axis=1)
    tile_cls = jnp.where(used, tile_cls, jnp.max(jnp.where(used, tile_cls, 0)))
    ea_np, eb_np = _class_experts_np()
    return pos, jnp.asarray(ea_np)[tile_cls], jnp.asarray(eb_np)[tile_cls], used.astype(jnp.int32)


def _row_copies(n_rows, start_one, wait_one):
    def issue(r, carry):
        start_one(r)
        return carry

    def drain(r, carry):
        wait_one()
        return carry

    lax.fori_loop(0, n_rows, issue, 0, unroll=8)
    lax.fori_loop(0, n_rows, drain, 0, unroll=8)


def _row(ref, r):
    return ref.at[pl.ds(r, 1), :]


def _dispatch_kernel(pos_ref, x_ref, init_hbm, xs_hbm, sem):
    del init_hbm
    _row_copies(
        x_ref.shape[0],
        lambda r: pltpu.make_async_copy(_row(x_ref, r), _row(xs_hbm, pos_ref[0, r]), sem).start(),
        lambda: pltpu.make_async_copy(_row(x_ref, 0), _row(xs_hbm, 0), sem).wait())


def _dispatch(pos, x2, n_slots, tm):
    n, d = x2.shape
    return pl.pallas_call(
        _dispatch_kernel,
        grid=(n // tm,),
        in_specs=[pl.BlockSpec((None, 1, tm), lambda i: (i, 0, 0), memory_space=pltpu.SMEM),
                  pl.BlockSpec((tm, d), lambda i: (i, 0)),
                  pl.BlockSpec(memory_space=pl.ANY)],
        out_specs=pl.BlockSpec(memory_space=pl.ANY),
        out_shape=jax.ShapeDtypeStruct((n_slots, d), F32),
        scratch_shapes=[pltpu.SemaphoreType.DMA],
        input_output_aliases={2: 0},
        compiler_params=pltpu.CompilerParams(dimension_semantics=("arbitrary",)),
        name="moe_dispatch",
    )(pos.reshape(n // tm, 1, tm), x2, jnp.zeros((n_slots, d), F32))


def _combine_kernel(pos_ref, ys_hbm, y_ref, sem):
    _row_copies(
        y_ref.shape[0],
        lambda r: pltpu.make_async_copy(_row(ys_hbm, pos_ref[0, r]), _row(y_ref, r), sem).start(),
        lambda: pltpu.make_async_copy(_row(ys_hbm, 0), _row(y_ref, 0), sem).wait())


def _combine(pos, ys, n, tm):
    d = ys.shape[1]
    return pl.pallas_call(
        _combine_kernel,
        grid=(n // tm,),
        in_specs=[pl.BlockSpec((None, 1, tm), lambda i: (i, 0, 0), memory_space=pltpu.SMEM),
                  pl.BlockSpec(memory_space=pl.ANY)],
        out_specs=pl.BlockSpec((tm, d), lambda i: (i, 0)),
        out_shape=jax.ShapeDtypeStruct((n, d), F32),
        scratch_shapes=[pltpu.SemaphoreType.DMA],
        compiler_params=pltpu.CompilerParams(dimension_semantics=("arbitrary",)),
        name="moe_combine",
    )(pos.reshape(n // tm, 1, tm), ys)


def _moe_kernel(ea_ref, eb_ref, used_ref, xs_ref, g_ref, wr_ref, br_ref,
                mga_ref, mua_ref, mda_ref, mgb_ref, mub_ref, mdb_ref, ys_ref):
    t = pl.program_id(0)

    @pl.when(used_ref[t] == 0)
    def _():
        ys_ref[...] = jnp.zeros(ys_ref.shape, F32)

    @pl.when(used_ref[t] != 0)
    def _():
        x = xs_ref[...]
        h2 = _rms(x, g_ref[...]).astype(BF16)
        logits = _dot(h2, wr_ref[...]) + br_ref[...]
        lane = lax.broadcasted_iota(jnp.int32, logits.shape, 1)
        pick = lambda e: jnp.sum(jnp.where(lane == N_GROUPS + e, logits, 0.0), axis=1, keepdims=True)
        la = pick(ea_ref[t])
        lb = pick(eb_ref[t])
        top = jnp.maximum(la, lb)
        pa = jnp.exp(la - top)
        pb = jnp.exp(lb - top)

        def expert(mg_ref, mu_ref, md_ref):
            g = _dot(h2, mg_ref[...])
            act = g * _sigmoid(g) * _dot(h2, mu_ref[...])
            return _dot(act.astype(BF16), md_ref[...])

        ya = expert(mga_ref, mua_ref, mda_ref)
        yb = expert(mgb_ref, mub_ref, mdb_ref)
        ys_ref[...] = x + ((pa / (pa + pb)) * ya + (pb / (pa + pb)) * yb)


def _moe(ea, eb, used, xs, g_ffn, wr, br, mg, mu, md, tm):
    n_slots, d = xs.shape
    de = mg.shape[2]
    const = lambda t, ea, eb, used: (0, 0)
    w_a = lambda shape: pl.BlockSpec((None,) + shape, lambda t, ea, eb, used: (ea[t], 0, 0))
    w_b = lambda shape: pl.BlockSpec((None,) + shape, lambda t, ea, eb, used: (eb[t], 0, 0))
    grid_spec = pltpu.PrefetchScalarGridSpec(
        num_scalar_prefetch=3,
        grid=(n_slots // tm,),
        in_specs=[pl.BlockSpec((tm, d), lambda t, ea, eb, used: (t, 0)),
                  pl.BlockSpec(g_ffn.shape, const),
                  pl.BlockSpec(wr.shape, const),
                  pl.BlockSpec(br.shape, const),
                  w_a((d, de)), w_a((d, de)), w_a((de, d)),
                  w_b((d, de)), w_b((d, de)), w_b((de, d))],
        out_specs=pl.BlockSpec((tm, d), lambda t, ea, eb, used: (t, 0)))
    return pl.pallas_call(
        _moe_kernel,
        grid_spec=grid_spec,
        out_shape=jax.ShapeDtypeStruct((n_slots, d), F32),
        compiler_params=pltpu.CompilerParams(
            dimension_semantics=("arbitrary",), vmem_limit_bytes=VMEM_LIMIT),
        name="moe",
    )(ea, eb, used, xs, g_ffn, wr, br, mg, mu, md, mg, mu, md)


def _tile(n, pref):
    return pref if n % pref == 0 else n


def _feature_major(cache):
    b, p = cache.shape[:2]
    nd = cache.ndim
    return jnp.transpose(cache, (0,) + tuple(range(2, nd)) + (1,)).reshape(b, COLS, p)


def _token_major(xt, head_dims):
    b, _, t = xt.shape
    nd = len(head_dims)
    return jnp.transpose(xt.reshape((b,) + head_dims + (t,)), (0, nd + 1) + tuple(range(1, nd + 1)))


def kernel(x_prompt, x_sample, cache_diff_k, cache_diff_v, cache_sb_k, cache_sb_v, rel_bias_table, norm_mix_g, w_in, q_norm_g, k_norm_g, lambda_q1, lambda_k1, lambda_q2, lambda_k2, subln_g, w_o_diff, w_o_sb, w_branch_gate, b_branch_gate, w_out, norm_ffn_g, w_router_group, b_router_group, w_router_expert, b_router_expert, moe_w_gate, moe_w_up, moe_w_down):
    b, t, d = x_prompt.shape
    bs, ts, _ = x_sample.shape
    depth = norm_mix_g.shape[0]
    p = cache_diff_k.shape[2]
    tq = _tile(t, 512)
    tq_b = _tile(t, 256)
    tk_s = _tile(p, 256)
    tn = 128
    assert t % tq == 0 and t % tq_b == 0 and tq % CHUNK == 0 and p % CHUNK == 0
    assert ts <= CHUNK and ts <= tn
    assert p % tk_s == 0

    q_loc = np.arange(tq)
    bkt_prompt = np.concatenate([_bucket_tile_np(q_loc + tq, np.arange(tq) + tq),
                                 _bucket_tile_np(q_loc + tq, np.arange(tq))], axis=0)
    far = np.unique(_t5_bucket_np(-np.arange(tq + 1, 4 * tq)))
    assert far.size == 1
    far_bucket = int(far[0])
    q_s = p + np.arange(ts)
    k_s = np.concatenate([np.arange(p), p + np.arange(tn)])
    bkt_sample = _bucket_tile_np(q_s, k_s)
    bkt_sample[:, p + ts:] = -1
    bias_prompt = _bias_tiles(rel_bias_table, jnp.asarray(bkt_prompt)).reshape(H_A, 2, tq, tq)
    bias_sample = _bias_tiles(rel_bias_table, jnp.asarray(bkt_sample))
    bias_sc, bias_sn = bias_sample[:, :, :p], bias_sample[:, :, p:]

    gm = np.kron(np.eye(256 // DH), np.full((DH, DH), 1.0 / DH)).astype(np.float32)
    gm = jnp.asarray(gm, dtype=BF16)
    u_p = _upper_sum_matrix(tq_b)
    u_s = _upper_sum_matrix(tk_s)
    u_n = _upper_sum_matrix(tn)

    y_p = x_prompt
    y_s = x_sample.reshape(bs * ts, d)
    tm_proj = _tile(t, 512)
    tm_p = _tile(b * t, 512)
    tm_s = _tile(bs * ts, 128)
    tm_moe_p = _tile(b * t, 512)
    outs = [[] for _ in range(8)]
    for l in range(depth):
        lam_init = 0.8 - 0.6 * math.exp(-0.3 * l)
        w_in_b = w_in[l].astype(BF16)
        col = lambda g: w_in_b[:, g * COLS:(g + 1) * COLS]
        wn = jnp.concatenate([col(0), col(2), col(3)], axis=1)
        wt = jnp.concatenate([col(1), col(4), col(5)], axis=1).T
        g_mix = norm_mix_g[l].reshape(1, d)
        gq = jnp.tile(q_norm_g[l], COLS // DH).reshape(1, COLS)
        gk = jnp.tile(k_norm_g[l], COLS // DH).reshape(1, COLS)
        gkt = jnp.broadcast_to(gk.reshape(COLS, 1), (COLS, SLAB))
        lam_p = jnp.stack([lambda_q1[l], lambda_k1[l], lambda_q2[l], lambda_k2[l]])
        sg = subln_g[l].reshape(1, SLAB)
        wg = w_branch_gate[l].astype(BF16)
        bg = b_branch_gate[l].reshape(1, 2 * d)
        wod = w_o_diff[l].astype(BF16)
        wos = w_o_sb[l].astype(BF16)
        wout = w_out[l].astype(BF16)
        g_ffn = norm_ffn_g[l].reshape(1, d)
        n_r = N_GROUPS + N_EXPERTS
        wr = jnp.pad(jnp.concatenate([w_router_group[l], w_router_expert[l]], axis=1),
                     ((0, 0), (0, ROUTER_LANES - n_r)))
        wr = wr.astype(BF16)
        br = jnp.pad(jnp.concatenate([b_router_group[l], b_router_expert[l]]),
                     (0, ROUTER_LANES - n_r)).reshape(1, ROUTER_LANES)
        mg = moe_w_gate[l].astype(BF16)
        mu = moe_w_up[l].astype(BF16)
        md = moe_w_down[l].astype(BF16)

        def tail_and_moe(x2d, ya, yb, tm, tm_moe):
            n = x2d.shape[0]
            x2, route, counts = _tail(x2d, ya, yb, g_mix, wg, bg, wod, wos, wout, g_ffn, wr, br, tm)
            pos, ea, eb, used = _dispatch_plan(route, counts, tm_moe)
            xs = _dispatch(pos, x2, (n // tm_moe + N_CLASSES) * tm_moe, tm_moe)
            ys = _moe(ea, eb, used, xs, g_ffn, wr, br, mg, mu, md, tm_moe)
            return _combine(pos, ys, n, tm_moe)

        kat, va, kbt, vbt, qa_b, kat_b, va_b, qb_b, kbt_b, vbt_b = _project_t(
            y_p, g_mix, wn, wt, gq, gkt, gm, tm_proj)
        ya = _diff_prompt(rel_bias_table, qa_b, kat_b, va_b, bias_prompt, lam_p, sg,
                          tq, far_bucket, lam_init)
        yb = _sb_prompt(qb_b, kbt_b, vbt_b, u_p, tq_b)
        y_p = tail_and_moe(y_p.reshape(b * t, d), ya.reshape(b * t, COLS), yb.reshape(b * t, COLS),
                           tm_p, tm_moe_p).reshape(b, t, d)
        outs[0].append(_token_major(kat, (H_A, 2, DH)))
        outs[1].append(va.reshape(b, t, H_A, 2 * DH))
        outs[2].append(_token_major(kbt, (H_B, DH)))
        outs[3].append(_token_major(vbt, (H_B, DH)))

        ka, va, kb, vb, qa_b, ka_b, va_b, qb_b, kb_b, vb_b = _project(y_s, g_mix, w_in_b, gq, gk, gm, tm_s)
        s3 = lambda a: a.reshape(bs, ts, COLS)
        padk = lambda a: jnp.pad(s3(a), ((0, 0), (0, tn - ts), (0, 0)))
        ya = _diff_sample(s3(qa_b), _feature_major(cache_diff_k[l]), cache_diff_v[l].reshape(bs, p, COLS),
                          padk(ka_b), padk(va_b), bias_sc, bias_sn, lam_p, sg, lam_init)
        yb = _sb_sample(s3(qb_b), _feature_major(cache_sb_k[l]), _feature_major(cache_sb_v[l]),
                        padk(kb_b), padk(vb_b), u_s, u_n, tk_s)
        y_s = tail_and_moe(y_s, ya.reshape(bs * ts, COLS), yb.reshape(bs * ts, COLS), tm_s, tm_s)
        outs[4].append(ka.reshape(bs, ts, H_A, 2, DH))
        outs[5].append(va.reshape(bs, ts, H_A, 2 * DH))
        outs[6].append(kb.reshape(bs, ts, H_B, DH))
        outs[7].append(vb.reshape(bs, ts, H_B, DH))

    return (y_p, y_s.reshape(bs, ts, d)) + tuple(jnp.stack(o) for o in outs)
```

```python
import functools
import math

import numpy as np
import jax
import jax.numpy as jnp
from jax import lax
from jax.experimental import pallas as pl
from jax.experimental.pallas import tpu as pltpu

F32 = jnp.float32
BF16 = jnp.bfloat16

EPS = 1e-6
CHUNK = 64
H_A = 4
DH = 64
H_B = 8
SLAB = 2 * DH
N_SLAB_A = H_A
N_SLAB_B = H_B // 2
COLS = H_A * SLAB
N_BUCKETS = 32
MAX_DISTANCE = 128
N_GROUPS = 4
EXPERTS_PER_GROUP = 4
N_EXPERTS = N_GROUPS * EXPERTS_PER_GROUP
N_PAIRS = EXPERTS_PER_GROUP * (EXPERTS_PER_GROUP - 1) // 2
N_CLASSES = N_GROUPS * N_PAIRS
ROUTER_LANES = 128
NEG = -1e30
SB_DEAD = -104.0
VMEM_LIMIT = 56 * 1024 * 1024
LOG2E = math.log2(math.e)
FAR_UNROLL = 4


def _rms(x, g):
    return x * lax.rsqrt(jnp.mean(x * x, axis=-1, keepdims=True) + EPS) * g


def _sigmoid(x):
    return 1.0 / (1.0 + jnp.exp(-x))


def _dot(a, b):
    return jnp.dot(a, b, preferred_element_type=F32)


def _dot_nt(a, b):
    return lax.dot_general(a, b, (((1,), (1,)), ((), ())), preferred_element_type=F32)


def _lane_tiles(x, n):
    return x if n == 1 else jnp.concatenate([x] * n, axis=1)


def _split_bf16(x):
    hi = x.astype(BF16)
    lo = (x - hi.astype(F32)).astype(BF16)
    return hi, lo


def _cols(j, width):
    return pl.ds(pl.multiple_of(j * width, width), width)


def _sub_head_norm(t, gain, gm):
    outs = []
    for s in range(0, COLS, 256):
        ts = t[:, s:s + 256]
        hi, lo = _split_bf16(ts * ts)
        outs.append(ts * lax.rsqrt(_dot(hi, gm) + _dot(lo, gm) + EPS))
    return jnp.concatenate(outs, axis=1) * gain


def _sub_head_norm_t(t, gain, gm):
    outs = []
    for s in range(0, COLS, 256):
        ts = t[s:s + 256, :]
        hi, lo = _split_bf16(ts * ts)
        outs.append(ts * lax.rsqrt(_dot(gm, hi) + _dot(gm, lo) + EPS))
    return jnp.concatenate(outs, axis=0) * gain


def _proj_kernel(x_ref, g_ref, w_ref, gq_ref, gk_ref, gm_ref,
                 ka_o, va_o, kb_o, vb_o, qa_b, ka_b, va_b, qb_b, kb_b, vb_b):
    h = _rms(x_ref[...], g_ref[...]).astype(BF16)
    proj = _dot(h, w_ref[...])
    gm = gm_ref[...]
    qa = _sub_head_norm(proj[:, 0 * COLS:1 * COLS], gq_ref[...], gm)
    ka = _sub_head_norm(proj[:, 1 * COLS:2 * COLS], gk_ref[...], gm)
    va = proj[:, 2 * COLS:3 * COLS]
    qb = proj[:, 3 * COLS:4 * COLS]
    kb = proj[:, 4 * COLS:5 * COLS]
    vb = proj[:, 5 * COLS:6 * COLS]
    ka_o[...] = ka
    va_o[...] = va
    kb_o[...] = kb
    vb_o[...] = vb
    scale = 1.0 / math.sqrt(DH)
    qa_b[...] = (qa * (scale * LOG2E)).astype(BF16)
    ka_b[...] = ka.astype(BF16)
    va_b[...] = va.astype(BF16)
    qb_b[...] = (qb * scale).astype(BF16)
    kb_b[...] = kb.astype(BF16)
    vb_b[...] = vb.astype(BF16)


def _project(x2d, g_mix, w_in_b, gq, gk, gm, tm):
    n, d = x2d.shape
    row = lambda i: (i, 0)
    const = lambda i: (0, 0)
    out_f = jax.ShapeDtypeStruct((n, COLS), F32)
    out_b = jax.ShapeDtypeStruct((n, COLS), BF16)
    blk = pl.BlockSpec((tm, COLS), row)
    return pl.pallas_call(
        _proj_kernel,
        grid=(n // tm,),
        in_specs=[pl.BlockSpec((tm, d), row),
                  pl.BlockSpec((1, d), const),
                  pl.BlockSpec(w_in_b.shape, const),
                  pl.BlockSpec((1, COLS), const),
                  pl.BlockSpec((1, COLS), const),
                  pl.BlockSpec(gm.shape, const)],
        out_specs=[blk] * 10,
        out_shape=[out_f] * 4 + [out_b] * 6,
        compiler_params=pltpu.CompilerParams(
            dimension_semantics=("arbitrary",), vmem_limit_bytes=VMEM_LIMIT),
        name="proj",
    )(x2d, g_mix, w_in_b, gq, gk, gm)


def _proj_t_kernel(x_ref, g_ref, wn_ref, wt_ref, gq_ref, gkt_ref, gm_ref,
                   kat_o, va_o, kbt_o, vbt_o, qa_b, kat_b, va_b, qb_b, kbt_b, vbt_b):
    h = _rms(x_ref[...], g_ref[...]).astype(BF16)
    tm = h.shape[0]
    nat = _dot(h, wn_ref[...])
    tr = _dot_nt(wt_ref[...], h)
    gm = gm_ref[...]
    qa = _sub_head_norm(nat[:, 0 * COLS:1 * COLS], gq_ref[...], gm)
    va = nat[:, 1 * COLS:2 * COLS]
    qb = nat[:, 2 * COLS:3 * COLS]
    kat = _sub_head_norm_t(tr[0 * COLS:1 * COLS], _lane_tiles(gkt_ref[...], tm // SLAB), gm)
    kbt = tr[1 * COLS:2 * COLS]
    vbt = tr[2 * COLS:3 * COLS]
    kat_o[...] = kat
    va_o[...] = va
    kbt_o[...] = kbt
    vbt_o[...] = vbt
    scale = 1.0 / math.sqrt(DH)
    qa_b[...] = (qa * (scale * LOG2E)).astype(BF16)
    kat_b[...] = kat.astype(BF16)
    va_b[...] = va.astype(BF16)
    qb_b[...] = (qb * scale).astype(BF16)
    kbt_b[...] = kbt.astype(BF16)
    vbt_b[...] = vbt.astype(BF16)


def _project_t(x, g_mix, wn, wt, gq, gkt, gm, tm):
    b, t, d = x.shape
    const = lambda b, i: (0, 0)
    nat = pl.BlockSpec((None, tm, COLS), lambda b, i: (b, i, 0))
    fm = pl.BlockSpec((None, COLS, tm), lambda b, i: (b, 0, i))
    nat_s = lambda dt: jax.ShapeDtypeStruct((b, t, COLS), dt)
    fm_s = lambda dt: jax.ShapeDtypeStruct((b, COLS, t), dt)
    return pl.pallas_call(
        _proj_t_kernel,
        grid=(b, t // tm),
        in_specs=[pl.BlockSpec((None, tm, d), lambda b, i: (b, i, 0)),
                  pl.BlockSpec((1, d), const),
                  pl.BlockSpec(wn.shape, const),
                  pl.BlockSpec(wt.shape, const),
                  pl.BlockSpec((1, COLS), const),
                  pl.BlockSpec(gkt.shape, const),
                  pl.BlockSpec(gm.shape, const)],
        out_specs=[fm, nat, fm, fm, nat, fm, nat, nat, fm, fm],
        out_shape=[fm_s(F32), nat_s(F32), fm_s(F32), fm_s(F32),
                   nat_s(BF16), fm_s(BF16), nat_s(BF16), nat_s(BF16), fm_s(BF16), fm_s(BF16)],
        compiler_params=pltpu.CompilerParams(
            dimension_semantics=("arbitrary", "arbitrary"), vmem_limit_bytes=VMEM_LIMIT),
        name="proj_t",
    )(x, g_mix, wn, wt, gq, gkt, gm)


def _t5_bucket_np(rel):
    half = N_BUCKETS // 2
    max_exact = half // 2
    base = np.where(rel > 0, half, 0)
    n = np.abs(rel)
    nf = np.maximum(n, 1).astype(np.float64)
    large = max_exact + (np.log(nf / max_exact) / math.log(MAX_DISTANCE / max_exact)
                         * (half - max_exact)).astype(np.int32)
    large = np.minimum(large, half - 1)
    return (base + np.where(n < max_exact, n, large)).astype(np.int32)


def _bucket_tile_np(q_pos, k_pos):
    rel = k_pos[None, :] - q_pos[:, None]
    allowed = (k_pos[None, :] // CHUNK) <= (q_pos[:, None] // CHUNK)
    return np.where(allowed, _t5_bucket_np(rel), -1).astype(np.int32)


def _bias_kernel(tbl_ref, bkt_ref, o_ref):
    h = pl.program_id(0)
    bkt = bkt_ref[...]
    out = jnp.full(bkt.shape, NEG, F32)
    for b in range(N_BUCKETS):
        out = jnp.where(bkt == b, tbl_ref[b, h] * LOG2E, out)
    o_ref[...] = out


def _bias_tiles(rel_table, buckets):
    r, c = buckets.shape
    return pl.pallas_call(
        _bias_kernel,
        grid=(H_A,),
        in_specs=[pl.BlockSpec(memory_space=pltpu.SMEM),
                  pl.BlockSpec((r, c), lambda h: (0, 0))],
        out_specs=pl.BlockSpec((None, r, c), lambda h: (h, 0, 0)),
        out_shape=jax.ShapeDtypeStruct((H_A, r, c), F32),
        name="bias_tiles",
    )(rel_table, buckets)


def _two_maps(q):
    lane = lax.broadcasted_iota(jnp.int32, q.shape, 1)
    zero = jnp.zeros_like(q)
    return jnp.concatenate([jnp.where(lane < DH, q, zero), jnp.where(lane >= DH, q, zero)], axis=0)


def _diff_lambda(lam_ref, lam_init):
    p = lam_ref[...]
    s1 = jnp.sum(p[0:1] * p[1:2], axis=1, keepdims=True)
    s2 = jnp.sum(p[2:3] * p[3:4], axis=1, keepdims=True)
    return jnp.exp(s1) - jnp.exp(s2) + lam_init


def _diff_finish(acc, l, t, lam, sg, lam_init):
    o = acc[:t] / l[:t] - lam * (acc[t:] / l[t:])
    o = o * lax.rsqrt(jnp.mean(o * o, axis=-1, keepdims=True) + EPS) * sg
    return o * (1.0 - lam_init)


def _diff_prompt_kernel(tbl_ref, q_ref, kt_ref, v_ref, bias_ref, lam_ref, sg_ref, o_ref,
                        vx_ref, m_ref, l_ref, acc_ref, *, tq, tk, far_bucket, lam_init):
    h = pl.program_id(1)
    i = pl.program_id(2)

    @pl.when(i == 0)
    def _():
        vx_ref[:, :SLAB] = v_ref[...]
        vx_ref[:, SLAB:] = jnp.ones((vx_ref.shape[0], SLAB), vx_ref.dtype)

    qq = _two_maps(q_ref[...])
    m_ref[...] = jnp.full(m_ref.shape, NEG, F32)
    l_ref[...] = jnp.zeros(l_ref.shape, F32)
    acc_ref[...] = jnp.zeros(acc_ref.shape, F32)
    n_sub = tq // tk

    def update(jk, bias):
        s = _dot(qq, kt_ref[:, _cols(jk, tk)])
        if bias is not None:
            s = s + jnp.concatenate([bias, bias], axis=0)
        m_old = m_ref[...]
        m_new = jnp.maximum(m_old, jnp.max(s, axis=1, keepdims=True))
        alpha = jnp.exp2(m_old - m_new)
        p = jnp.exp2(s - _lane_tiles(m_new, tk // SLAB))
        pv = _dot(p.astype(BF16), vx_ref[_cols(jk, tk), :])
        l_ref[...] = alpha * l_ref[...] + pv[:, SLAB:]
        acc_ref[...] = alpha * acc_ref[...] + pv[:, :SLAB]
        m_ref[...] = m_new

    n_far = jnp.maximum(i - 1, 0)

    def far_tiles(first, count):
        for sub in range(count * n_sub):
            update(first * n_sub + sub, None)

    def far_body(j, carry):
        far_tiles(j * FAR_UNROLL, FAR_UNROLL)
        return carry

    lax.fori_loop(0, n_far // FAR_UNROLL, far_body, 0)
    done = n_far - n_far % FAR_UNROLL
    step = FAR_UNROLL // 2
    while step >= 1:
        take = (n_far % (2 * step)) >= step

        @pl.when(take)
        def _(done=done, step=step):
            far_tiles(done, step)

        done = done + jnp.where(take, step, 0)
        step //= 2
    m_ref[...] = m_ref[...] + tbl_ref[far_bucket, h] * LOG2E

    def biased_tile(j, kind):
        for sub in range(n_sub):
            update(j * n_sub + sub, bias_ref[kind, :, sub * tk:(sub + 1) * tk])

    @pl.when(i >= 1)
    def _():
        biased_tile(i - 1, 1)
        biased_tile(i, 0)

    @pl.when(i == 0)
    def _():
        biased_tile(i, 0)

    lam = _diff_lambda(lam_ref, lam_init)
    o = _diff_finish(acc_ref[...], l_ref[...], tq, lam, sg_ref[...], lam_init)
    o_ref[...] = o.astype(o_ref.dtype)


def _diff_prompt(rel_table, qa, kat, va, bias, lam_p, sg, tq, far_bucket, lam_init):
    b, t, _ = qa.shape
    kern = functools.partial(_diff_prompt_kernel, tq=tq, tk=_tile(tq, 256), far_bucket=far_bucket,
                             lam_init=lam_init)
    return pl.pallas_call(
        kern,
        grid=(b, H_A, t // tq),
        in_specs=[pl.BlockSpec(memory_space=pltpu.SMEM),
                  pl.BlockSpec((None, tq, SLAB), lambda b, h, i: (b, i, h)),
                  pl.BlockSpec((None, SLAB, t), lambda b, h, i: (b, h, 0)),
                  pl.BlockSpec((None, t, SLAB), lambda b, h, i: (b, 0, h)),
                  pl.BlockSpec((None, 2, tq, tq), lambda b, h, i: (h, 0, 0, 0)),
                  pl.BlockSpec(lam_p.shape, lambda b, h, i: (0, 0)),
                  pl.BlockSpec((1, SLAB), lambda b, h, i: (0, 0))],
        out_specs=pl.BlockSpec((None, tq, SLAB), lambda b, h, i: (b, i, h)),
        out_shape=jax.ShapeDtypeStruct((b, t, COLS), BF16),
        scratch_shapes=[pltpu.VMEM((t, 2 * SLAB), BF16),
                        pltpu.VMEM((2 * tq, SLAB), F32),
                        pltpu.VMEM((2 * tq, SLAB), F32),
                        pltpu.VMEM((2 * tq, SLAB), F32)],
        compiler_params=pltpu.CompilerParams(
            dimension_semantics=("arbitrary", "arbitrary", "arbitrary"),
            vmem_limit_bytes=VMEM_LIMIT),
        name="diff_prompt",
    )(rel_table, qa, kat, va, bias, lam_p, sg)


def _diff_sample_kernel(q_ref, kct_ref, vc_ref, kn_ref, vn_ref, bc_ref, bn_ref, lam_ref, sg_ref,
                        o_ref, *, lam_init):
    qq = _two_maps(q_ref[...])
    t = q_ref.shape[0]
    bc = bc_ref[...]
    bn = bn_ref[...]
    s_c = _dot(qq, kct_ref[...].astype(BF16)) + jnp.concatenate([bc, bc], axis=0)
    s_n = _dot_nt(qq, kn_ref[...]) + jnp.concatenate([bn, bn], axis=0)
    m = jnp.maximum(jnp.max(s_c, axis=1, keepdims=True), jnp.max(s_n, axis=1, keepdims=True))
    p_c = jnp.exp2(s_c - m)
    p_n = jnp.exp2(s_n - m)
    l = jnp.sum(p_c, axis=1, keepdims=True) + jnp.sum(p_n, axis=1, keepdims=True)
    acc = _dot(p_c.astype(BF16), vc_ref[...].astype(BF16)) + _dot(p_n.astype(BF16), vn_ref[...])
    lam = _diff_lambda(lam_ref, lam_init)
    o = _diff_finish(acc, l, t, lam, sg_ref[...], lam_init)
    o_ref[...] = o.astype(o_ref.dtype)


def _diff_sample(qa, kct, vc, kn, vn, bias_c, bias_n, lam_p, sg, lam_init):
    b, t, _ = qa.shape
    p = vc.shape[1]
    tn = kn.shape[1]
    slab = lambda rows: pl.BlockSpec((None, rows, SLAB), lambda b, h: (b, 0, h))
    return pl.pallas_call(
        functools.partial(_diff_sample_kernel, lam_init=lam_init),
        grid=(b, H_A),
        in_specs=[slab(t),
                  pl.BlockSpec((None, SLAB, p), lambda b, h: (b, h, 0)),
                  slab(p), slab(tn), slab(tn),
                  pl.BlockSpec((None, t, p), lambda b, h: (h, 0, 0)),
                  pl.BlockSpec((None, t, tn), lambda b, h: (h, 0, 0)),
                  pl.BlockSpec(lam_p.shape, lambda b, h: (0, 0)),
                  pl.BlockSpec((1, SLAB), lambda b, h: (0, 0))],
        out_specs=slab(t),
        out_shape=jax.ShapeDtypeStruct((b, t, COLS), BF16),
        compiler_params=pltpu.CompilerParams(
            dimension_semantics=("arbitrary", "arbitrary"), vmem_limit_bytes=VMEM_LIMIT),
        name="diff_sample",
    )(qa, kct, vc, kn, vn, bias_c, bias_n, lam_p, sg)


def _sb_weights(z, u, c, allowed):
    sp = jnp.log(1.0 + jnp.exp(-jnp.abs(z)))
    log_beta = jnp.minimum(z, 0.0) - sp
    log_keep = -jnp.maximum(z, 0.0) - sp
    if allowed is not None:
        log_keep = jnp.where(allowed, log_keep, 0.0)
    hi, lo = _split_bf16(log_keep)
    later = _dot(hi, u) + _dot(lo, u)
    a = jnp.exp(log_beta + later + _lane_tiles(c, z.shape[1] // SLAB))
    if allowed is not None:
        a = jnp.where(allowed, a, 0.0)
    return a.astype(BF16), c + jnp.sum(log_keep, axis=1, keepdims=True)


def _strictly_causal_two_heads(t, n_keys):
    row = lax.broadcasted_iota(jnp.int32, (2 * t, n_keys), 0)
    col = lax.broadcasted_iota(jnp.int32, (2 * t, n_keys), 1)
    return col < jnp.where(row >= t, row - t, row)


def _merge_heads(acc, t):
    lane = lax.broadcasted_iota(jnp.int32, (t, SLAB), 1)
    return jnp.where(lane < DH, acc[:t], acc[t:])


def _sb_sweep_left(j0, c, acc, tile):
    def alive(c):
        return (jnp.max(c) > SB_DEAD).astype(jnp.int32)

    def cond(st):
        j, live, _, _ = st
        return (j >= 0) & (live > 0)

    def body(st):
        j, _, c, acc = st
        c, acc = tile(j, c, acc)
        return j - 1, alive(c), c, acc

    _, _, _, acc = lax.while_loop(cond, body, (j0, alive(c), c, acc))
    return acc


def _sb_prompt_kernel(q_ref, kt_ref, vt_ref, u_ref, o_ref, *, tq):
    n_q = q_ref.shape[0] // tq
    u = u_ref[...]
    mask = _strictly_causal_two_heads(tq, tq)
    zeros = jnp.zeros((2 * tq, SLAB), F32)

    def make_tile(qq):
        def tile(j, c, acc, mask=None, valid=None):
            a, c_new = _sb_weights(_dot(qq, kt_ref[:, _cols(j, tq)]), u, c, mask)
            if valid is not None:
                a = jnp.where(valid, a, jnp.zeros_like(a))
                c_new = jnp.where(valid, c_new, c)
            return c_new, acc + _dot_nt(a, vt_ref[:, _cols(j, tq)])
        return tile

    state = []
    for s in range(n_q):
        i = pl.program_id(2) * n_q + s
        tile = make_tile(_two_maps(q_ref[s * tq:(s + 1) * tq, :]))
        c, acc = tile(i, zeros, zeros, mask=mask)
        c, acc = tile(jnp.maximum(i - 1, 0), c, acc, valid=i >= 1)
        state.append((i, tile, c, acc))
    for s, (i, tile, c, acc) in enumerate(state):
        acc = _sb_sweep_left(i - 2, c, acc, tile)
        o_ref[s * tq:(s + 1) * tq, :] = _merge_heads(acc, tq).astype(o_ref.dtype)


def _sb_prompt(qb, kbt, vbt, u, tq):
    b, t, _ = qb.shape
    tg = _tile(t, 4 * tq)
    return pl.pallas_call(
        functools.partial(_sb_prompt_kernel, tq=tq),
        grid=(b, N_SLAB_B, t // tg),
        in_specs=[pl.BlockSpec((None, tg, SLAB), lambda b, h, i: (b, i, h)),
                  pl.BlockSpec((None, SLAB, t), lambda b, h, i: (b, h, 0)),
                  pl.BlockSpec((None, SLAB, t), lambda b, h, i: (b, h, 0)),
                  pl.BlockSpec(u.shape, lambda b, h, i: (0, 0))],
        out_specs=pl.BlockSpec((None, tg, SLAB), lambda b, h, i: (b, i, h)),
        out_shape=jax.ShapeDtypeStruct((b, t, COLS), BF16),
        compiler_params=pltpu.CompilerParams(
            dimension_semantics=("arbitrary", "arbitrary", "arbitrary"),
            vmem_limit_bytes=VMEM_LIMIT),
        name="sb_prompt",
    )(qb, kbt, vbt, u)


def _sb_sample_kernel(q_ref, kct_ref, vct_ref, kn_ref, vn_ref, u_ref, un_ref, o_ref, *, tk):
    t = q_ref.shape[0]
    tn = kn_ref.shape[0]
    n_tiles = kct_ref.shape[1] // tk
    qq = _two_maps(q_ref[...])
    u = u_ref[...]
    zeros = jnp.zeros((2 * t, SLAB), F32)
    a, c = _sb_weights(_dot_nt(qq, kn_ref[...]), un_ref[...], zeros,
                       _strictly_causal_two_heads(t, tn))
    acc = _dot(a, vn_ref[...])

    def tile(j, c, acc):
        a, c = _sb_weights(_dot(qq, kct_ref[:, _cols(j, tk)].astype(BF16)), u, c, None)
        return c, acc + _dot_nt(a, vct_ref[:, _cols(j, tk)].astype(BF16))

    acc = _sb_sweep_left(n_tiles - 1, c, acc, tile)
    o_ref[...] = _merge_heads(acc, t).astype(o_ref.dtype)


def _sb_sample(qb, kct, vct, kn, vn, u, un, tk):
    b, t, _ = qb.shape
    p = kct.shape[2]
    tn = kn.shape[1]
    slab = lambda rows: pl.BlockSpec((None, rows, SLAB), lambda b, h: (b, 0, h))
    slab_t = pl.BlockSpec((None, SLAB, p), lambda b, h: (b, h, 0))
    return pl.pallas_call(
        functools.partial(_sb_sample_kernel, tk=tk),
        grid=(b, N_SLAB_B),
        in_specs=[slab(t), slab_t, slab_t, slab(tn), slab(tn),
                  pl.BlockSpec(u.shape, lambda b, h: (0, 0)),
                  pl.BlockSpec(un.shape, lambda b, h: (0, 0))],
        out_specs=slab(t),
        out_shape=jax.ShapeDtypeStruct((b, t, COLS), BF16),
        compiler_params=pltpu.CompilerParams(
            dimension_semantics=("arbitrary", "arbitrary"), vmem_limit_bytes=VMEM_LIMIT),
        name="sb_sample",
    )(qb, kct, vct, kn, vn, u, un)


def _upper_sum_matrix(n):
    j = np.arange(n)[:, None]
    s = np.arange(n)[None, :]
    return jnp.asarray((j > s).astype(np.float32), dtype=BF16)


def _route_class(logits):
    lane = lax.broadcasted_iota(jnp.int32, logits.shape, 1).astype(F32)
    big = float(ROUTER_LANES)

    def first_argmax(vals):
        top = jnp.max(vals, axis=1, keepdims=True)
        return jnp.min(jnp.where(vals == top, lane, big), axis=1, keepdims=True)

    g_sel = first_argmax(jnp.where(lane < N_GROUPS, logits, NEG))
    first = N_GROUPS + g_sel * EXPERTS_PER_GROUP
    in_group = (lane >= first) & (lane < first + EXPERTS_PER_GROUP)
    el = jnp.where(in_group, logits, NEG)
    i1 = first_argmax(el)
    i2 = first_argmax(jnp.where(lane == i1, NEG, el))
    a = jnp.minimum(i1, i2) - first
    b = jnp.maximum(i1, i2) - first
    pair = a * (2 * EXPERTS_PER_GROUP - 1 - a) * 0.5 + (b - a - 1.0)
    return g_sel * N_PAIRS + pair


def _tail_kernel(x_ref, ya_ref, yb_ref, gmix_ref, wg_ref, bg_ref, wod_ref, wos_ref, wout_ref,
                 gffn_ref, wr_ref, br_ref, low_ref, x2_ref, route_ref, counts_ref, run_ref):
    @pl.when(pl.program_id(0) == 0)
    def _():
        run_ref[...] = jnp.zeros(run_ref.shape, F32)

    x = x_ref[...]
    d = x.shape[1]
    h = _rms(x, gmix_ref[...]).astype(BF16)
    gates = _sigmoid(_dot(h, wg_ref[...]) + bg_ref[...])
    mix = gates[:, :d] * _dot(ya_ref[...], wod_ref[...]) + gates[:, d:] * _dot(yb_ref[...], wos_ref[...])
    x2 = x + _dot(mix.astype(BF16), wout_ref[...])
    x2_ref[...] = x2
    h2 = _rms(x2, gffn_ref[...]).astype(BF16)
    logits = _dot(h2, wr_ref[...]) + br_ref[...]
    cls = _route_class(logits)
    lane = lax.broadcasted_iota(jnp.int32, logits.shape, 1).astype(F32)
    onehot = jnp.where(lane == cls, 1.0, 0.0)
    earlier = _dot(low_ref[...], onehot.astype(BF16)) + run_ref[...]
    rank = jnp.sum(earlier * onehot, axis=1, keepdims=True)
    route = jnp.where(lane == 0.0, cls, jnp.where(lane == 1.0, rank, 0.0))
    route_ref[...] = jnp.transpose(route)[:route_ref.shape[0], :]
    run = run_ref[...] + jnp.sum(onehot, axis=0, keepdims=True)
    run_ref[...] = run
    counts_ref[...] = run


def _tail(x2d, ya, yb, g_mix, wg, bg, wod, wos, wout, g_ffn, wr, br, tm):
    n, d = x2d.shape
    row = lambda i: (i, 0)
    const = lambda i: (0, 0)
    full = lambda a: pl.BlockSpec(a.shape, const)
    r = np.arange(tm)
    low =jnp.asarray((r[None, :] < r[:, None]).astype(np.float32), dtype=BF16)
    return pl.pallas_call(
        _tail_kernel,
        grid=(n // tm,),
        in_specs=[pl.BlockSpec((tm, d), row),
                  pl.BlockSpec((tm, COLS), row),
                  pl.BlockSpec((tm, COLS), row),
                  full(g_mix), full(wg), full(bg), full(wod), full(wos), full(wout),
                  full(g_ffn), full(wr), full(br), full(low)],
        out_specs=[pl.BlockSpec((tm, d), row),
                   pl.BlockSpec((8, tm), lambda i: (0, i)),
                   pl.BlockSpec((1, ROUTER_LANES), const)],
        out_shape=[jax.ShapeDtypeStruct((n, d), F32),
                   jax.ShapeDtypeStruct((8, n), F32),
                   jax.ShapeDtypeStruct((1, ROUTER_LANES), F32)],
        scratch_shapes=[pltpu.VMEM((1, ROUTER_LANES), F32)],
        compiler_params=pltpu.CompilerParams(
            dimension_semantics=("arbitrary",), vmem_limit_bytes=VMEM_LIMIT),
        name="tail",
    )(x2d, ya, yb, g_mix, wg, bg, wod, wos, wout, g_ffn, wr, br, low)


def _class_experts_np():
    ea, eb = [], []
    for g in range(N_GROUPS):
        for a in range(EXPERTS_PER_GROUP):
            for b in range(a + 1, EXPERTS_PER_GROUP):
                ea.append(g * EXPERTS_PER_GROUP + a)
                eb.append(g * EXPERTS_PER_GROUP + b)
    return np.asarray(ea, np.int32), np.asarray(eb, np.int32)


def _dispatch_plan(route, counts, tm):
    n = route.shape[1]
    cls = route[0].astype(jnp.int32)
    rank = route[1].astype(jnp.int32)
    cnt = counts[0, :N_CLASSES].astype(jnp.int32)
    tiles = (cnt + tm - 1) // tm
    tile_end = jnp.cumsum(tiles)
    pos = (tile_end - tiles)[cls] * tm + rank
    n_tiles = n // tm + N_CLASSES
    t_idx = jnp.arange(n_tiles, dtype=jnp.int32)
    used = t_idx < tile_end[-1]
    tile_cls = jnp.sum((t_idx[:, None] >= tile_end[None, :]).astype(jnp.int32), axis=1)
    tile_cls = jnp.where(used, tile_cls, jnp.max(jnp.where(used, tile_cls, 0)))
    ea_np, eb_np = _class_experts_np()
    return pos, jnp.asarray(ea_np)[tile_cls], jnp.asarray(eb_np)[tile_cls], used.astype(jnp.int32)


SUBLANES = 8


def _row_copies(n_rows, start_one, wait_one):
    def issue(g, carry):
        for s in range(SUBLANES):
            start_one(g, s)
        return carry

    def drain(g, carry):
        for _ in range(SUBLANES):
            wait_one()
        return carry

    lax.fori_loop(0, n_rows // SUBLANES, issue, 0)
    lax.fori_loop(0, n_rows // SUBLANES, drain, 0)


def _row3(ref, g, s):
    return ref.at[g, pl.ds(s, 1), :]


def _dispatch_kernel(hi_ref, lo_ref, x_ref, init_hbm, xs_hbm, sem):
    del init_hbm

    def start(g, s):
        r = g * SUBLANES + s
        pltpu.make_async_copy(_row3(x_ref, g, s), _row3(xs_hbm, hi_ref[0, r], lo_ref[0, r]), sem).start()

    _row_copies(x_ref.shape[0] * SUBLANES, start,
                lambda: pltpu.make_async_copy(_row3(x_ref, 0, 0), _row3(xs_hbm, 0, 0), sem).wait())


def _split_pos(pos, tm):
    n = pos.shape[0]
    blocks = lambda a: a.reshape(n // tm, 1, tm)
    return blocks(pos // SUBLANES), blocks(pos % SUBLANES)


def _pos_spec(tm):
    return pl.BlockSpec((None, 1, tm), lambda i: (i, 0, 0), memory_space=pltpu.SMEM)


def _dispatch(pos, x2, n_slots, tm):
    n, d = x2.shape
    hi, lo = _split_pos(pos, tm)
    xs = pl.pallas_call(
        _dispatch_kernel,
        grid=(n // tm,),
        in_specs=[_pos_spec(tm), _pos_spec(tm),
                  pl.BlockSpec((tm // SUBLANES, SUBLANES, d), lambda i: (i, 0, 0)),
                  pl.BlockSpec(memory_space=pl.ANY)],
        out_specs=pl.BlockSpec(memory_space=pl.ANY),
        out_shape=jax.ShapeDtypeStruct((n_slots // SUBLANES, SUBLANES, d), F32),
        scratch_shapes=[pltpu.SemaphoreType.DMA],
        input_output_aliases={3: 0},
        compiler_params=pltpu.CompilerParams(dimension_semantics=("arbitrary",)),
        name="moe_dispatch",
    )(hi, lo, x2.reshape(n // SUBLANES, SUBLANES, d), jnp.zeros((n_slots // SUBLANES, SUBLANES, d), F32))
    return xs.reshape(n_slots, d)


def _combine_kernel(hi_ref, lo_ref, ys_hbm, y_ref, sem):
    def start(g, s):
        r = g * SUBLANES + s
        pltpu.make_async_copy(_row3(ys_hbm, hi_ref[0, r], lo_ref[0, r]), _row3(y_ref, g, s), sem).start()

    _row_copies(y_ref.shape[0] * SUBLANES, start,
                lambda: pltpu.make_async_copy(_row3(ys_hbm, 0, 0), _row3(y_ref, 0, 0), sem).wait())


def _combine(pos, ys, n, tm):
    n_slots, d = ys.shape
    hi, lo = _split_pos(pos, tm)
    return pl.pallas_call(
        _combine_kernel,
        grid=(n // tm,),
        in_specs=[_pos_spec(tm), _pos_spec(tm), pl.BlockSpec(memory_space=pl.ANY)],
        out_specs=pl.BlockSpec((tm // SUBLANES, SUBLANES, d), lambda i: (i, 0, 0)),
        out_shape=jax.ShapeDtypeStruct((n // SUBLANES, SUBLANES, d), F32),
        scratch_shapes=[pltpu.SemaphoreType.DMA],
        compiler_params=pltpu.CompilerParams(dimension_semantics=("arbitrary",)),
        name="moe_combine",
    )(hi, lo, ys.reshape(n_slots // SUBLANES, SUBLANES, d)).reshape(n, d)


def _moe_kernel(ea_ref, eb_ref, used_ref, xs_ref, g_ref, wr_ref, br_ref,
                mga_ref, mua_ref, mda_ref, mgb_ref, mub_ref, mdb_ref, ys_ref):
    t = pl.program_id(0)

    @pl.when(used_ref[t] == 0)
    def _():
        ys_ref[...] = jnp.zeros(ys_ref.shape, F32)

    @pl.when(used_ref[t] != 0)
    def _():
        x = xs_ref[...]
        h2 = _rms(x, g_ref[...]).astype(BF16)
        logits = _dot(h2, wr_ref[...]) + br_ref[...]
        lane = lax.broadcasted_iota(jnp.int32, logits.shape, 1)
        pick = lambda e: jnp.sum(jnp.where(lane == N_GROUPS + e, logits, 0.0), axis=1, keepdims=True)
        la = pick(ea_ref[t])
        lb = pick(eb_ref[t])
        top = jnp.maximum(la, lb)
        pa = jnp.exp(la - top)
        pb = jnp.exp(lb - top)

        def expert(mg_ref, mu_ref, md_ref):
            g = _dot(h2, mg_ref[...])
            act = g * _sigmoid(g) * _dot(h2, mu_ref[...])
            return _dot(act.astype(BF16), md_ref[...])

        ya = expert(mga_ref, mua_ref, mda_ref)
        yb = expert(mgb_ref, mub_ref, mdb_ref)
        ys_ref[...] = x + ((pa / (pa + pb)) * ya + (pb / (pa + pb)) * yb)


def _moe(ea, eb, used, xs, g_ffn, wr, br, mg, mu, md, tm):
    n_slots, d = xs.shape
    de = mg.shape[2]
    const = lambda t, ea, eb, used: (0, 0)
    w_a = lambda shape: pl.BlockSpec((None,) + shape, lambda t, ea, eb, used: (ea[t], 0, 0))
    w_b = lambda shape: pl.BlockSpec((None,) + shape, lambda t, ea, eb, used: (eb[t], 0, 0))
    grid_spec = pltpu.PrefetchScalarGridSpec(
        num_scalar_prefetch=3,
        grid=(n_slots // tm,),
        in_specs=[pl.BlockSpec((tm, d), lambda t, ea, eb, used: (t, 0)),
                  pl.BlockSpec(g_ffn.shape, const),
                  pl.BlockSpec(wr.shape, const),
                  pl.BlockSpec(br.shape, const),
                  w_a((d, de)), w_a((d, de)), w_a((de, d)),
                  w_b((d, de)), w_b((d, de)), w_b((de, d))],
        out_specs=pl.BlockSpec((tm, d), lambda t, ea, eb, used: (t, 0)))
    return pl.pallas_call(
        _moe_kernel,
        grid_spec=grid_spec,
        out_shape=jax.ShapeDtypeStruct((n_slots, d), F32),
        compiler_params=pltpu.CompilerParams(
            dimension_semantics=("arbitrary",), vmem_limit_bytes=VMEM_LIMIT),
        name="moe",
    )(ea, eb, used, xs, g_ffn, wr, br, mg, mu, md, mg, mu, md)


def _tile(n, pref):
    return pref if n % pref == 0 else n


def _feature_major(cache):
    b, p = cache.shape[:2]
    nd = cache.ndim
    return jnp.transpose(cache, (0,) + tuple(range(2, nd)) + (1,)).reshape(b, COLS, p)


def _token_major(xt, head_dims):
    b, _, t = xt.shape
    nd = len(head_dims)
    return jnp.transpose(xt.reshape((b,) + head_dims + (t,)), (0, nd + 1) + tuple(range(1, nd + 1)))


def kernel(x_prompt, x_sample, cache_diff_k, cache_diff_v, cache_sb_k, cache_sb_v, rel_bias_table, norm_mix_g, w_in, q_norm_g, k_norm_g, lambda_q1, lambda_k1, lambda_q2, lambda_k2, subln_g, w_o_diff, w_o_sb, w_branch_gate, b_branch_gate, w_out, norm_ffn_g, w_router_group, b_router_group, w_router_expert, b_router_expert, moe_w_gate, moe_w_up, moe_w_down):
    b, t, d = x_prompt.shape
    bs, ts, _ = x_sample.shape
    depth = norm_mix_g.shape[0]
    p = cache_diff_k.shape[2]
    tq = _tile(t, 512)
    tq_b = _tile(t, 256)
    tk_s = _tile(p, 256)
    tn = 128
    assert t % tq == 0 and t % tq_b == 0 and tq % CHUNK == 0 and p % CHUNK == 0
    assert ts <= CHUNK and ts <= tn
    assert p % tk_s == 0

    q_loc = np.arange(tq)
    bkt_prompt = np.concatenate([_bucket_tile_np(q_loc + tq, np.arange(tq) + tq),
                                 _bucket_tile_np(q_loc + tq, np.arange(tq))], axis=0)
    far = np.unique(_t5_bucket_np(-np.arange(tq + 1, 4 * tq)))
    assert far.size == 1
    far_bucket = int(far[0])
    q_s = p + np.arange(ts)
    k_s = np.concatenate([np.arange(p), p + np.arange(tn)])
    bkt_sample = _bucket_tile_np(q_s, k_s)
    bkt_sample[:, p + ts:] = -1
    bias_prompt = _bias_tiles(rel_bias_table, jnp.asarray(bkt_prompt)).reshape(H_A, 2, tq, tq)
    bias_sample = _bias_tiles(rel_bias_table, jnp.asarray(bkt_sample))
    bias_sc, bias_sn = bias_sample[:, :, :p], bias_sample[:, :, p:]

    gm = np.kron(np.eye(256 // DH), np.full((DH, DH), 1.0 / DH)).astype(np.float32)
    gm = jnp.asarray(gm, dtype=BF16)
    u_p = _upper_sum_matrix(tq_b)
    u_s = _upper_sum_matrix(tk_s)
    u_n = _upper_sum_matrix(tn)

    y_p = x_prompt
    y_s = x_sample.reshape(bs * ts, d)
    tm_proj = _tile(t, 512)
    tm_p = _tile(b * t, 512)
    tm_s = _tile(bs * ts, 128)
    tm_moe_p = _tile(b * t, 512)
    outs = [[] for _ in range(8)]
    for l in range(depth):
        lam_init = 0.8 - 0.6 * math.exp(-0.3 * l)
        w_in_b = w_in[l].astype(BF16)
        col = lambda g: w_in_b[:, g * COLS:(g + 1) * COLS]
        wn = jnp.concatenate([col(0), col(2), col(3)], axis=1)
        wt = jnp.concatenate([col(1), col(4), col(5)], axis=1).T
        g_mix = norm_mix_g[l].reshape(1, d)
        gq = jnp.tile(q_norm_g[l], COLS // DH).reshape(1, COLS)
        gk = jnp.tile(k_norm_g[l], COLS // DH).reshape(1, COLS)
        gkt = jnp.broadcast_to(gk.reshape(COLS, 1), (COLS, SLAB))
        lam_p = jnp.stack([lambda_q1[l], lambda_k1[l], lambda_q2[l], lambda_k2[l]])
        sg = subln_g[l].reshape(1, SLAB)
        wg = w_branch_gate[l].astype(BF16)
        bg = b_branch_gate[l].reshape(1, 2 * d)
        wod = w_o_diff[l].astype(BF16)
        wos = w_o_sb[l].astype(BF16)
        wout = w_out[l].astype(BF16)
        g_ffn = norm_ffn_g[l].reshape(1, d)
        n_r = N_GROUPS + N_EXPERTS
        wr = jnp.pad(jnp.concatenate([w_router_group[l], w_router_expert[l]], axis=1),
                     ((0, 0), (0, ROUTER_LANES - n_r)))
        wr = wr.astype(BF16)
        br = jnp.pad(jnp.concatenate([b_router_group[l], b_router_expert[l]]),
                     (0, ROUTER_LANES - n_r)).reshape(1, ROUTER_LANES)
        mg = moe_w_gate[l].astype(BF16)
        mu = moe_w_up[l].astype(BF16)
        md = moe_w_down[l].astype(BF16)

        def tail_and_moe(x2d, ya, yb, tm, tm_moe):
            n = x2d.shape[0]
            x2, route, counts = _tail(x2d, ya, yb, g_mix, wg, bg, wod, wos, wout, g_ffn, wr, br, tm)
            pos, ea, eb, used = _dispatch_plan(route, counts, tm_moe)
            xs = _dispatch(pos, x2, (n // tm_moe + N_CLASSES) * tm_moe, tm_moe)
            ys = _moe(ea, eb, used, xs, g_ffn, wr, br, mg, mu, md, tm_moe)
            return _combine(pos, ys, n, tm_moe)

        kat, va, kbt, vbt, qa_b, kat_b, va_b, qb_b, kbt_b, vbt_b = _project_t(
            y_p, g_mix, wn, wt, gq, gkt, gm, tm_proj)
        ya = _diff_prompt(rel_bias_table, qa_b, kat_b, va_b, bias_prompt, lam_p, sg,
                          tq, far_bucket, lam_init)
        yb = _sb_prompt(qb_b, kbt_b, vbt_b, u_p, tq_b)
        y_p = tail_and_moe(y_p.reshape(b * t, d), ya.reshape(b * t, COLS), yb.reshape(b * t, COLS),
                           tm_p, tm_moe_p).reshape(b, t, d)
        outs[0].append(_token_major(kat, (H_A, 2, DH)))
        outs[1].append(va.reshape(b, t, H_A, 2 * DH))
        outs[2].append(_token_major(kbt, (H_B, DH)))
        outs[3].append(_token_major(vbt, (H_B, DH)))

        ka, va, kb, vb, qa_b, ka_b, va_b, qb_b, kb_b, vb_b = _project(y_s, g_mix, w_in_b, gq, gk, gm, tm_s)
        s3 = lambda a: a.reshape(bs, ts, COLS)
        padk = lambda a: jnp.pad(s3(a), ((0, 0), (0, tn - ts), (0, 0)))
        ya = _diff_sample(s3(qa_b), _feature_major(cache_diff_k[l]), cache_diff_v[l].reshape(bs, p, COLS),
                          padk(ka_b), padk(va_b), bias_sc, bias_sn, lam_p, sg, lam_init)
        yb = _sb_sample(s3(qb_b), _feature_major(cache_sb_k[l]), _feature_major(cache_sb_v[l]),
                        padk(kb_b), padk(vb_b), u_s, u_n, tk_s)
        y_s = tail_and_moe(y_s, ya.reshape(bs * ts, COLS), yb.reshape(bs * ts, COLS), tm_s, tm_s)
        outs[4].append(ka.reshape(bs, ts, H_A, 2, DH))
        outs[5].append(va.reshape(bs, ts, H_A, 2 * DH))
        outs[6].append(kb.reshape(bs, ts, H_B, DH))
        outs[7].append(vb.reshape(bs, ts, H_B, DH))

    return (y_p, y_s.reshape(bs, ts, d)) + tuple(jnp.stack(o) for o in outs)
```

```python
import functools
import math

import numpy as np
import jax
import jax.numpy as jnp
from jax import lax
from jax.experimental import pallas as pl
from jax.experimental.pallas import tpu as pltpu

F32 = jnp.float32
BF16 = jnp.bfloat16

EPS = 1e-6
CHUNK = 64
H_A = 4
DH = 64
H_B = 8
SLAB = 2 * DH
N_SLAB_A = H_A
N_SLAB_B = H_B // 2
COLS = H_A * SLAB
N_BUCKETS = 32
MAX_DISTANCE = 128
N_GROUPS = 4
EXPERTS_PER_GROUP = 4
N_EXPERTS = N_GROUPS * EXPERTS_PER_GROUP
N_PAIRS = EXPERTS_PER_GROUP * (EXPERTS_PER_GROUP - 1) // 2
N_CLASSES = N_GROUPS * N_PAIRS
ROUTER_LANES = 128
NEG = -1e30
SB_DEAD = -104.0
VMEM_LIMIT = 56 * 1024 * 1024
LOG2E = math.log2(math.e)
FAR_UNROLL = 8


def _rms(x, g):
    return x * lax.rsqrt(jnp.mean(x * x, axis=-1, keepdims=True) + EPS) * g


def _sigmoid(x):
    return 1.0 / (1.0 + jnp.exp(-x))


def _dot(a, b):
    return jnp.dot(a, b, preferred_element_type=F32)


def _dot_nt(a, b):
    return lax.dot_general(a, b, (((1,), (1,)), ((), ())), preferred_element_type=F32)


def _lane_tiles(x, n):
    return x if n == 1 else jnp.concatenate([x] * n, axis=1)


def _split_bf16(x):
    hi = x.astype(BF16)
    lo = (x - hi.astype(F32)).astype(BF16)
    return hi, lo


def _cols(j, width):
    return pl.ds(pl.multiple_of(j * width, width), width)


def _sub_head_norm(t, gain, gm):
    outs = []
    for s in range(0, COLS, 256):
        ts = t[:, s:s + 256]
        hi, lo = _split_bf16(ts * ts)
        outs.append(ts * lax.rsqrt(_dot(hi, gm) + _dot(lo, gm) + EPS))
    return jnp.concatenate(outs, axis=1) * gain


def _sub_head_norm_t(t, gain, gm):
    outs = []
    for s in range(0, COLS, 256):
        ts = t[s:s + 256, :]
        hi, lo = _split_bf16(ts * ts)
        outs.append(ts * lax.rsqrt(_dot(gm, hi) + _dot(gm, lo) + EPS))
    return jnp.concatenate(outs, axis=0) * gain


def _proj_kernel(x_ref, g_ref, w_ref, gq_ref, gk_ref, gm_ref,
                 ka_o, va_o, kb_o, vb_o, qa_b, ka_b, va_b, qb_b, kb_b, vb_b):
    h = _rms(x_ref[...], g_ref[...]).astype(BF16)
    proj = _dot(h, w_ref[...])
    gm = gm_ref[...]
    qa = _sub_head_norm(proj[:, 0 * COLS:1 * COLS], gq_ref[...], gm)
    ka = _sub_head_norm(proj[:, 1 * COLS:2 * COLS], gk_ref[...], gm)
    va = proj[:, 2 * COLS:3 * COLS]
    qb = proj[:, 3 * COLS:4 * COLS]
    kb = proj[:, 4 * COLS:5 * COLS]
    vb = proj[:, 5 * COLS:6 * COLS]
    ka_o[...] = ka
    va_o[...] = va
    kb_o[...] = kb
    vb_o[...] = vb
    scale = 1.0 / math.sqrt(DH)
    qa_b[...] = (qa * (scale * LOG2E)).astype(BF16)
    ka_b[...] = ka.astype(BF16)
    va_b[...] = va.astype(BF16)
    qb_b[...] = (qb * scale).astype(BF16)
    kb_b[...] = kb.astype(BF16)
    vb_b[...] = vb.astype(BF16)


def _project(x2d, g_mix, w_in_b, gq, gk, gm, tm):
    n, d = x2d.shape
    row = lambda i: (i, 0)
    const = lambda i: (0, 0)
    out_f = jax.ShapeDtypeStruct((n, COLS), F32)
    out_b = jax.ShapeDtypeStruct((n, COLS), BF16)
    blk = pl.BlockSpec((tm, COLS), row)
    return pl.pallas_call(
        _proj_kernel,
        grid=(n // tm,),
        in_specs=[pl.BlockSpec((tm, d), row),
                  pl.BlockSpec((1, d), const),
                  pl.BlockSpec(w_in_b.shape, const),
                  pl.BlockSpec((1, COLS), const),
                  pl.BlockSpec((1, COLS), const),
                  pl.BlockSpec(gm.shape, const)],
        out_specs=[blk] * 10,
        out_shape=[out_f] * 4 + [out_b] * 6,
        compiler_params=pltpu.CompilerParams(
            dimension_semantics=("arbitrary",), vmem_limit_bytes=VMEM_LIMIT),
        name="proj",
    )(x2d, g_mix, w_in_b, gq, gk, gm)


def _proj_t_kernel(x_ref, g_ref, wn_ref, wt_ref, gq_ref, gkt_ref, gm_ref,
                   kat_o, va_o, kbt_o, vbt_o, qa_b, kat_b, va_b, qb_b, kbt_b, vbt_b):
    h = _rms(x_ref[...], g_ref[...]).astype(BF16)
    tm = h.shape[0]
    nat = _dot(h, wn_ref[...])
    tr = _dot_nt(wt_ref[...], h)
    gm = gm_ref[...]
    qa = _sub_head_norm(nat[:, 0 * COLS:1 * COLS], gq_ref[...], gm)
    va = nat[:, 1 * COLS:2 * COLS]
    qb = nat[:, 2 * COLS:3 * COLS]
    kat = _sub_head_norm_t(tr[0 * COLS:1 * COLS], _lane_tiles(gkt_ref[...], tm // SLAB), gm)
    kbt = tr[1 * COLS:2 * COLS]
    vbt = tr[2 * COLS:3 * COLS]
    kat_o[...] = kat
    va_o[...] = va
    kbt_o[...] = kbt
    vbt_o[...] = vbt
    scale = 1.0 / math.sqrt(DH)
    qa_b[...] = (qa * (scale * LOG2E)).astype(BF16)
    kat_b[...] = kat.astype(BF16)
    va_b[...] = va.astype(BF16)
    qb_b[...] = (qb * scale).astype(BF16)
    kbt_b[...] = kbt.astype(BF16)
    vbt_b[...] = vbt.astype(BF16)


def _project_t(x, g_mix, wn, wt, gq, gkt, gm, tm):
    b, t, d = x.shape
    const = lambda b, i: (0, 0)
    nat = pl.BlockSpec((None, tm, COLS), lambda b, i: (b, i, 0))
    fm = pl.BlockSpec((None, COLS, tm), lambda b, i: (b, 0, i))
    nat_s = lambda dt: jax.ShapeDtypeStruct((b, t, COLS), dt)
    fm_s = lambda dt: jax.ShapeDtypeStruct((b, COLS, t), dt)
    return pl.pallas_call(
        _proj_t_kernel,
        grid=(b, t // tm),
        in_specs=[pl.BlockSpec((None, tm, d), lambda b, i: (b, i, 0)),
                  pl.BlockSpec((1, d), const),
                  pl.BlockSpec(wn.shape, const),
                  pl.BlockSpec(wt.shape, const),
                  pl.BlockSpec((1, COLS), const),
                  pl.BlockSpec(gkt.shape, const),
                  pl.BlockSpec(gm.shape, const)],
        out_specs=[fm, nat, fm, fm, nat, fm, nat, nat, fm, fm],
        out_shape=[fm_s(F32), nat_s(F32), fm_s(F32), fm_s(F32),
                   nat_s(BF16), fm_s(BF16), nat_s(BF16), nat_s(BF16), fm_s(BF16), fm_s(BF16)],
        compiler_params=pltpu.CompilerParams(
            dimension_semantics=("arbitrary", "arbitrary"), vmem_limit_bytes=VMEM_LIMIT),
        name="proj_t",
    )(x, g_mix, wn, wt, gq, gkt, gm)


def _t5_bucket_np(rel):
    half = N_BUCKETS // 2
    max_exact = half // 2
    base = np.where(rel > 0, half, 0)
    n = np.abs(rel)
    nf = np.maximum(n, 1).astype(np.float64)
    large = max_exact + (np.log(nf / max_exact) / math.log(MAX_DISTANCE / max_exact)
                         * (half - max_exact)).astype(np.int32)
    large = np.minimum(large, half - 1)
    return (base + np.where(n < max_exact, n, large)).astype(np.int32)


def _bucket_tile_np(q_pos, k_pos):
    rel = k_pos[None, :] - q_pos[:, None]
    allowed = (k_pos[None, :] // CHUNK) <= (q_pos[:, None] // CHUNK)
    return np.where(allowed, _t5_bucket_np(rel), -1).astype(np.int32)


def _bias_kernel(tbl_ref, bkt_ref, o_ref):
    h = pl.program_id(0)
    bkt = bkt_ref[...]
    out = jnp.full(bkt.shape, NEG, F32)
    for b in range(N_BUCKETS):
        out = jnp.where(bkt == b, tbl_ref[b, h] * LOG2E, out)
    o_ref[...] = out


def _bias_tiles(rel_table, buckets):
    r, c = buckets.shape
    return pl.pallas_call(
        _bias_kernel,
        grid=(H_A,),
        in_specs=[pl.BlockSpec(memory_space=pltpu.SMEM),
                  pl.BlockSpec((r, c), lambda h: (0, 0))],
        out_specs=pl.BlockSpec((None, r, c), lambda h: (h, 0, 0)),
        out_shape=jax.ShapeDtypeStruct((H_A, r, c), F32),
        name="bias_tiles",
    )(rel_table, buckets)


def _two_maps(q):
    lane = lax.broadcasted_iota(jnp.int32, q.shape, 1)
    zero = jnp.zeros_like(q)
    return jnp.concatenate([jnp.where(lane < DH, q, zero), jnp.where(lane >= DH, q, zero)], axis=0)


def _diff_lambda(lam_ref, lam_init):
    p = lam_ref[...]
    s1 = jnp.sum(p[0:1] * p[1:2], axis=1, keepdims=True)
    s2 = jnp.sum(p[2:3] * p[3:4], axis=1, keepdims=True)
    return jnp.exp(s1) - jnp.exp(s2) + lam_init


def _diff_finish(acc, l, t, lam, sg, lam_init):
    o = acc[:t] / l[:t] - lam * (acc[t:] / l[t:])
    o = o * lax.rsqrt(jnp.mean(o * o, axis=-1, keepdims=True) + EPS) * sg
    return o * (1.0 - lam_init)


def _diff_prompt_kernel(tbl_ref, q_ref, kt_ref, v_ref, bias_ref, lam_ref, sg_ref, o_ref,
                        vx_ref, m_ref, l_ref, acc_ref, *, tq, tk, far_bucket, lam_init):
    h = pl.program_id(1)
    i = pl.program_id(2)

    @pl.when(i == 0)
    def _():
        vx_ref[:, :SLAB] = v_ref[...]
        vx_ref[:, SLAB:] = jnp.ones((vx_ref.shape[0], SLAB), vx_ref.dtype)

    qq = _two_maps(q_ref[...])
    m_ref[...] = jnp.full(m_ref.shape, NEG, F32)
    l_ref[...] = jnp.zeros(l_ref.shape, F32)
    acc_ref[...] = jnp.zeros(acc_ref.shape, F32)
    n_sub = tq // tk

    def update(jk, bias, first_row=0):
        n = tq - first_row
        segs = [(first_row, n), (tq + first_row, n)] if first_row else [(0, 2 * tq)]
        rd = lambda ref: jnp.concatenate([ref[a:a + k, :] for a, k in segs], axis=0)

        def wr(ref, val):
            for idx, (a, k) in enumerate(segs):
                ref[a:a + k, :] = val[idx * k:(idx + 1) * k]

        s = _dot(jnp.concatenate([qq[a:a + k] for a, k in segs], axis=0), kt_ref[:, _cols(jk, tk)])
        if bias is not None:
            s = s + jnp.concatenate([bias[first_row:], bias[first_row:]], axis=0)
        m_old = rd(m_ref)
        m_new = jnp.maximum(m_old, jnp.max(s, axis=1, keepdims=True))
        alpha = jnp.exp2(m_old - m_new)
        p = jnp.exp2(s - _lane_tiles(m_new, tk // SLAB))
        pv = _dot(p.astype(BF16), vx_ref[_cols(jk, tk), :])
        wr(l_ref, alpha * rd(l_ref) + pv[:, SLAB:])
        wr(acc_ref, alpha * rd(acc_ref) + pv[:, :SLAB])
        wr(m_ref, m_new)

    n_far = jnp.maximum(i - 1, 0)

    def far_tiles(first, count):
        for sub in range(count * n_sub):
            update(first * n_sub + sub, None)

    def far_body(j, carry):
        far_tiles(j * FAR_UNROLL, FAR_UNROLL)
        return carry

    lax.fori_loop(0, n_far // FAR_UNROLL, far_body, 0)
    done = n_far - n_far % FAR_UNROLL
    step = FAR_UNROLL // 2
    while step >= 1:
        take = (n_far % (2 * step)) >= step

        @pl.when(take)
        def _(done=done, step=step):
            far_tiles(done, step)

        done = done + jnp.where(take, step, 0)
        step //= 2
    m_ref[...] = m_ref[...] + tbl_ref[far_bucket, h] * LOG2E

    def biased_tile(j, kind):
        for sub in range(n_sub):
            first_row = sub * tk if kind == 0 else 0
            update(j * n_sub + sub, bias_ref[kind, :, sub * tk:(sub + 1) * tk], first_row)

    @pl.when(i >= 1)
    def _():
        biased_tile(i - 1, 1)
        biased_tile(i, 0)

    @pl.when(i == 0)
    def _():
        biased_tile(i, 0)

    lam = _diff_lambda(lam_ref, lam_init)
    o = _diff_finish(acc_ref[...], l_ref[...], tq, lam, sg_ref[...], lam_init)
    o_ref[...] = o.astype(o_ref.dtype)


def _diff_prompt(rel_table, qa, kat, va, bias, lam_p, sg, tq, far_bucket, lam_init):
    b, t, _ = qa.shape
    kern = functools.partial(_diff_prompt_kernel, tq=tq, tk=_tile(tq, 256), far_bucket=far_bucket,
                             lam_init=lam_init)
    return pl.pallas_call(
        kern,
        grid=(b, H_A, t // tq),
        in_specs=[pl.BlockSpec(memory_space=pltpu.SMEM),
                  pl.BlockSpec((None, tq, SLAB), lambda b, h, i: (b, i, h)),
                  pl.BlockSpec((None, SLAB, t), lambda b, h, i: (b, h, 0)),
                  pl.BlockSpec((None, t, SLAB), lambda b, h, i: (b, 0, h)),
                  pl.BlockSpec((None, 2, tq, tq), lambda b, h, i: (h, 0, 0, 0)),
                  pl.BlockSpec(lam_p.shape, lambda b, h, i: (0, 0)),
                  pl.BlockSpec((1, SLAB), lambda b, h, i: (0, 0))],
        out_specs=pl.BlockSpec((None, tq, SLAB), lambda b, h, i: (b, i, h)),
        out_shape=jax.ShapeDtypeStruct((b, t, COLS), BF16),
        scratch_shapes=[pltpu.VMEM((t, 2 * SLAB), BF16),
                        pltpu.VMEM((2 * tq, SLAB), F32),
                        pltpu.VMEM((2 * tq, SLAB), F32),
                        pltpu.VMEM((2 * tq, SLAB), F32)],
        compiler_params=pltpu.CompilerParams(
            dimension_semantics=("arbitrary", "arbitrary", "arbitrary"),
            vmem_limit_bytes=VMEM_LIMIT),
        name="diff_prompt",
    )(rel_table, qa, kat, va, bias, lam_p, sg)


def _diff_sample_kernel(q_ref, kct_ref, vc_ref, kn_ref, vn_ref, bc_ref, bn_ref, lam_ref, sg_ref,
                        o_ref, *, lam_init):
    qq = _two_maps(q_ref[...])
    t = q_ref.shape[0]
    bc = bc_ref[...]
    bn = bn_ref[...]
    s_c = _dot(qq, kct_ref[...].astype(BF16)) + jnp.concatenate([bc, bc], axis=0)
    s_n = _dot_nt(qq, kn_ref[...]) + jnp.concatenate([bn, bn], axis=0)
    m = jnp.maximum(jnp.max(s_c, axis=1, keepdims=True), jnp.max(s_n, axis=1, keepdims=True))
    p_c = jnp.exp2(s_c - m)
    p_n = jnp.exp2(s_n - m)
    l = jnp.sum(p_c, axis=1, keepdims=True) + jnp.sum(p_n, axis=1, keepdims=True)
    acc = _dot(p_c.astype(BF16), vc_ref[...].astype(BF16)) + _dot(p_n.astype(BF16), vn_ref[...])
    lam = _diff_lambda(lam_ref, lam_init)
    o = _diff_finish(acc, l, t, lam, sg_ref[...], lam_init)
    o_ref[...] = o.astype(o_ref.dtype)


def _diff_sample(qa, kct, vc, kn, vn, bias_c, bias_n, lam_p, sg, lam_init):
    b, t, _ = qa.shape
    p = vc.shape[1]
    tn = kn.shape[1]
    slab = lambda rows: pl.BlockSpec((None, rows, SLAB), lambda b, h: (b, 0, h))
    return pl.pallas_call(
        functools.partial(_diff_sample_kernel, lam_init=lam_init),
        grid=(b, H_A),
        in_specs=[slab(t),
                  pl.BlockSpec((None, SLAB, p), lambda b, h: (b, h, 0)),
                  slab(p), slab(tn), slab(tn),
                  pl.BlockSpec((None, t, p), lambda b, h: (h, 0, 0)),
                  pl.BlockSpec((None, t, tn), lambda b, h: (h, 0, 0)),
                  pl.BlockSpec(lam_p.shape, lambda b, h: (0, 0)),
                  pl.BlockSpec((1, SLAB), lambda b, h: (0, 0))],
        out_specs=slab(t),
        out_shape=jax.ShapeDtypeStruct((b, t, COLS), BF16),
        compiler_params=pltpu.CompilerParams(
            dimension_semantics=("arbitrary", "arbitrary"), vmem_limit_bytes=VMEM_LIMIT),
        name="diff_sample",
    )(qa, kct, vc, kn, vn, bias_c, bias_n, lam_p, sg)


def _sb_weights(z, u, c, allowed):
    sp = jnp.log(1.0 + jnp.exp(-jnp.abs(z)))
    log_beta = jnp.minimum(z, 0.0) - sp
    log_keep = -jnp.maximum(z, 0.0) - sp
    if allowed is not None:
        log_keep = jnp.where(allowed, log_keep, 0.0)
    hi, lo = _split_bf16(log_keep)
    later = _dot(hi, u) + _dot(lo, u)
    a = jnp.exp(log_beta + later + _lane_tiles(c, z.shape[1] // SLAB))
    if allowed is not None:
        a = jnp.where(allowed, a, 0.0)
    return a.astype(BF16), c + jnp.sum(log_keep, axis=1, keepdims=True)


def _strictly_causal_two_heads(t, n_keys):
    row = lax.broadcasted_iota(jnp.int32, (2 * t, n_keys), 0)
    col = lax.broadcasted_iota(jnp.int32, (2 * t, n_keys), 1)
    return col < jnp.where(row >= t, row - t, row)


def _merge_heads(acc, t):
    lane = lax.broadcasted_iota(jnp.int32, (t, SLAB), 1)
    return jnp.where(lane < DH, acc[:t], acc[t:])


def _sb_sweep_left(j0, c, acc, tile):
    def alive(c):
        return (jnp.max(c) > SB_DEAD).astype(jnp.int32)

    def cond(st):
        j, live, _, _ = st
        return (j >= 0) & (live > 0)

    def body(st):
        j, _, c, acc = st
        c, acc = tile(j, c, acc)
        return j - 1, alive(c), c, acc

    _, _, _, acc = lax.while_loop(cond, body, (j0, alive(c), c, acc))
    return acc


def _sb_prompt_kernel(q_ref, kt_ref, vt_ref, u_ref, o_ref, *, tq):
    n_q = q_ref.shape[0] // tq
    u = u_ref[...]
    mask = _strictly_causal_two_heads(tq, tq)
    zeros = jnp.zeros((2 * tq, SLAB), F32)

    def make_tile(qq):
        def tile(j, c, acc, mask=None, valid=None):
            a, c_new = _sb_weights(_dot(qq, kt_ref[:, _cols(j, tq)]), u, c, mask)
            if valid is not None:
                a = jnp.where(valid, a, jnp.zeros_like(a))
                c_new = jnp.where(valid, c_new, c)
            return c_new, acc + _dot_nt(a, vt_ref[:, _cols(j, tq)])
        return tile

    state = []
    for s in range(n_q):
        i = pl.program_id(2) * n_q + s
        tile = make_tile(_two_maps(q_ref[s * tq:(s + 1) * tq, :]))
        c, acc = tile(i, zeros, zeros, mask=mask)
        c, acc = tile(jnp.maximum(i - 1, 0), c, acc, valid=i >= 1)
        state.append((i, tile, c, acc))
    for s, (i, tile, c, acc) in enumerate(state):
        acc = _sb_sweep_left(i - 2, c, acc, tile)
        o_ref[s * tq:(s + 1) * tq, :] = _merge_heads(acc, tq).astype(o_ref.dtype)


def _sb_prompt(qb, kbt, vbt, u, tq):
    b, t, _ = qb.shape
    tg = _tile(t, 4 * tq)
    return pl.pallas_call(
        functools.partial(_sb_prompt_kernel, tq=tq),
        grid=(b, N_SLAB_B, t // tg),
        in_specs=[pl.BlockSpec((None, tg, SLAB), lambda b, h, i: (b, i, h)),
                  pl.BlockSpec((None, SLAB, t), lambda b, h, i: (b, h, 0)),
                  pl.BlockSpec((None, SLAB, t), lambda b, h, i: (b, h, 0)),
                  pl.BlockSpec(u.shape, lambda b, h, i: (0, 0))],
        out_specs=pl.BlockSpec((None, tg, SLAB), lambda b, h, i: (b, i, h)),
        out_shape=jax.ShapeDtypeStruct((b, t, COLS), BF16),
        compiler_params=pltpu.CompilerParams(
            dimension_semantics=("arbitrary", "arbitrary", "arbitrary"),
            vmem_limit_bytes=VMEM_LIMIT),
        name="sb_prompt",
    )(qb, kbt, vbt, u)


def _sb_sample_kernel(q_ref, kct_ref, vct_ref, kn_ref, vn_ref, u_ref, un_ref, o_ref, *, tk):
    t = q_ref.shape[0]
    tn = kn_ref.shape[0]
    n_tiles = kct_ref.shape[1] // tk
    qq = _two_maps(q_ref[...])
    u = u_ref[...]
    zeros = jnp.zeros((2 * t, SLAB), F32)
    a, c = _sb_weights(_dot_nt(qq, kn_ref[...]), un_ref[...], zeros,
                       _strictly_causal_two_heads(t, tn))
    acc = _dot(a, vn_ref[...])

    def tile(j, c, acc):
        a, c = _sb_weights(_dot(qq, kct_ref[:, _cols(j, tk)].astype(BF16)), u, c, None)
        return c, acc + _dot_nt(a, vct_ref[:, _cols(j, tk)].astype(BF16))

    acc = _sb_sweep_left(n_tiles - 1, c, acc, tile)
    o_ref[...] = _merge_heads(acc, t).astype(o_ref.dtype)


def _sb_sample(qb, kct, vct, kn, vn, u, un, tk):
    b, t, _ = qb.shape
    p = kct.shape[2]
    tn = kn.shape[1]
    slab = lambda rows: pl.BlockSpec((None, rows, SLAB), lambda b, h: (b, 0, h))
    slab_t = pl.BlockSpec((None, SLAB, p), lambda b, h: (b, h, 0))
    return pl.pallas_call(
        functools.partial(_sb_sample_kernel, tk=tk),
        grid=(b, N_SLAB_B),
        in_specs=[slab(t), slab_t, slab_t, slab(tn), slab(tn),
                  pl.BlockSpec(u.shape, lambda b, h: (0, 0)),
                  pl.BlockSpec(un.shape, lambda b, h: (0, 0))],
        out_specs=slab(t),
        out_shape=jax.ShapeDtypeStruct((b, t, COLS), BF16),
        compiler_params=pltpu.CompilerParams(
            dimension_semantics=("arbitrary", "arbitrary"), vmem_limit_bytes=VMEM_LIMIT),
        name="sb_sample",
    )(qb, kct, vct, kn, vn, u, un)


def _upper_sum_matrix(n):
    j = np.arange(n)[:, None]
    s = np.arange(n)[None, :]
    return jnp.asarray((j > s).astype(np.float32), dtype=BF16)


def _route_class(logits):
    lane = lax.broadcasted_iota(jnp.int32, logits.shape, 1).astype(F32)
    big = float(ROUTER_LANES)

    def first_argmax(vals):
        top = jnp.max(vals, axis=1, keepdims=True)
        return jnp.min(jnp.where(vals == top, lane, big), axis=1, keepdims=True)

    g_sel = first_argmax(jnp.where(lane < N_GROUPS, logits, NEG))
    first = N_GROUPS + g_sel * EXPERTS_PER_GROUP
    in_group = (lane >= first) & (lane < first + EXPERTS_PER_GROUP)
    el = jnp.where(in_group, logits, NEG)
    i1 = first_argmax(el)
    i2 = first_argmax(jnp.where(lane == i1, NEG, el))
    a = jnp.minimum(i1, i2) - first
    b = jnp.maximum(i1, i2) - first
    pair = a * (2 * EXPERTS_PER_GROUP - 1 - a) * 0.5 + (b - a - 1.0)
    return g_sel * N_PAIRS + pair


def _tail_kernel(x_ref, ya_ref, yb_ref, gmix_ref, wg_ref, bg_ref, wod_ref, wos_ref, wout_ref,
                 gffn_ref, wr_ref, br_ref, low_ref, x2_ref, route_ref, counts_ref, run_ref):
    @pl.when(pl.program_id(0) == 0)
    def _():
        run_ref[...] = jnp.zeros(run_ref.shape, F32)

    x = x_ref[...]
    d = x.shape[1]
    h = _rms(x, gmix_ref[...]).astype(BF16)
    gates = _sigmoid(_dot(h, wg_ref[...]) + bg_ref[...])
    mix = gates[:, :d] * _dot(ya_ref[...], wod_ref[...]) + gates[:, d:] * _dot(yb_ref[...], wos_ref[...])
    x2 = x + _dot(mix.astype(BF16), wout_ref[...])
    x2_ref[...] = x2
    h2 = _rms(x2, gffn_ref[...]).astype(BF16)
    logits = _dot(h2, wr_ref[...]) + br_ref[...]
    cls = _route_class(logits)
    lane = lax.broadcasted_iota(jnp.int32, logits.shape, 1).astype(F32)
    onehot = jnp.where(lane == cls, 1.0, 0.0)
    earlier = _dot(low_ref[...], onehot.astype(BF16)) + run_ref[...]
    rank = jnp.sum(earlier * onehot, axis=1, keepdims=True)
    route = jnp.where(lane == 0.0, cls, jnp.where(lane == 1.0, rank, 0.0))
    route_ref[...] = jnp.transpose(route)[:route_ref.shape[0], :]
    run = run_ref[...] + jnp.sum(onehot, axis=0, keepdims=True)
    run_ref[...] = run
    counts_ref[...] = run


def _tail(x2d, ya, yb, g_mix, wg, bg, wod, wos, wout, g_ffn, wr, br, tm):
    n, d = x2d.shape
    row = lambda i: (i, 0)
    const = lambda i: (0, 0)
    full = lambda a: pl.BlockSpec(a.shape, const)
    r = np.arange(tm)
    low =jnp.asarray((r[None, :] < r[:, None]).astype(np.float32), dtype=BF16)
    return pl.pallas_call(
        _tail_kernel,
        grid=(n // tm,),
        in_specs=[pl.BlockSpec((tm, d), row),
                  pl.BlockSpec((tm, COLS), row),
                  pl.BlockSpec((tm, COLS), row),
                  full(g_mix), full(wg), full(bg), full(wod), full(wos), full(wout),
                  full(g_ffn), full(wr), full(br), full(low)],
        out_specs=[pl.BlockSpec((tm, d), row),
                   pl.BlockSpec((8, tm), lambda i: (0, i)),
                   pl.BlockSpec((1, ROUTER_LANES), const)],
        out_shape=[jax.ShapeDtypeStruct((n, d), F32),
                   jax.ShapeDtypeStruct((8, n), F32),
                   jax.ShapeDtypeStruct((1, ROUTER_LANES), F32)],
        scratch_shapes=[pltpu.VMEM((1, ROUTER_LANES), F32)],
        compiler_params=pltpu.CompilerParams(
            dimension_semantics=("arbitrary",), vmem_limit_bytes=VMEM_LIMIT),
        name="tail",
    )(x2d, ya, yb, g_mix, wg, bg, wod, wos, wout, g_ffn, wr, br, low)


def _class_experts_np():
    ea, eb = [], []
    for g in range(N_GROUPS):
        for a in range(EXPERTS_PER_GROUP):
            for b in range(a + 1, EXPERTS_PER_GROUP):
                ea.append(g * EXPERTS_PER_GROUP + a)
                eb.append(g * EXPERTS_PER_GROUP + b)
    return np.asarray(ea, np.int32), np.asarray(eb, np.int32)


def _dispatch_plan(route, counts, tm):
    n = route.shape[1]
    cls = route[0].astype(jnp.int32)
    rank = route[1].astype(jnp.int32)
    cnt = counts[0, :N_CLASSES].astype(jnp.int32)
    tiles = (cnt + tm - 1) // tm
    tile_end = jnp.cumsum(tiles)
    pos = (tile_end - tiles)[cls] * tm + rank
    n_tiles = n // tm + N_CLASSES
    t_idx = jnp.arange(n_tiles, dtype=jnp.int32)
    used = t_idx < tile_end[-1]
    tile_cls = jnp.sum((t_idx[:, None] >= tile_end[None, :]).astype(jnp.int32), axis=1)
    tile_cls = jnp.where(used, tile_cls, jnp.max(jnp.where(used, tile_cls, 0)))
    ea_np, eb_np = _class_experts_np()
    return pos, jnp.asarray(ea_np)[tile_cls], jnp.asarray(eb_np)[tile_cls], used.astype(jnp.int32)


SUBLANES = 8


def _row_copies(n_rows, start_one, wait_one):
    def issue(g, carry):
        for s in range(SUBLANES):
            start_one(g, s)
        return carry

    def drain(g, carry):
        for _ in range(SUBLANES):
            wait_one()
        return carry

    lax.fori_loop(0, n_rows // SUBLANES, issue, 0)
    lax.fori_loop(0, n_rows // SUBLANES, drain, 0)


def _row3(ref, g, s):
    return ref.at[g, pl.ds(s, 1), :]


def _dispatch_kernel(hi_ref, lo_ref, x_ref, init_hbm, xs_hbm, sem):
    del init_hbm

    def start(g, s):
        r = g * SUBLANES + s
        pltpu.make_async_copy(_row3(x_ref, g, s), _row3(xs_hbm, hi_ref[0, r], lo_ref[0, r]), sem).start()

    _row_copies(x_ref.shape[0] * SUBLANES, start,
                lambda: pltpu.make_async_copy(_row3(x_ref, 0, 0), _row3(xs_hbm, 0, 0), sem).wait())


def _split_pos(pos, tm):
    n = pos.shape[0]
    blocks = lambda a: a.reshape(n // tm, 1, tm)
    return blocks(pos // SUBLANES), blocks(pos % SUBLANES)


def _pos_spec(tm):
    return pl.BlockSpec((None, 1, tm), lambda i: (i, 0, 0), memory_space=pltpu.SMEM)


def _dispatch(pos, x2, n_slots, tm):
    n, d = x2.shape
    hi, lo = _split_pos(pos, tm)
    xs = pl.pallas_call(
        _dispatch_kernel,
        grid=(n // tm,),
        in_specs=[_pos_spec(tm), _pos_spec(tm),
                  pl.BlockSpec((tm // SUBLANES, SUBLANES, d), lambda i: (i, 0, 0)),
                  pl.BlockSpec(memory_space=pl.ANY)],
        out_specs=pl.BlockSpec(memory_space=pl.ANY),
        out_shape=jax.ShapeDtypeStruct((n_slots // SUBLANES, SUBLANES, d), F32),
        scratch_shapes=[pltpu.SemaphoreType.DMA],
        input_output_aliases={3: 0},
        compiler_params=pltpu.CompilerParams(dimension_semantics=("arbitrary",)),
        name="moe_dispatch",
    )(hi, lo, x2.reshape(n // SUBLANES, SUBLANES, d), jnp.zeros((n_slots // SUBLANES, SUBLANES, d), F32))
    return xs.reshape(n_slots, d)


def _combine_kernel(hi_ref, lo_ref, ys_hbm, y_ref, sem):
    def start(g, s):
        r = g * SUBLANES + s
        pltpu.make_async_copy(_row3(ys_hbm, hi_ref[0, r], lo_ref[0, r]), _row3(y_ref, g, s), sem).start()

    _row_copies(y_ref.shape[0] * SUBLANES, start,
                lambda: pltpu.make_async_copy(_row3(ys_hbm, 0, 0), _row3(y_ref, 0, 0), sem).wait())


def _combine(pos, ys, n, tm):
    n_slots, d = ys.shape
    hi, lo = _split_pos(pos, tm)
    return pl.pallas_call(
        _combine_kernel,
        grid=(n // tm,),
        in_specs=[_pos_spec(tm), _pos_spec(tm), pl.BlockSpec(memory_space=pl.ANY)],
        out_specs=pl.BlockSpec((tm // SUBLANES, SUBLANES, d), lambda i: (i, 0, 0)),
        out_shape=jax.ShapeDtypeStruct((n // SUBLANES, SUBLANES, d), F32),
        scratch_shapes=[pltpu.SemaphoreType.DMA],
        compiler_params=pltpu.CompilerParams(dimension_semantics=("arbitrary",)),
        name="moe_combine",
    )(hi, lo, ys.reshape(n_slots // SUBLANES, SUBLANES, d)).reshape(n, d)


def _moe_kernel(ea_ref, eb_ref, used_ref, xs_ref, g_ref, wr_ref, br_ref,
                mga_ref, mua_ref, mda_ref, mgb_ref, mub_ref, mdb_ref, ys_ref):
    t = pl.program_id(0)

    @pl.when(used_ref[t] == 0)
    def _():
        ys_ref[...] = jnp.zeros(ys_ref.shape, F32)

    @pl.when(used_ref[t] != 0)
    def _():
        x = xs_ref[...]
        h2 = _rms(x, g_ref[...]).astype(BF16)
        logits = _dot(h2, wr_ref[...]) + br_ref[...]
        lane = lax.broadcasted_iota(jnp.int32, logits.shape, 1)
        pick = lambda e: jnp.sum(jnp.where(lane == N_GROUPS + e, logits, 0.0), axis=1, keepdims=True)
        la = pick(ea_ref[t])
        lb = pick(eb_ref[t])
        top = jnp.maximum(la, lb)
        pa = jnp.exp(la - top)
        pb = jnp.exp(lb - top)

        def expert(mg_ref, mu_ref, md_ref):
            g = _dot(h2, mg_ref[...])
            act = g * _sigmoid(g) * _dot(h2, mu_ref[...])
            return _dot(act.astype(BF16), md_ref[...])

        ya = expert(mga_ref, mua_ref, mda_ref)
        yb = expert(mgb_ref, mub_ref, mdb_ref)
        ys_ref[...] = x + ((pa / (pa + pb)) * ya + (pb / (pa + pb)) * yb)


def _moe(ea, eb, used, xs, g_ffn, wr, br, mg, mu, md, tm):
    n_slots, d = xs.shape
    de = mg.shape[2]
    const = lambda t, ea, eb, used: (0, 0)
    w_a = lambda shape: pl.BlockSpec((None,) + shape, lambda t, ea, eb, used: (ea[t], 0, 0))
    w_b = lambda shape: pl.BlockSpec((None,) + shape, lambda t, ea, eb, used: (eb[t], 0, 0))
    grid_spec = pltpu.PrefetchScalarGridSpec(
        num_scalar_prefetch=3,
        grid=(n_slots // tm,),
        in_specs=[pl.BlockSpec((tm, d), lambda t, ea, eb, used: (t, 0)),
                  pl.BlockSpec(g_ffn.shape, const),
                  pl.BlockSpec(wr.shape, const),
                  pl.BlockSpec(br.shape, const),
                  w_a((d, de)), w_a((d, de)), w_a((de, d)),
                  w_b((d, de)), w_b((d, de)), w_b((de, d))],
        out_specs=pl.BlockSpec((tm, d), lambda t, ea, eb, used: (t, 0)))
    return pl.pallas_call(
        _moe_kernel,
        grid_spec=grid_spec,
        out_shape=jax.ShapeDtypeStruct((n_slots, d), F32),
        compiler_params=pltpu.CompilerParams(
            dimension_semantics=("arbitrary",), vmem_limit_bytes=VMEM_LIMIT),
        name="moe",
    )(ea, eb, used, xs, g_ffn, wr, br, mg, mu, md, mg, mu, md)


def _tile(n, pref):
    return pref if n % pref == 0 else n


def _feature_major(cache):
    b, p = cache.shape[:2]
    nd = cache.ndim
    return jnp.transpose(cache, (0,) + tuple(range(2, nd)) + (1,)).reshape(b, COLS, p)


def _token_major(xt, head_dims):
    b, _, t = xt.shape
    nd = len(head_dims)
    return jnp.transpose(xt.reshape((b,) + head_dims + (t,)), (0, nd + 1) + tuple(range(1, nd + 1)))


def kernel(x_prompt, x_sample, cache_diff_k, cache_diff_v, cache_sb_k, cache_sb_v, rel_bias_table, norm_mix_g, w_in, q_norm_g, k_norm_g, lambda_q1, lambda_k1, lambda_q2, lambda_k2, subln_g, w_o_diff, w_o_sb, w_branch_gate, b_branch_gate, w_out, norm_ffn_g, w_router_group, b_router_group, w_router_expert, b_router_expert, moe_w_gate, moe_w_up, moe_w_down):
    b, t, d = x_prompt.shape
    bs, ts, _ = x_sample.shape
    depth = norm_mix_g.shape[0]
    p = cache_diff_k.shape[2]
    tq = _tile(t, 512)
    tq_b = _tile(t, 256)
    tk_s = _tile(p, 256)
    tn = 128
    assert t % tq == 0 and t % tq_b == 0 and tq % CHUNK == 0 and p % CHUNK == 0
    assert ts <= CHUNK and ts <= tn
    assert p % tk_s == 0

    q_loc = np.arange(tq)
    bkt_prompt = np.concatenate([_bucket_tile_np(q_loc + tq, np.arange(tq) + tq),
                                 _bucket_tile_np(q_loc + tq, np.arange(tq))], axis=0)
    far = np.unique(_t5_bucket_np(-np.arange(tq + 1, 4 * tq)))
    assert far.size == 1
    far_bucket = int(far[0])
    q_s = p + np.arange(ts)
    k_s = np.concatenate([np.arange(p), p + np.arange(tn)])
    bkt_sample = _bucket_tile_np(q_s, k_s)
    bkt_sample[:, p + ts:] = -1
    bias_prompt = _bias_tiles(rel_bias_table, jnp.asarray(bkt_prompt)).reshape(H_A, 2, tq, tq)
    bias_sample = _bias_tiles(rel_bias_table, jnp.asarray(bkt_sample))
    bias_sc, bias_sn = bias_sample[:, :, :p], bias_sample[:, :, p:]

    gm = np.kron(np.eye(256 // DH), np.full((DH, DH), 1.0 / DH)).astype(np.float32)
    gm = jnp.asarray(gm, dtype=BF16)
    u_p = _upper_sum_matrix(tq_b)
    u_s = _upper_sum_matrix(tk_s)
    u_n = _upper_sum_matrix(tn)

    y_p = x_prompt
    y_s = x_sample.reshape(bs * ts, d)
    tm_proj = _tile(t, 512)
    tm_p = _tile(b * t, 512)
    tm_s = _tile(bs * ts, 128)
    tm_moe_p = _tile(b * t, 512)
    outs = [[] for _ in range(8)]
    for l in range(depth):
        lam_init = 0.8 - 0.6 * math.exp(-0.3 * l)
        w_in_b = w_in[l].astype(BF16)
        col = lambda g: w_in_b[:, g * COLS:(g + 1) * COLS]
        wn = jnp.concatenate([col(0), col(2), col(3)], axis=1)
        wt = jnp.concatenate([col(1), col(4), col(5)], axis=1).T
        g_mix = norm_mix_g[l].reshape(1, d)
        gq = jnp.tile(q_norm_g[l], COLS // DH).reshape(1, COLS)
        gk = jnp.tile(k_norm_g[l], COLS // DH).reshape(1, COLS)
        gkt = jnp.broadcast_to(gk.reshape(COLS, 1), (COLS, SLAB))
        lam_p = jnp.stack([lambda_q1[l], lambda_k1[l], lambda_q2[l], lambda_k2[l]])
        sg = subln_g[l].reshape(1, SLAB)
        wg = w_branch_gate[l].astype(BF16)
        bg = b_branch_gate[l].reshape(1, 2 * d)
        wod = w_o_diff[l].astype(BF16)
        wos = w_o_sb[l].astype(BF16)
        wout = w_out[l].astype(BF16)
        g_ffn = norm_ffn_g[l].reshape(1, d)
        n_r = N_GROUPS + N_EXPERTS
        wr = jnp.pad(jnp.concatenate([w_router_group[l], w_router_expert[l]], axis=1),
                     ((0, 0), (0, ROUTER_LANES - n_r)))
        wr = wr.astype(BF16)
        br = jnp.pad(jnp.concatenate([b_router_group[l], b_router_expert[l]]),
                     (0, ROUTER_LANES - n_r)).reshape(1, ROUTER_LANES)
        mg = moe_w_gate[l].astype(BF16)
        mu = moe_w_up[l].astype(BF16)
        md = moe_w_down[l].astype(BF16)

        def tail_and_moe(x2d, ya, yb, tm, tm_moe):
            n = x2d.shape[0]
            x2, route, counts = _tail(x2d, ya, yb, g_mix, wg, bg, wod, wos, wout, g_ffn, wr, br, tm)
            pos, ea, eb, used = _dispatch_plan(route, counts, tm_moe)
            xs = _dispatch(pos, x2, (n // tm_moe + N_CLASSES) * tm_moe, tm_moe)
            ys = _moe(ea, eb, used, xs, g_ffn, wr, br, mg, mu, md, tm_moe)
            return _combine(pos, ys, n, tm_moe)

        kat, va, kbt, vbt, qa_b, kat_b, va_b, qb_b, kbt_b, vbt_b = _project_t(
            y_p, g_mix, wn, wt, gq, gkt, gm, tm_proj)
        ya = _diff_prompt(rel_bias_table, qa_b, kat_b, va_b, bias_prompt, lam_p, sg,
                          tq, far_bucket, lam_init)
        yb = _sb_prompt(qb_b, kbt_b, vbt_b, u_p, tq_b)
        y_p = tail_and_moe(y_p.reshape(b * t, d), ya.reshape(b * t, COLS), yb.reshape(b * t, COLS),
                           tm_p, tm_moe_p).reshape(b, t, d)
        outs[0].append(_token_major(kat, (H_A, 2, DH)))
        outs[1].append(va.reshape(b, t, H_A, 2 * DH))
        outs[2].append(_token_major(kbt, (H_B, DH)))
        outs[3].append(_token_major(vbt, (H_B, DH)))

        ka, va, kb, vb, qa_b, ka_b, va_b, qb_b, kb_b, vb_b = _project(y_s, g_mix, w_in_b, gq, gk, gm, tm_s)
        s3 = lambda a: a.reshape(bs, ts, COLS)
        padk = lambda a: jnp.pad(s3(a), ((0, 0), (0, tn - ts), (0, 0)))
        ya = _diff_sample(s3(qa_b), _feature_major(cache_diff_k[l]), cache_diff_v[l].reshape(bs, p, COLS),
                          padk(ka_b), padk(va_b), bias_sc, bias_sn, lam_p, sg, lam_init)
        yb = _sb_sample(s3(qb_b), _feature_major(cache_sb_k[l]), _feature_major(cache_sb_v[l]),
                        padk(kb_b), padk(vb_b), u_s, u_n, tk_s)
        y_s = tail_and_moe(y_s, ya.reshape(bs * ts, COLS), yb.reshape(bs * ts, COLS), tm_s, tm_s)
        outs[4].append(ka.reshape(bs, ts, H_A, 2, DH))
        outs[5].append(va.reshape(bs, ts, H_A, 2 * DH))
        outs[6].append(kb.reshape(bs, ts, H_B, DH))
        outs[7].append(vb.reshape(bs, ts, H_B, DH))

    return (y_p, y_s.reshape(bs, ts, d)) + tuple(jnp.stack(o) for o in outs)
```

```python
import functools
import math

import numpy as np
import jax
import jax.numpy as jnp
from jax import lax
from jax.experimental import pallas as pl
from jax.experimental.pallas import tpu as pltpu

F32 = jnp.float32
BF16 = jnp.bfloat16

EPS = 1e-6
CHUNK = 64
H_A = 4
DH = 64
H_B = 8
SLAB = 2 * DH
N_SLAB_A = H_A
N_SLAB_B = H_B // 2
COLS = H_A * SLAB
N_BUCKETS = 32
MAX_DISTANCE = 128
N_GROUPS = 4
EXPERTS_PER_GROUP = 4
N_EXPERTS = N_GROUPS * EXPERTS_PER_GROUP
N_PAIRS = EXPERTS_PER_GROUP * (EXPERTS_PER_GROUP - 1) // 2
N_CLASSES = N_GROUPS * N_PAIRS
ROUTER_LANES = 128
NEG = -1e30
SB_DEAD = -104.0
VMEM_LIMIT = 56 * 1024 * 1024
LOG2E = math.log2(math.e)
FAR_UNROLL = 8


def _rms(x, g):
    return x * lax.rsqrt(jnp.mean(x * x, axis=-1, keepdims=True) + EPS) * g


def _sigmoid(x):
    return 1.0 / (1.0 + jnp.exp(-x))


def _dot(a, b):
    return jnp.dot(a, b, preferred_element_type=F32)


def _dot_nt(a, b):
    return lax.dot_general(a, b, (((1,), (1,)), ((), ())), preferred_element_type=F32)


def _lane_tiles(x, n):
    return x if n == 1 else jnp.concatenate([x] * n, axis=1)


def _split_bf16(x):
    hi = x.astype(BF16)
    lo = (x - hi.astype(F32)).astype(BF16)
    return hi, lo


def _cols(j, width):
    return pl.ds(pl.multiple_of(j * width, width), width)


def _sub_head_norm(t, gain, gm):
    outs = []
    for s in range(0, COLS, 256):
        ts = t[:, s:s + 256]
        hi, lo = _split_bf16(ts * ts)
        outs.append(ts * lax.rsqrt(_dot(hi, gm) + _dot(lo, gm) + EPS))
    return jnp.concatenate(outs, axis=1) * gain


def _sub_head_norm_t(t, gain, gm):
    outs = []
    for s in range(0, COLS, 256):
        ts = t[s:s + 256, :]
        hi, lo = _split_bf16(ts * ts)
        outs.append(ts * lax.rsqrt(_dot(gm, hi) + _dot(gm, lo) + EPS))
    return jnp.concatenate(outs, axis=0) * gain


def _proj_kernel(x_ref, g_ref, w_ref, gq_ref, gk_ref, gm_ref,
                 ka_o, va_o, kb_o, vb_o, qa_b, ka_b, va_b, qb_b, kb_b, vb_b):
    h = _rms(x_ref[...], g_ref[...]).astype(BF16)
    proj = _dot(h, w_ref[...])
    gm = gm_ref[...]
    qa = _sub_head_norm(proj[:, 0 * COLS:1 * COLS], gq_ref[...], gm)
    ka = _sub_head_norm(proj[:, 1 * COLS:2 * COLS], gk_ref[...], gm)
    va = proj[:, 2 * COLS:3 * COLS]
    qb = proj[:, 3 * COLS:4 * COLS]
    kb = proj[:, 4 * COLS:5 * COLS]
    vb = proj[:, 5 * COLS:6 * COLS]
    ka_o[...] = ka
    va_o[...] = va
    kb_o[...] = kb
    vb_o[...] = vb
    scale = 1.0 / math.sqrt(DH)
    qa_b[...] = (qa * (scale * LOG2E)).astype(BF16)
    ka_b[...] = ka.astype(BF16)
    va_b[...] = va.astype(BF16)
    qb_b[...] = (qb * scale).astype(BF16)
    kb_b[...] = kb.astype(BF16)
    vb_b[...] = vb.astype(BF16)


def _project(x2d, g_mix, w_in_b, gq, gk, gm, tm):
    n, d = x2d.shape
    row = lambda i: (i, 0)
    const = lambda i: (0, 0)
    out_f = jax.ShapeDtypeStruct((n, COLS), F32)
    out_b = jax.ShapeDtypeStruct((n, COLS), BF16)
    blk = pl.BlockSpec((tm, COLS), row)
    return pl.pallas_call(
        _proj_kernel,
        grid=(n // tm,),
        in_specs=[pl.BlockSpec((tm, d), row),
                  pl.BlockSpec((1, d), const),
                  pl.BlockSpec(w_in_b.shape, const),
                  pl.BlockSpec((1, COLS), const),
                  pl.BlockSpec((1, COLS), const),
                  pl.BlockSpec(gm.shape, const)],
        out_specs=[blk] * 10,
        out_shape=[out_f] * 4 + [out_b] * 6,
        compiler_params=pltpu.CompilerParams(
            dimension_semantics=("arbitrary",), vmem_limit_bytes=VMEM_LIMIT),
        name="proj",
    )(x2d, g_mix, w_in_b, gq, gk, gm)


def _proj_t_kernel(x_ref, g_ref, wn_ref, wt_ref, gq_ref, gkt_ref, gm_ref,
                   kat_o, va_o, kbt_o, vbt_o, qa_b, kat_b, va_b, qb_b, kbt_b, vbt_b):
    h = _rms(x_ref[...], g_ref[...]).astype(BF16)
    tm = h.shape[0]
    nat = _dot(h, wn_ref[...])
    tr = _dot_nt(wt_ref[...], h)
    gm = gm_ref[...]
    qa = _sub_head_norm(nat[:, 0 * COLS:1 * COLS], gq_ref[...], gm)
    va = nat[:, 1 * COLS:2 * COLS]
    qb = nat[:, 2 * COLS:3 * COLS]
    kat = _sub_head_norm_t(tr[0 * COLS:1 * COLS], _lane_tiles(gkt_ref[...], tm // SLAB), gm)
    kbt = tr[1 * COLS:2 * COLS]
    vbt = tr[2 * COLS:3 * COLS]
    kat_o[...] = kat
    for hh in range(H_A):
        va_o[:, hh, :] = va[:, hh * SLAB:(hh + 1) * SLAB]
    kbt_o[...] = kbt
    vbt_o[...] = vbt
    scale = 1.0 / math.sqrt(DH)
    qa_b[...] = (qa * (scale * LOG2E)).astype(BF16)
    kat_b[...] = kat.astype(BF16)
    va_b[...] = va.astype(BF16)
    qb_b[...] = (qb * scale).astype(BF16)
    kbt_b[...] = kbt.astype(BF16)
    vbt_b[...] = vbt.astype(BF16)


def _project_t(x, g_mix, wn, wt, gq, gkt, gm, tm):
    b, t, d = x.shape
    const = lambda b, i: (0, 0)
    nat = pl.BlockSpec((None, tm, COLS), lambda b, i: (b, i, 0))
    fm = pl.BlockSpec((None, COLS, tm), lambda b, i: (b, 0, i))
    nat_s = lambda dt: jax.ShapeDtypeStruct((b, t, COLS), dt)
    fm_s = lambda dt: jax.ShapeDtypeStruct((b, COLS, t), dt)
    return pl.pallas_call(
        _proj_t_kernel,
        grid=(b, t // tm),
        in_specs=[pl.BlockSpec((None, tm, d), lambda b, i: (b, i, 0)),
                  pl.BlockSpec((1, d), const),
                  pl.BlockSpec(wn.shape, const),
                  pl.BlockSpec(wt.shape, const),
                  pl.BlockSpec((1, COLS), const),
                  pl.BlockSpec(gkt.shape, const),
                  pl.BlockSpec(gm.shape, const)],
        out_specs=[fm, pl.BlockSpec((None, tm, H_A, SLAB), lambda b, i: (b, i, 0, 0)), fm, fm,
                   nat, fm, nat, nat, fm, fm],
        out_shape=[fm_s(F32), jax.ShapeDtypeStruct((b, t, H_A, SLAB), F32), fm_s(F32), fm_s(F32),
                   nat_s(BF16), fm_s(BF16), nat_s(BF16), nat_s(BF16), fm_s(BF16), fm_s(BF16)],
        compiler_params=pltpu.CompilerParams(
            dimension_semantics=("arbitrary", "arbitrary"), vmem_limit_bytes=VMEM_LIMIT),
        name="proj_t",
    )(x, g_mix, wn, wt, gq, gkt, gm)


def _t5_bucket_np(rel):
    half = N_BUCKETS // 2
    max_exact = half // 2
    base = np.where(rel > 0, half, 0)
    n = np.abs(rel)
    nf = np.maximum(n, 1).astype(np.float64)
    large = max_exact + (np.log(nf / max_exact) / math.log(MAX_DISTANCE / max_exact)
                         * (half - max_exact)).astype(np.int32)
    large = np.minimum(large, half - 1)
    return (base + np.where(n < max_exact, n, large)).astype(np.int32)


def _bucket_tile_np(q_pos, k_pos):
    rel = k_pos[None, :] - q_pos[:, None]
    allowed = (k_pos[None, :] // CHUNK) <= (q_pos[:, None] // CHUNK)
    return np.where(allowed, _t5_bucket_np(rel), -1).astype(np.int32)


def _bias_kernel(tbl_ref, bkt_ref, o_ref):
    h = pl.program_id(0)
    bkt = bkt_ref[...]
    out = jnp.full(bkt.shape, NEG, F32)
    for b in range(N_BUCKETS):
        out = jnp.where(bkt == b, tbl_ref[b, h] * LOG2E, out)
    o_ref[...] = out


def _bias_tiles(rel_table, buckets):
    r, c = buckets.shape
    return pl.pallas_call(
        _bias_kernel,
        grid=(H_A,),
        in_specs=[pl.BlockSpec(memory_space=pltpu.SMEM),
                  pl.BlockSpec((r, c), lambda h: (0, 0))],
        out_specs=pl.BlockSpec((None, r, c), lambda h: (h, 0, 0)),
        out_shape=jax.ShapeDtypeStruct((H_A, r, c), F32),
        name="bias_tiles",
    )(rel_table, buckets)


def _two_maps(q):
    lane = lax.broadcasted_iota(jnp.int32, q.shape, 1)
    zero = jnp.zeros_like(q)
    return jnp.concatenate([jnp.where(lane < DH, q, zero), jnp.where(lane >= DH, q, zero)], axis=0)


def _diff_lambda(lam_ref, lam_init):
    p = lam_ref[...]
    s1 = jnp.sum(p[0:1] * p[1:2], axis=1, keepdims=True)
    s2 = jnp.sum(p[2:3] * p[3:4], axis=1, keepdims=True)
    return jnp.exp(s1) - jnp.exp(s2) + lam_init


def _diff_finish(acc, l, t, lam, sg, lam_init):
    o = acc[:t] / l[:t] - lam * (acc[t:] / l[t:])
    o = o * lax.rsqrt(jnp.mean(o * o, axis=-1, keepdims=True) + EPS) * sg
    return o * (1.0 - lam_init)


def _diff_prompt_kernel(tbl_ref, q_ref, kt_ref, v_ref, bias_ref, lam_ref, sg_ref, o_ref,
                        vx_ref, m_ref, l_ref, acc_ref, *, tq, tk, far_bucket, lam_init):
    h = pl.program_id(1)
    i = pl.program_id(2)

    @pl.when(i == 0)
    def _():
        vx_ref[:, :SLAB] = v_ref[...]
        vx_ref[:, SLAB:] = jnp.ones((vx_ref.shape[0], SLAB), vx_ref.dtype)

    qq = _two_maps(q_ref[...])
    m_ref[...] = jnp.full(m_ref.shape, NEG, F32)
    l_ref[...] = jnp.zeros(l_ref.shape, F32)
    acc_ref[...] = jnp.zeros(acc_ref.shape, F32)
    n_sub = tq // tk

    def update(jk, bias, first_row=0):
        n = tq - first_row
        segs = [(first_row, n), (tq + first_row, n)] if first_row else [(0, 2 * tq)]
        rd = lambda ref: jnp.concatenate([ref[a:a + k, :] for a, k in segs], axis=0)

        def wr(ref, val):
            for idx, (a, k) in enumerate(segs):
                ref[a:a + k, :] = val[idx * k:(idx + 1) * k]

        s = _dot(jnp.concatenate([qq[a:a + k] for a, k in segs], axis=0), kt_ref[:, _cols(jk, tk)])
        if bias is not None:
            s = s + jnp.concatenate([bias[first_row:], bias[first_row:]], axis=0)
        m_old = rd(m_ref)
        m_new = jnp.maximum(m_old, jnp.max(s, axis=1, keepdims=True))
        alpha = jnp.exp2(m_old - m_new)
        p = jnp.exp2(s - _lane_tiles(m_new, tk // SLAB))
        pv = _dot(p.astype(BF16), vx_ref[_cols(jk, tk), :])
        wr(l_ref, alpha * rd(l_ref) + pv[:, SLAB:])
        wr(acc_ref, alpha * rd(acc_ref) + pv[:, :SLAB])
        wr(m_ref, m_new)

    n_far = jnp.maximum(i - 1, 0)

    def far_tiles(first, count):
        for sub in range(count * n_sub):
            update(first * n_sub + sub, None)

    def far_body(j, carry):
        far_tiles(j * FAR_UNROLL, FAR_UNROLL)
        return carry

    lax.fori_loop(0, n_far // FAR_UNROLL, far_body, 0)
    done = n_far - n_far % FAR_UNROLL
    step = FAR_UNROLL // 2
    while step >= 1:
        take = (n_far % (2 * step)) >= step

        @pl.when(take)
        def _(done=done, step=step):
            far_tiles(done, step)

        done = done + jnp.where(take, step, 0)
        step //= 2
    m_ref[...] = m_ref[...] + tbl_ref[far_bucket, h] * LOG2E

    def biased_tile(j, kind):
        for sub in range(n_sub):
            first_row = sub * tk if kind == 0 else 0
            update(j * n_sub + sub, bias_ref[kind, :, sub * tk:(sub + 1) * tk], first_row)

    @pl.when(i >= 1)
    def _():
        biased_tile(i - 1, 1)
        biased_tile(i, 0)

    @pl.when(i == 0)
    def _():
        biased_tile(i, 0)

    lam = _diff_lambda(lam_ref, lam_init)
    o = _diff_finish(acc_ref[...], l_ref[...], tq, lam, sg_ref[...], lam_init)
    o_ref[...] = o.astype(o_ref.dtype)


def _diff_prompt(rel_table, qa, kat, va, bias, lam_p, sg, tq, far_bucket, lam_init):
    b, t, _ = qa.shape
    kern = functools.partial(_diff_prompt_kernel, tq=tq, tk=_tile(tq, 256), far_bucket=far_bucket,
                             lam_init=lam_init)
    return pl.pallas_call(
        kern,
        grid=(b, H_A, t // tq),
        in_specs=[pl.BlockSpec(memory_space=pltpu.SMEM),
                  pl.BlockSpec((None, tq, SLAB), lambda b, h, i: (b, i, h)),
                  pl.BlockSpec((None, SLAB, t), lambda b, h, i: (b, h, 0)),
                  pl.BlockSpec((None, t, SLAB), lambda b, h, i: (b, 0, h)),
                  pl.BlockSpec((None, 2, tq, tq), lambda b, h, i: (h, 0, 0, 0)),
                  pl.BlockSpec(lam_p.shape, lambda b, h, i: (0, 0)),
                  pl.BlockSpec((1, SLAB), lambda b, h, i: (0, 0))],
        out_specs=pl.BlockSpec((None, tq, SLAB), lambda b, h, i: (b, i, h)),
        out_shape=jax.ShapeDtypeStruct((b, t, COLS), BF16),
        scratch_shapes=[pltpu.VMEM((t, 2 * SLAB), BF16),
                        pltpu.VMEM((2 * tq, SLAB), F32),
                        pltpu.VMEM((2 * tq, SLAB), F32),
                        pltpu.VMEM((2 * tq, SLAB), F32)],
        compiler_params=pltpu.CompilerParams(
            dimension_semantics=("arbitrary", "arbitrary", "arbitrary"),
            vmem_limit_bytes=VMEM_LIMIT),
        name="diff_prompt",
    )(rel_table, qa, kat, va, bias, lam_p, sg)


def _diff_sample_kernel(q_ref, kct_ref, vc_ref, kn_ref, vn_ref, bc_ref, bn_ref, lam_ref, sg_ref,
                        o_ref, *, lam_init):
    qq = _two_maps(q_ref[...])
    t = q_ref.shape[0]
    bc = bc_ref[...]
    bn = bn_ref[...]
    s_c = _dot(qq, kct_ref[...].astype(BF16)) + jnp.concatenate([bc, bc], axis=0)
    s_n = _dot_nt(qq, kn_ref[...]) + jnp.concatenate([bn, bn], axis=0)
    m = jnp.maximum(jnp.max(s_c, axis=1, keepdims=True), jnp.max(s_n, axis=1, keepdims=True))
    p_c = jnp.exp2(s_c - m)
    p_n = jnp.exp2(s_n - m)
    l = jnp.sum(p_c, axis=1, keepdims=True) + jnp.sum(p_n, axis=1, keepdims=True)
    acc = _dot(p_c.astype(BF16), vc_ref[...].astype(BF16)) + _dot(p_n.astype(BF16), vn_ref[...])
    lam = _diff_lambda(lam_ref, lam_init)
    o = _diff_finish(acc, l, t, lam, sg_ref[...], lam_init)
    o_ref[...] = o.astype(o_ref.dtype)


def _diff_sample(qa, kct, vc, kn, vn, bias_c, bias_n, lam_p, sg, lam_init):
    b, t, _ = qa.shape
    p = vc.shape[1]
    tn = kn.shape[1]
    slab = lambda rows: pl.BlockSpec((None, rows, SLAB), lambda b, h: (b, 0, h))
    return pl.pallas_call(
        functools.partial(_diff_sample_kernel, lam_init=lam_init),
        grid=(b, H_A),
        in_specs=[slab(t),
                  pl.BlockSpec((None, SLAB, p), lambda b, h: (b, h, 0)),
                  slab(p), slab(tn), slab(tn),
                  pl.BlockSpec((None, t, p), lambda b, h: (h, 0, 0)),
                  pl.BlockSpec((None, t, tn), lambda b, h: (h, 0, 0)),
                  pl.BlockSpec(lam_p.shape, lambda b, h: (0, 0)),
                  pl.BlockSpec((1, SLAB), lambda b, h: (0, 0))],
        out_specs=slab(t),
        out_shape=jax.ShapeDtypeStruct((b, t, COLS), BF16),
        compiler_params=pltpu.CompilerParams(
            dimension_semantics=("arbitrary", "arbitrary"), vmem_limit_bytes=VMEM_LIMIT),
        name="diff_sample",
    )(qa, kct, vc, kn, vn, bias_c, bias_n, lam_p, sg)


def _sb_weights(z, u, c, allowed):
    sp = jnp.log(1.0 + jnp.exp(-jnp.abs(z)))
    log_beta = jnp.minimum(z, 0.0) - sp
    log_keep = -jnp.maximum(z, 0.0) - sp
    if allowed is not None:
        log_keep = jnp.where(allowed, log_keep, 0.0)
    hi, lo = _split_bf16(log_keep)
    later = _dot(hi, u) + _dot(lo, u)
    a = jnp.exp(log_beta + later + _lane_tiles(c, z.shape[1] // SLAB))
    if allowed is not None:
        a = jnp.where(allowed, a, 0.0)
    return a.astype(BF16), c + jnp.sum(log_keep, axis=1, keepdims=True)


def _strictly_causal_two_heads(t, n_keys):
    row = lax.broadcasted_iota(jnp.int32, (2 * t, n_keys), 0)
    col = lax.broadcasted_iota(jnp.int32, (2 * t, n_keys), 1)
    return col < jnp.where(row >= t, row - t, row)


def _merge_heads(acc, t):
    lane = lax.broadcasted_iota(jnp.int32, (t, SLAB), 1)
    return jnp.where(lane < DH, acc[:t], acc[t:])


def _sb_sweep_left(j0, c, acc, tile):
    def alive(c):
        return (jnp.max(c) > SB_DEAD).astype(jnp.int32)

    def cond(st):
        j, live, _, _ = st
        return (j >= 0) & (live > 0)

    def body(st):
        j, _, c, acc = st
        c, acc = tile(j, c, acc)
        return j - 1, alive(c), c, acc

    _, _, _, acc = lax.while_loop(cond, body, (j0, alive(c), c, acc))
    return acc


def _sb_prompt_kernel(q_ref, kt_ref, vt_ref, u_ref, o_ref, *, tq):
    n_q = q_ref.shape[0] // tq
    u = u_ref[...]
    mask = _strictly_causal_two_heads(tq, tq)
    zeros = jnp.zeros((2 * tq, SLAB), F32)

    def make_tile(qq):
        def tile(j, c, acc, mask=None, valid=None):
            a, c_new = _sb_weights(_dot(qq, kt_ref[:, _cols(j, tq)]), u, c, mask)
            if valid is not None:
                a = jnp.where(valid, a, jnp.zeros_like(a))
                c_new = jnp.where(valid, c_new, c)
            return c_new, acc + _dot_nt(a, vt_ref[:, _cols(j, tq)])
        return tile

    state = []
    for s in range(n_q):
        i = pl.program_id(2) * n_q + s
        tile = make_tile(_two_maps(q_ref[s * tq:(s + 1) * tq, :]))
        c, acc = tile(i, zeros, zeros, mask=mask)
        c, acc = tile(jnp.maximum(i - 1, 0), c, acc, valid=i >= 1)
        state.append((i, tile, c, acc))
    for s, (i, tile, c, acc) in enumerate(state):
        acc = _sb_sweep_left(i - 2, c, acc, tile)
        o_ref[s * tq:(s + 1) * tq, :] = _merge_heads(acc, tq).astype(o_ref.dtype)


def _sb_prompt(qb, kbt, vbt, u, tq):
    b, t, _ = qb.shape
    tg = _tile(t, 4 * tq)
    return pl.pallas_call(
        functools.partial(_sb_prompt_kernel, tq=tq),
        grid=(b, N_SLAB_B, t // tg),
        in_specs=[pl.BlockSpec((None, tg, SLAB), lambda b, h, i: (b, i, h)),
                  pl.BlockSpec((None, SLAB, t), lambda b, h, i: (b, h, 0)),
                  pl.BlockSpec((None, SLAB, t), lambda b, h, i: (b, h, 0)),
                  pl.BlockSpec(u.shape, lambda b, h, i: (0, 0))],
        out_specs=pl.BlockSpec((None, tg, SLAB), lambda b, h, i: (b, i, h)),
        out_shape=jax.ShapeDtypeStruct((b, t, COLS), BF16),
        compiler_params=pltpu.CompilerParams(
            dimension_semantics=("arbitrary", "arbitrary", "arbitrary"),
            vmem_limit_bytes=VMEM_LIMIT),
        name="sb_prompt",
    )(qb, kbt, vbt, u)


def _sb_sample_kernel(q_ref, kct_ref, vct_ref, kn_ref, vn_ref, u_ref, un_ref, o_ref, *, tk):
    t = q_ref.shape[0]
    tn = kn_ref.shape[0]
    n_tiles = kct_ref.shape[1] // tk
    qq = _two_maps(q_ref[...])
    u = u_ref[...]
    zeros = jnp.zeros((2 * t, SLAB), F32)
    a, c = _sb_weights(_dot_nt(qq, kn_ref[...]), un_ref[...], zeros,
                       _strictly_causal_two_heads(t, tn))
    acc = _dot(a, vn_ref[...])

    def tile(j, c, acc):
        a, c = _sb_weights(_dot(qq, kct_ref[:, _cols(j, tk)].astype(BF16)), u, c, None)
        return c, acc + _dot_nt(a, vct_ref[:, _cols(j, tk)].astype(BF16))

    acc = _sb_sweep_left(n_tiles - 1, c, acc, tile)
    o_ref[...] = _merge_heads(acc, t).astype(o_ref.dtype)


def _sb_sample(qb, kct, vct, kn, vn, u, un, tk):
    b, t, _ = qb.shape
    p = kct.shape[2]
    tn = kn.shape[1]
    slab = lambda rows: pl.BlockSpec((None, rows, SLAB), lambda b, h: (b, 0, h))
    slab_t = pl.BlockSpec((None, SLAB, p), lambda b, h: (b, h, 0))
    return pl.pallas_call(
        functools.partial(_sb_sample_kernel, tk=tk),
        grid=(b, N_SLAB_B),
        in_specs=[slab(t), slab_t, slab_t, slab(tn), slab(tn),
                  pl.BlockSpec(u.shape, lambda b, h: (0, 0)),
                  pl.BlockSpec(un.shape, lambda b, h: (0, 0))],
        out_specs=slab(t),
        out_shape=jax.ShapeDtypeStruct((b, t, COLS), BF16),
        compiler_params=pltpu.CompilerParams(
            dimension_semantics=("arbitrary", "arbitrary"), vmem_limit_bytes=VMEM_LIMIT),
        name="sb_sample",
    )(qb, kct, vct, kn, vn, u, un)


def _upper_sum_matrix(n):
    j = np.arange(n)[:, None]
    s = np.arange(n)[None, :]
    return jnp.asarray((j > s).astype(np.float32), dtype=BF16)


def _route_class(logits):
    lane = lax.broadcasted_iota(jnp.int32, logits.shape, 1).astype(F32)
    big = float(ROUTER_LANES)

    def first_argmax(vals):
        top = jnp.max(vals, axis=1, keepdims=True)
        return jnp.min(jnp.where(vals == top, lane, big), axis=1, keepdims=True)

    g_sel = first_argmax(jnp.where(lane < N_GROUPS, logits, NEG))
    first = N_GROUPS + g_sel * EXPERTS_PER_GROUP
    in_group = (lane >= first) & (lane < first + EXPERTS_PER_GROUP)
    el = jnp.where(in_group, logits, NEG)
    i1 = first_argmax(el)
    i2 = first_argmax(jnp.where(lane == i1, NEG, el))
    a = jnp.minimum(i1, i2) - first
    b = jnp.maximum(i1, i2) - first
    pair = a * (2 * EXPERTS_PER_GROUP - 1 - a) * 0.5 + (b - a - 1.0)
    return g_sel * N_PAIRS + pair


def _tail_kernel(x_ref, ya_ref, yb_ref, gmix_ref, wg_ref, bg_ref, wod_ref, wos_ref, wout_ref,
                 gffn_ref, wr_ref, br_ref, low_ref, x2_ref, route_ref, counts_ref, run_ref):
    @pl.when(pl.program_id(0) == 0)
    def _():
        run_ref[...] = jnp.zeros(run_ref.shape, F32)

    x = x_ref[...]
    d = x.shape[1]
    h = _rms(x, gmix_ref[...]).astype(BF16)
    gates = _sigmoid(_dot(h, wg_ref[...]) + bg_ref[...])
    mix = gates[:, :d] * _dot(ya_ref[...], wod_ref[...]) + gates[:, d:] * _dot(yb_ref[...], wos_ref[...])
    x2 = x + _dot(mix.astype(BF16), wout_ref[...])
    x2_ref[...] = x2.reshape(x2_ref.shape)
    h2 = _rms(x2, gffn_ref[...]).astype(BF16)
    logits = _dot(h2, wr_ref[...]) + br_ref[...]
    cls = _route_class(logits)
    lane = lax.broadcasted_iota(jnp.int32, logits.shape, 1).astype(F32)
    onehot = jnp.where(lane == cls, 1.0, 0.0)
    earlier = _dot(low_ref[...], onehot.astype(BF16)) + run_ref[...]
    rank = jnp.sum(earlier * onehot, axis=1, keepdims=True)
    route = jnp.where(lane == 0.0, cls, jnp.where(lane == 1.0, rank, 0.0))
    route_ref[...] = jnp.transpose(route)[:route_ref.shape[0], :]
    run = run_ref[...] + jnp.sum(onehot, axis=0, keepdims=True)
    run_ref[...] = run
    counts_ref[...] = run


def _tail(x2d, ya, yb, g_mix, wg, bg, wod, wos, wout, g_ffn, wr, br, tm):
    n, d = x2d.shape
    row = lambda i: (i, 0)
    const = lambda i: (0, 0)
    full = lambda a: pl.BlockSpec(a.shape, const)
    r = np.arange(tm)
    low =jnp.asarray((r[None, :] < r[:, None]).astype(np.float32), dtype=BF16)
    return pl.pallas_call(
        _tail_kernel,
        grid=(n // tm,),
        in_specs=[pl.BlockSpec((tm, d), row),
                  pl.BlockSpec((tm, COLS), row),
                  pl.BlockSpec((tm, COLS), row),
                  full(g_mix), full(wg), full(bg), full(wod), full(wos), full(wout),
                  full(g_ffn), full(wr), full(br), full(low)],
        out_specs=[pl.BlockSpec((tm, d // SLAB, SLAB), lambda i: (i, 0, 0)),
                   pl.BlockSpec((8, tm), lambda i: (0, i)),
                   pl.BlockSpec((1, ROUTER_LANES), const)],
        out_shape=[jax.ShapeDtypeStruct((n, d // SLAB, SLAB), F32),
                   jax.ShapeDtypeStruct((8, n), F32),
                   jax.ShapeDtypeStruct((1, ROUTER_LANES), F32)],
        scratch_shapes=[pltpu.VMEM((1, ROUTER_LANES), F32)],
        compiler_params=pltpu.CompilerParams(
            dimension_semantics=("arbitrary",), vmem_limit_bytes=VMEM_LIMIT),
        name="tail",
    )(x2d, ya, yb, g_mix, wg, bg, wod, wos, wout, g_ffn, wr, br, low)


def _class_experts_np():
    ea, eb = [], []
    for g in range(N_GROUPS):
        for a in range(EXPERTS_PER_GROUP):
            for b in range(a + 1, EXPERTS_PER_GROUP):
                ea.append(g * EXPERTS_PER_GROUP + a)
                eb.append(g * EXPERTS_PER_GROUP + b)
    return np.asarray(ea, np.int32), np.asarray(eb, np.int32)


def _dispatch_plan(route, counts, tm):
    n = route.shape[1]
    cls = route[0].astype(jnp.int32)
    rank = route[1].astype(jnp.int32)
    cnt = counts[0, :N_CLASSES].astype(jnp.int32)
    tiles = (cnt + tm - 1) // tm
    tile_end = jnp.cumsum(tiles)
    pos = (tile_end - tiles)[cls] * tm + rank
    n_tiles = n // tm + N_CLASSES
    t_idx = jnp.arange(n_tiles, dtype=jnp.int32)
    used = t_idx < tile_end[-1]
    tile_cls = jnp.sum((t_idx[:, None] >= tile_end[None, :]).astype(jnp.int32), axis=1)
    tile_cls = jnp.where(used, tile_cls, jnp.max(jnp.where(used, tile_cls, 0)))
    ea_np, eb_np = _class_experts_np()
    return pos, jnp.asarray(ea_np)[tile_cls], jnp.asarray(eb_np)[tile_cls], used.astype(jnp.int32)


def _token_copies(n_tok, start_one, wait_one):
    def issue(r, carry):
        start_one(r)
        return carry

    def drain(r, carry):
        wait_one()
        return carry

    lax.fori_loop(0, n_tok, issue, 0, unroll=8)
    lax.fori_loop(0, n_tok, drain, 0, unroll=8)


def _pos_spec(tm):
    return pl.BlockSpec((None, 1, tm), lambda i: (i, 0, 0), memory_space=pltpu.SMEM)


def _dispatch_kernel(pos_ref, x_ref, init_hbm, xs_hbm, sem):
    del init_hbm
    _token_copies(
        x_ref.shape[0],
        lambda r: pltpu.make_async_copy(x_ref.at[r], xs_hbm.at[pos_ref[0, r]], sem).start(),
        lambda: pltpu.make_async_copy(x_ref.at[0], xs_hbm.at[0], sem).wait())


def _dispatch(pos, x2, n_slots, tm):
    n = x2.shape[0]
    tile = x2.shape[1:]
    return pl.pallas_call(
        _dispatch_kernel,
        grid=(n // tm,),
        in_specs=[_pos_spec(tm),
                  pl.BlockSpec((tm,) + tile, lambda i: (i, 0, 0)),
                  pl.BlockSpec(memory_space=pl.ANY)],
        out_specs=pl.BlockSpec(memory_space=pl.ANY),
        out_shape=jax.ShapeDtypeStruct((n_slots,) + tile, F32),
        scratch_shapes=[pltpu.SemaphoreType.DMA],
        input_output_aliases={2: 0},
        compiler_params=pltpu.CompilerParams(dimension_semantics=("arbitrary",)),
        name="moe_dispatch",
    )(pos.reshape(n // tm, 1, tm), x2, jnp.zeros((n_slots,) + tile, F32))


def _combine_kernel(pos_ref, ys_hbm, y_ref, buf_ref, sem):
    _token_copies(
        buf_ref.shape[0],
        lambda r: pltpu.make_async_copy(ys_hbm.at[pos_ref[0, r]], buf_ref.at[r], sem).start(),
        lambda: pltpu.make_async_copy(ys_hbm.at[0], buf_ref.at[0], sem).wait())
    y_ref[...] = buf_ref[...].reshape(y_ref.shape)


def _combine(pos, ys, n, tm):
    tile = ys.shape[1:]
    d = tile[0] * tile[1]
    return pl.pallas_call(
        _combine_kernel,
        grid=(n // tm,),
        in_specs=[_pos_spec(tm), pl.BlockSpec(memory_space=pl.ANY)],
        out_specs=pl.BlockSpec((tm, d), lambda i: (i, 0)),
        out_shape=jax.ShapeDtypeStruct((n, d), F32),
        scratch_shapes=[pltpu.VMEM((tm,) + tile, F32), pltpu.SemaphoreType.DMA],
        compiler_params=pltpu.CompilerParams(dimension_semantics=("arbitrary",)),
        name="moe_combine",
    )(pos.reshape(n // tm, 1, tm), ys)


def _moe_kernel(ea_ref, eb_ref, used_ref, xs_ref, g_ref, wr_ref, br_ref,
                mga_ref, mua_ref, mda_ref, mgb_ref, mub_ref, mdb_ref, ys_ref):
    t = pl.program_id(0)

    @pl.when(used_ref[t] == 0)
    def _():
        ys_ref[...] = jnp.zeros(ys_ref.shape, F32)

    @pl.when(used_ref[t] != 0)
    def _():
        x = xs_ref[...].reshape(xs_ref.shape[0], -1)
        h2 = _rms(x, g_ref[...]).astype(BF16)
        logits = _dot(h2, wr_ref[...]) + br_ref[...]
        lane = lax.broadcasted_iota(jnp.int32, logits.shape, 1)
        pick = lambda e: jnp.sum(jnp.where(lane == N_GROUPS + e, logits, 0.0), axis=1, keepdims=True)
        la = pick(ea_ref[t])
        lb = pick(eb_ref[t])
        top = jnp.maximum(la, lb)
        pa = jnp.exp(la - top)
        pb = jnp.exp(lb - top)

        def expert(mg_ref, mu_ref, md_ref):
            g = _dot(h2, mg_ref[...])
            act = g * _sigmoid(g) * _dot(h2, mu_ref[...])
            return _dot(act.astype(BF16), md_ref[...])

        ya = expert(mga_ref, mua_ref, mda_ref)
        yb = expert(mgb_ref, mub_ref, mdb_ref)
        y = x + ((pa / (pa + pb)) * ya + (pb / (pa + pb)) * yb)
        ys_ref[...] = y.reshape(ys_ref.shape)


def _moe(ea, eb, used, xs, g_ffn, wr, br, mg, mu, md, tm):
    n_slots = xs.shape[0]
    tile = xs.shape[1:]
    d = tile[0] * tile[1]
    de = mg.shape[2]
    tok = pl.BlockSpec((tm,) + tile, lambda t, ea, eb, used: (t, 0, 0))
    const = lambda t, ea, eb, used: (0, 0)
    w_a = lambda shape: pl.BlockSpec((None,) + shape, lambda t, ea, eb, used: (ea[t], 0, 0))
    w_b = lambda shape: pl.BlockSpec((None,) + shape, lambda t, ea, eb, used: (eb[t], 0, 0))
    grid_spec = pltpu.PrefetchScalarGridSpec(
        num_scalar_prefetch=3,
        grid=(n_slots // tm,),
        in_specs=[tok,
                  pl.BlockSpec(g_ffn.shape, const),
                  pl.BlockSpec(wr.shape, const),
                  pl.BlockSpec(br.shape, const),
                  w_a((d, de)), w_a((d, de)), w_a((de, d)),
                  w_b((d, de)), w_b((d, de)), w_b((de, d))],
        out_specs=tok)
    return pl.pallas_call(
        _moe_kernel,
        grid_spec=grid_spec,
        out_shape=jax.ShapeDtypeStruct(xs.shape, F32),
        compiler_params=pltpu.CompilerParams(
            dimension_semantics=("arbitrary",), vmem_limit_bytes=VMEM_LIMIT),
        name="moe",
    )(ea, eb, used, xs, g_ffn, wr, br, mg, mu, md, mg, mu, md)


def _tile(n, pref):
    return pref if n % pref == 0 else n


def _feature_major(cache):
    b, p = cache.shape[:2]
    nd = cache.ndim
    return jnp.transpose(cache, (0,) + tuple(range(2, nd)) + (1,)).reshape(b, COLS, p)


def _token_major(xt, head_dims):
    b, _, t = xt.shape
    nd = len(head_dims)
    return jnp.transpose(xt.reshape((b,) + head_dims + (t,)), (0, nd + 1) + tuple(range(1, nd + 1)))


def kernel(x_prompt, x_sample, cache_diff_k, cache_diff_v, cache_sb_k, cache_sb_v, rel_bias_table, norm_mix_g, w_in, q_norm_g, k_norm_g, lambda_q1, lambda_k1, lambda_q2, lambda_k2, subln_g, w_o_diff, w_o_sb, w_branch_gate, b_branch_gate, w_out, norm_ffn_g, w_router_group, b_router_group, w_router_expert, b_router_expert, moe_w_gate, moe_w_up, moe_w_down):
    b, t, d = x_prompt.shape
    bs, ts, _ = x_sample.shape
    depth = norm_mix_g.shape[0]
    p = cache_diff_k.shape[2]
    tq = _tile(t, 512)
    tq_b = _tile(t, 256)
    tk_s = _tile(p, 256)
    tn = 128
    assert t % tq == 0 and t % tq_b == 0 and tq % CHUNK == 0 and p % CHUNK == 0
    assert ts <= CHUNK and ts <= tn
    assert p % tk_s == 0

    q_loc = np.arange(tq)
    bkt_prompt = np.concatenate([_bucket_tile_np(q_loc + tq, np.arange(tq) + tq),
                                 _bucket_tile_np(q_loc + tq, np.arange(tq))], axis=0)
    far = np.unique(_t5_bucket_np(-np.arange(tq + 1, 4 * tq)))
    assert far.size == 1
    far_bucket = int(far[0])
    q_s = p + np.arange(ts)
    k_s = np.concatenate([np.arange(p), p + np.arange(tn)])
    bkt_sample = _bucket_tile_np(q_s, k_s)
    bkt_sample[:, p + ts:] = -1
    bias_prompt = _bias_tiles(rel_bias_table, jnp.asarray(bkt_prompt)).reshape(H_A, 2, tq, tq)
    bias_sample = _bias_tiles(rel_bias_table, jnp.asarray(bkt_sample))
    bias_sc, bias_sn = bias_sample[:, :, :p], bias_sample[:, :, p:]

    gm = np.kron(np.eye(256 // DH), np.full((DH, DH), 1.0 / DH)).astype(np.float32)
    gm = jnp.asarray(gm, dtype=BF16)
    u_p = _upper_sum_matrix(tq_b)
    u_s = _upper_sum_matrix(tk_s)
    u_n = _upper_sum_matrix(tn)

    y_p = x_prompt
    y_s = x_sample.reshape(bs * ts, d)
    tm_proj = _tile(t, 512)
    tm_p = _tile(b * t, 512)
    tm_s = _tile(bs * ts, 128)
    tm_moe_p = _tile(b * t, 512)
    outs = [[] for _ in range(8)]
    for l in range(depth):
        lam_init = 0.8 - 0.6 * math.exp(-0.3 * l)
        w_in_b = w_in[l].astype(BF16)
        col = lambda g: w_in_b[:, g * COLS:(g + 1) * COLS]
        wn = jnp.concatenate([col(0), col(2), col(3)], axis=1)
        wt = jnp.concatenate([col(1), col(4), col(5)], axis=1).T
        g_mix = norm_mix_g[l].reshape(1, d)
        gq = jnp.tile(q_norm_g[l], COLS // DH).reshape(1, COLS)
        gk = jnp.tile(k_norm_g[l], COLS // DH).reshape(1, COLS)
        gkt = jnp.broadcast_to(gk.reshape(COLS, 1), (COLS, SLAB))
        lam_p = jnp.stack([lambda_q1[l], lambda_k1[l], lambda_q2[l], lambda_k2[l]])
        sg = subln_g[l].reshape(1, SLAB)
        wg = w_branch_gate[l].astype(BF16)
        bg = b_branch_gate[l].reshape(1, 2 * d)
        wod = w_o_diff[l].astype(BF16)
        wos = w_o_sb[l].astype(BF16)
        wout = w_out[l].astype(BF16)
        g_ffn = norm_ffn_g[l].reshape(1, d)
        n_r = N_GROUPS + N_EXPERTS
        wr = jnp.pad(jnp.concatenate([w_router_group[l], w_router_expert[l]], axis=1),
                     ((0, 0), (0, ROUTER_LANES - n_r)))
        wr = wr.astype(BF16)
        br = jnp.pad(jnp.concatenate([b_router_group[l], b_router_expert[l]]),
                     (0, ROUTER_LANES - n_r)).reshape(1, ROUTER_LANES)
        mg = moe_w_gate[l].astype(BF16)
        mu = moe_w_up[l].astype(BF16)
        md = moe_w_down[l].astype(BF16)

        def tail_and_moe(x2d, ya, yb, tm, tm_moe):
            n = x2d.shape[0]
            x2, route, counts = _tail(x2d, ya, yb, g_mix, wg, bg, wod, wos, wout, g_ffn, wr, br, tm)
            pos, ea, eb, used = _dispatch_plan(route, counts, tm_moe)
            xs = _dispatch(pos, x2, (n // tm_moe + N_CLASSES) * tm_moe, tm_moe)
            ys = _moe(ea, eb, used, xs, g_ffn, wr, br, mg, mu, md, tm_moe)
            return _combine(pos, ys, n, tm_moe)

        kat, va, kbt, vbt, qa_b, kat_b, va_b, qb_b, kbt_b, vbt_b = _project_t(
            y_p, g_mix, wn, wt, gq, gkt, gm, tm_proj)
        ya = _diff_prompt(rel_bias_table, qa_b, kat_b, va_b, bias_prompt, lam_p, sg,
                          tq, far_bucket, lam_init)
        yb = _sb_prompt(qb_b, kbt_b, vbt_b, u_p, tq_b)
        y_p = tail_and_moe(y_p.reshape(b * t, d), ya.reshape(b * t, COLS), yb.reshape(b * t, COLS),
                           tm_p, tm_moe_p).reshape(b, t, d)
        outs[0].append(_token_major(kat, (H_A, 2, DH)))
        outs[1].append(va)
        outs[2].append(_token_major(kbt, (H_B, DH)))
        outs[3].append(_token_major(vbt, (H_B, DH)))

        ka, va, kb, vb, qa_b, ka_b, va_b, qb_b, kb_b, vb_b = _project(y_s, g_mix, w_in_b, gq, gk, gm, tm_s)
        s3 = lambda a: a.reshape(bs, ts, COLS)
        padk = lambda a: jnp.pad(s3(a), ((0, 0), (0, tn - ts), (0, 0)))
        ya = _diff_sample(s3(qa_b), _feature_major(cache_diff_k[l]), cache_diff_v[l].reshape(bs, p, COLS),
                          padk(ka_b), padk(va_b), bias_sc, bias_sn, lam_p, sg, lam_init)
        yb = _sb_sample(s3(qb_b), _feature_major(cache_sb_k[l]), _feature_major(cache_sb_v[l]),
                        padk(kb_b), padk(vb_b), u_s, u_n, tk_s)
        y_s = tail_and_moe(y_s, ya.reshape(bs * ts, COLS), yb.reshape(bs * ts, COLS), tm_s, tm_s)
        outs[4].append(ka.reshape(bs, ts, H_A, 2, DH))
        outs[5].append(va.reshape(bs, ts, H_A, 2 * DH))
        outs[6].append(kb.reshape(bs, ts, H_B, DH))
        outs[7].append(vb.reshape(bs, ts, H_B, DH))

    return (y_p, y_s.reshape(bs, ts, d)) + tuple(jnp.stack(o) for o in outs)
```

```python
import functools
import math

import numpy as np
import jax
import jax.numpy as jnp
from jax import lax
from jax.experimental import pallas as pl
from jax.experimental.pallas import tpu as pltpu

F32 = jnp.float32
BF16 = jnp.bfloat16

EPS = 1e-6
CHUNK = 64
H_A = 4
DH = 64
H_B = 8
SLAB = 2 * DH
N_SLAB_A = H_A
N_SLAB_B = H_B // 2
COLS = H_A * SLAB
N_BUCKETS = 32
MAX_DISTANCE = 128
N_GROUPS = 4
EXPERTS_PER_GROUP = 4
N_EXPERTS = N_GROUPS * EXPERTS_PER_GROUP
N_PAIRS = EXPERTS_PER_GROUP * (EXPERTS_PER_GROUP - 1) // 2
N_CLASSES = N_GROUPS * N_PAIRS
ROUTER_LANES = 128
NEG = -1e30
SB_DEAD = -104.0
VMEM_LIMIT = 56 * 1024 * 1024
LOG2E = math.log2(math.e)
FAR_UNROLL = 8


def _rms(x, g):
    return x * lax.rsqrt(jnp.mean(x * x, axis=-1, keepdims=True) + EPS) * g


def _sigmoid(x):
    return 1.0 / (1.0 + jnp.exp(-x))


def _dot(a, b):
    return jnp.dot(a, b, preferred_element_type=F32)


def _dot_nt(a, b):
    return lax.dot_general(a, b, (((1,), (1,)), ((), ())), preferred_element_type=F32)


def _lane_tiles(x, n):
    return x if n == 1 else jnp.concatenate([x] * n, axis=1)


def _split_bf16(x):
    hi = x.astype(BF16)
    lo = (x - hi.astype(F32)).astype(BF16)
    return hi, lo


def _cols(j, width):
    return pl.ds(pl.multiple_of(j * width, width), width)


def _sub_head_norm(t, gain, gm):
    outs = []
    for s in range(0, COLS, 256):
        ts = t[:, s:s + 256]
        hi, lo = _split_bf16(ts * ts)
        outs.append(ts * lax.rsqrt(_dot(hi, gm) + _dot(lo, gm) + EPS))
    return jnp.concatenate(outs, axis=1) * gain


def _sub_head_norm_t(t, gain, gm):
    outs = []
    for s in range(0, COLS, 256):
        ts = t[s:s + 256, :]
        hi, lo = _split_bf16(ts * ts)
        outs.append(ts * lax.rsqrt(_dot(gm, hi) + _dot(gm, lo) + EPS))
    return jnp.concatenate(outs, axis=0) * gain


def _proj_kernel(x_ref, g_ref, w_ref, gq_ref, gk_ref, gm_ref,
                 ka_o, va_o, kb_o, vb_o, qa_b, ka_b, va_b, qb_b, kb_b, vb_b):
    h = _rms(x_ref[...], g_ref[...]).astype(BF16)
    proj = _dot(h, w_ref[...])
    gm = gm_ref[...]
    qa = _sub_head_norm(proj[:, 0 * COLS:1 * COLS], gq_ref[...], gm)
    ka = _sub_head_norm(proj[:, 1 * COLS:2 * COLS], gk_ref[...], gm)
    va = proj[:, 2 * COLS:3 * COLS]
    qb = proj[:, 3 * COLS:4 * COLS]
    kb = proj[:, 4 * COLS:5 * COLS]
    vb = proj[:, 5 * COLS:6 * COLS]
    ka_o[...] = ka
    va_o[...] = va
    kb_o[...] = kb
    vb_o[...] = vb
    scale = 1.0 / math.sqrt(DH)
    qa_b[...] = (qa * (scale * LOG2E)).astype(BF16)
    ka_b[...] = ka.astype(BF16)
    va_b[...] = va.astype(BF16)
    qb_b[...] = (qb * scale).astype(BF16)
    kb_b[...] = kb.astype(BF16)
    vb_b[...] = vb.astype(BF16)


def _project(x2d, g_mix, w_in_b, gq, gk, gm, tm):
    n, d = x2d.shape
    row = lambda i: (i, 0)
    const = lambda i: (0, 0)
    out_f = jax.ShapeDtypeStruct((n, COLS), F32)
    out_b = jax.ShapeDtypeStruct((n, COLS), BF16)
    blk = pl.BlockSpec((tm, COLS), row)
    return pl.pallas_call(
        _proj_kernel,
        grid=(n // tm,),
        in_specs=[pl.BlockSpec((tm, d), row),
                  pl.BlockSpec((1, d), const),
                  pl.BlockSpec(w_in_b.shape, const),
                  pl.BlockSpec((1, COLS), const),
                  pl.BlockSpec((1, COLS), const),
                  pl.BlockSpec(gm.shape, const)],
        out_specs=[blk] * 10,
        out_shape=[out_f] * 4 + [out_b] * 6,
        compiler_params=pltpu.CompilerParams(
            dimension_semantics=("arbitrary",), vmem_limit_bytes=VMEM_LIMIT),
        name="proj",
    )(x2d, g_mix, w_in_b, gq, gk, gm)


def _proj_t_kernel(x_ref, g_ref, wn_ref, wt_ref, gq_ref, gkt_ref, gm_ref,
                   kat_o, va_o, kbt_o, vbt_o, qa_b, kat_b, va_b, qb_b, kbt_b, vbt_b):
    h = _rms(x_ref[...], g_ref[...]).astype(BF16)
    tm = h.shape[0]
    nat = _dot(h, wn_ref[...])
    tr = _dot_nt(wt_ref[...], h)
    gm = gm_ref[...]
    qa = _sub_head_norm(nat[:, 0 * COLS:1 * COLS], gq_ref[...], gm)
    va = nat[:, 1 * COLS:2 * COLS]
    qb = nat[:, 2 * COLS:3 * COLS]
    kat = _sub_head_norm_t(tr[0 * COLS:1 * COLS], _lane_tiles(gkt_ref[...], tm // SLAB), gm)
    kbt = tr[1 * COLS:2 * COLS]
    vbt = tr[2 * COLS:3 * COLS]
    kat_o[...] = kat
    for hh in range(H_A):
        va_o[:, hh, :] = va[:, hh * SLAB:(hh + 1) * SLAB]
    kbt_o[...] = kbt
    vbt_o[...] = vbt
    scale = 1.0 / math.sqrt(DH)
    qa_b[...] = (qa * (scale * LOG2E)).astype(BF16)
    kat_b[...] = kat.astype(BF16)
    va_b[...] = va.astype(BF16)
    qb_b[...] = (qb * scale).astype(BF16)
    kbt_b[...] = kbt.astype(BF16)
    vbt_b[...] = vbt.astype(BF16)


def _project_t(x, g_mix, wn, wt, gq, gkt, gm, tm):
    b, t, d = x.shape
    const = lambda b, i: (0, 0)
    nat = pl.BlockSpec((None, tm, COLS), lambda b, i: (b, i, 0))
    fm = pl.BlockSpec((None, COLS, tm), lambda b, i: (b, 0, i))
    nat_s = lambda dt: jax.ShapeDtypeStruct((b, t, COLS), dt)
    fm_s = lambda dt: jax.ShapeDtypeStruct((b, COLS, t), dt)
    return pl.pallas_call(
        _proj_t_kernel,
        grid=(b, t // tm),
        in_specs=[pl.BlockSpec((None, tm, d), lambda b, i: (b, i, 0)),
                  pl.BlockSpec((1, d), const),
                  pl.BlockSpec(wn.shape, const),
                  pl.BlockSpec(wt.shape, const),
                  pl.BlockSpec((1, COLS), const),
                  pl.BlockSpec(gkt.shape, const),
                  pl.BlockSpec(gm.shape, const)],
        out_specs=[fm, pl.BlockSpec((None, tm, H_A, SLAB), lambda b, i: (b, i, 0, 0)), fm, fm,
                   nat, fm, nat, nat, fm, fm],
        out_shape=[fm_s(F32), jax.ShapeDtypeStruct((b, t, H_A, SLAB), F32), fm_s(F32), fm_s(F32),
                   nat_s(BF16), fm_s(BF16), nat_s(BF16), nat_s(BF16), fm_s(BF16), fm_s(BF16)],
        compiler_params=pltpu.CompilerParams(
            dimension_semantics=("arbitrary", "arbitrary"), vmem_limit_bytes=VMEM_LIMIT),
        name="proj_t",
    )(x, g_mix, wn, wt, gq, gkt, gm)


def _t5_bucket_np(rel):
    half = N_BUCKETS // 2
    max_exact = half // 2
    base = np.where(rel > 0, half, 0)
    n = np.abs(rel)
    nf = np.maximum(n, 1).astype(np.float64)
    large = max_exact + (np.log(nf / max_exact) / math.log(MAX_DISTANCE / max_exact)
                         * (half - max_exact)).astype(np.int32)
    large = np.minimum(large, half - 1)
    return (base + np.where(n < max_exact, n, large)).astype(np.int32)


def _bucket_tile_np(q_pos, k_pos):
    rel = k_pos[None, :] - q_pos[:, None]
    allowed = (k_pos[None, :] // CHUNK) <= (q_pos[:, None] // CHUNK)
    return np.where(allowed, _t5_bucket_np(rel), -1).astype(np.int32)


def _bias_kernel(tbl_ref, bkt_ref, o_ref):
    h = pl.program_id(0)
    bkt = bkt_ref[...]
    out = jnp.full(bkt.shape, NEG, F32)
    for b in range(N_BUCKETS):
        out = jnp.where(bkt == b, tbl_ref[b, h] * LOG2E, out)
    o_ref[...] = out


def _bias_tiles(rel_table, buckets):
    r, c = buckets.shape
    return pl.pallas_call(
        _bias_kernel,
        grid=(H_A,),
        in_specs=[pl.BlockSpec(memory_space=pltpu.SMEM),
                  pl.BlockSpec((r, c), lambda h: (0, 0))],
        out_specs=pl.BlockSpec((None, r, c), lambda h: (h, 0, 0)),
        out_shape=jax.ShapeDtypeStruct((H_A, r, c), F32),
        name="bias_tiles",
    )(rel_table, buckets)


def _two_maps(q):
    lane = lax.broadcasted_iota(jnp.int32, q.shape, 1)
    zero = jnp.zeros_like(q)
    return jnp.concatenate([jnp.where(lane < DH, q, zero), jnp.where(lane >= DH, q, zero)], axis=0)


def _diff_lambda(lam_ref, lam_init):
    p = lam_ref[...]
    s1 = jnp.sum(p[0:1] * p[1:2], axis=1, keepdims=True)
    s2 = jnp.sum(p[2:3] * p[3:4], axis=1, keepdims=True)
    return jnp.exp(s1) - jnp.exp(s2) + lam_init


def _diff_finish(acc, l, t, lam, sg, lam_init):
    o = acc[:t] / l[:t] - lam * (acc[t:] / l[t:])
    o = o * lax.rsqrt(jnp.mean(o * o, axis=-1, keepdims=True) + EPS) * sg
    return o * (1.0 - lam_init)


def _diff_prompt_kernel(tbl_ref, q_ref, kt_ref, v_ref, bias_ref, lam_ref, sg_ref, o_ref,
                        vx_ref, m_ref, l_ref, acc_ref, *, tq, tk, far_bucket, lam_init):
    h = pl.program_id(1)
    i = pl.program_id(2)

    @pl.when(i == 0)
    def _():
        vx_ref[:, :SLAB] = v_ref[...]
        vx_ref[:, SLAB:] = jnp.ones((vx_ref.shape[0], SLAB), vx_ref.dtype)

    qq = _two_maps(q_ref[...])
    m_ref[...] = jnp.full(m_ref.shape, NEG, F32)
    l_ref[...] = jnp.zeros(l_ref.shape, F32)
    acc_ref[...] = jnp.zeros(acc_ref.shape, F32)
    n_sub = tq // tk

    def update(jk, bias, first_row=0):
        n = tq - first_row
        segs = [(first_row, n), (tq + first_row, n)] if first_row else [(0, 2 * tq)]
        rd = lambda ref: jnp.concatenate([ref[a:a + k, :] for a, k in segs], axis=0)

        def wr(ref, val):
            for idx, (a, k) in enumerate(segs):
                ref[a:a + k, :] = val[idx * k:(idx + 1) * k]

        s = _dot(jnp.concatenate([qq[a:a + k] for a, k in segs], axis=0), kt_ref[:, _cols(jk, tk)])
        if bias is not None:
            s = s + jnp.concatenate([bias[first_row:], bias[first_row:]], axis=0)
        m_old = rd(m_ref)
        m_new = jnp.maximum(m_old, jnp.max(s, axis=1, keepdims=True))
        alpha = jnp.exp2(m_old - m_new)
        p = jnp.exp2(s - _lane_tiles(m_new, tk // SLAB))
        pv = _dot(p.astype(BF16), vx_ref[_cols(jk, tk), :])
        wr(l_ref, alpha * rd(l_ref) + pv[:, SLAB:])
        wr(acc_ref, alpha * rd(acc_ref) + pv[:, :SLAB])
        wr(m_ref, m_new)

    n_far = jnp.maximum(i - 1, 0)

    def far_tiles(first, count):
        for sub in range(count * n_sub):
            update(first * n_sub + sub, None)

    def far_body(j, carry):
        far_tiles(j * FAR_UNROLL, FAR_UNROLL)
        return carry

    lax.fori_loop(0, n_far // FAR_UNROLL, far_body, 0)
    done = n_far - n_far % FAR_UNROLL
    step = FAR_UNROLL // 2
    while step >= 1:
        take = (n_far % (2 * step)) >= step

        @pl.when(take)
        def _(done=done, step=step):
            far_tiles(done, step)

        done = done + jnp.where(take, step, 0)
        step //= 2
    m_ref[...] = m_ref[...] + tbl_ref[far_bucket, h] * LOG2E

    def biased_tile(j, kind):
        for sub in range(n_sub):
            first_row = sub * tk if kind == 0 else 0
            update(j * n_sub + sub, bias_ref[kind, :, sub * tk:(sub + 1) * tk], first_row)

    @pl.when(i >= 1)
    def _():
        biased_tile(i - 1, 1)
        biased_tile(i, 0)

    @pl.when(i == 0)
    def _():
        biased_tile(i, 0)

    lam = _diff_lambda(lam_ref, lam_init)
    o = _diff_finish(acc_ref[...], l_ref[...], tq, lam, sg_ref[...], lam_init)
    o_ref[...] = o.astype(o_ref.dtype)


def _diff_prompt(rel_table, qa, kat, va, bias, lam_p, sg, tq, far_bucket, lam_init):
    b, t, _ = qa.shape
    kern = functools.partial(_diff_prompt_kernel, tq=tq, tk=_tile(tq, 256), far_bucket=far_bucket,
                             lam_init=lam_init)
    return pl.pallas_call(
        kern,
        grid=(b, H_A, t // tq),
        in_specs=[pl.BlockSpec(memory_space=pltpu.SMEM),
                  pl.BlockSpec((None, tq, SLAB), lambda b, h, i: (b, i, h)),
                  pl.BlockSpec((None, SLAB, t), lambda b, h, i: (b, h, 0)),
                  pl.BlockSpec((None, t, SLAB), lambda b, h, i: (b, 0, h)),
                  pl.BlockSpec((None, 2, tq, tq), lambda b, h, i: (h, 0, 0, 0)),
                  pl.BlockSpec(lam_p.shape, lambda b, h, i: (0, 0)),
                  pl.BlockSpec((1, SLAB), lambda b, h, i: (0, 0))],
        out_specs=pl.BlockSpec((None, tq, SLAB), lambda b, h, i: (b, i, h)),
        out_shape=jax.ShapeDtypeStruct((b, t, COLS), BF16),
        scratch_shapes=[pltpu.VMEM((t, 2 * SLAB), BF16),
                        pltpu.VMEM((2 * tq, SLAB), F32),
                        pltpu.VMEM((2 * tq, SLAB), F32),
                        pltpu.VMEM((2 * tq, SLAB), F32)],
        compiler_params=pltpu.CompilerParams(
            dimension_semantics=("arbitrary", "arbitrary", "arbitrary"),
            vmem_limit_bytes=VMEM_LIMIT),
        name="diff_prompt",
    )(rel_table, qa, kat, va, bias, lam_p, sg)


def _diff_sample_kernel(q_ref, kct_ref, vc_ref, kn_ref, vn_ref, bc_ref, bn_ref, lam_ref, sg_ref,
                        o_ref, *, lam_init):
    qq = _two_maps(q_ref[...])
    t = q_ref.shape[0]
    bc = bc_ref[...]
    bn = bn_ref[...]
    s_c = _dot(qq, kct_ref[...].astype(BF16)) + jnp.concatenate([bc, bc], axis=0)
    s_n = _dot_nt(qq, kn_ref[...]) + jnp.concatenate([bn, bn], axis=0)
    m = jnp.maximum(jnp.max(s_c, axis=1, keepdims=True), jnp.max(s_n, axis=1, keepdims=True))
    p_c = jnp.exp2(s_c - m)
    p_n = jnp.exp2(s_n - m)
    l = jnp.sum(p_c, axis=1, keepdims=True) + jnp.sum(p_n, axis=1, keepdims=True)
    acc = _dot(p_c.astype(BF16), vc_ref[...].astype(BF16)) + _dot(p_n.astype(BF16), vn_ref[...])
    lam = _diff_lambda(lam_ref, lam_init)
    o = _diff_finish(acc, l, t, lam, sg_ref[...], lam_init)
    o_ref[...] = o.astype(o_ref.dtype)


def _diff_sample(qa, kct, vc, kn, vn, bias_c, bias_n, lam_p, sg, lam_init):
    b, t, _ = qa.shape
    p = vc.shape[1]
    tn = kn.shape[1]
    slab = lambda rows: pl.BlockSpec((None, rows, SLAB), lambda b, h: (b, 0, h))
    return pl.pallas_call(
        functools.partial(_diff_sample_kernel, lam_init=lam_init),
        grid=(b, H_A),
        in_specs=[slab(t),
                  pl.BlockSpec((None, SLAB, p), lambda b, h: (b, h, 0)),
                  slab(p), slab(tn), slab(tn),
                  pl.BlockSpec((None, t, p), lambda b, h: (h, 0, 0)),
                  pl.BlockSpec((None, t, tn), lambda b, h: (h, 0, 0)),
                  pl.BlockSpec(lam_p.shape, lambda b, h: (0, 0)),
                  pl.BlockSpec((1, SLAB), lambda b, h: (0, 0))],
        out_specs=slab(t),
        out_shape=jax.ShapeDtypeStruct((b, t, COLS), BF16),
        compiler_params=pltpu.CompilerParams(
            dimension_semantics=("arbitrary", "arbitrary"), vmem_limit_bytes=VMEM_LIMIT),
        name="diff_sample",
    )(qa, kct, vc, kn, vn, bias_c, bias_n, lam_p, sg)


def _sb_weights(z, u, c, allowed):
    sp = jnp.log(1.0 + jnp.exp(-jnp.abs(z)))
    log_beta = jnp.minimum(z, 0.0) - sp
    log_keep = -jnp.maximum(z, 0.0) - sp
    if allowed is not None:
        log_keep = jnp.where(allowed, log_keep, 0.0)
    hi, lo = _split_bf16(log_keep)
    later = _dot(hi, u) + _dot(lo, u)
    a = jnp.exp(log_beta + later + _lane_tiles(c, z.shape[1] // SLAB))
    if allowed is not None:
        a = jnp.where(allowed, a, 0.0)
    return a.astype(BF16), c + jnp.sum(log_keep, axis=1, keepdims=True)


def _strictly_causal_two_heads(t, n_keys):
    row = lax.broadcasted_iota(jnp.int32, (2 * t, n_keys), 0)
    col = lax.broadcasted_iota(jnp.int32, (2 * t, n_keys), 1)
    return col < jnp.where(row >= t, row - t, row)


def _merge_heads(acc, t):
    lane = lax.broadcasted_iota(jnp.int32, (t, SLAB), 1)
    return jnp.where(lane < DH, acc[:t], acc[t:])


def _sb_sweep_left(j0, c, acc, tile):
    def alive(c):
        return (jnp.max(c) > SB_DEAD).astype(jnp.int32)

    def cond(st):
        j, live, _, _ = st
        return (j >= 0) & (live > 0)

    def body(st):
        j, _, c, acc = st
        c, acc = tile(j, c, acc)
        return j - 1, alive(c), c, acc

    _, _, _, acc = lax.while_loop(cond, body, (j0, alive(c), c, acc))
    return acc


def _sb_prompt_kernel(q_ref, kt_ref, vt_ref, u_ref, o_ref, *, tq):
    n_q = q_ref.shape[0] // tq
    u = u_ref[...]
    mask = _strictly_causal_two_heads(tq, tq)
    zeros = jnp.zeros((2 * tq, SLAB), F32)

    def make_tile(qq):
        def tile(j, c, acc, mask=None, valid=None):
            a, c_new = _sb_weights(_dot(qq, kt_ref[:, _cols(j, tq)]), u, c, mask)
            if valid is not None:
                a = jnp.where(valid, a, jnp.zeros_like(a))
                c_new = jnp.where(valid, c_new, c)
            return c_new, acc + _dot_nt(a, vt_ref[:, _cols(j, tq)])
        return tile

    state = []
    for s in range(n_q):
        i = pl.program_id(2) * n_q + s
        tile = make_tile(_two_maps(q_ref[s * tq:(s + 1) * tq, :]))
        c, acc = tile(i, zeros, zeros, mask=mask)
        c, acc = tile(jnp.maximum(i - 1, 0), c, acc, valid=i >= 1)
        state.append((i, tile, c, acc))
    for s, (i, tile, c, acc) in enumerate(state):
        acc = _sb_sweep_left(i - 2, c, acc, tile)
        o_ref[s * tq:(s + 1) * tq, :] = _merge_heads(acc, tq).astype(o_ref.dtype)


def _sb_prompt(qb, kbt, vbt, u, tq):
    b, t, _ = qb.shape
    tg = _tile(t, 4 * tq)
    return pl.pallas_call(
        functools.partial(_sb_prompt_kernel, tq=tq),
        grid=(b, N_SLAB_B, t // tg),
        in_specs=[pl.BlockSpec((None, tg, SLAB), lambda b, h, i: (b, i, h)),
                  pl.BlockSpec((None, SLAB, t), lambda b, h, i: (b, h, 0)),
                  pl.BlockSpec((None, SLAB, t), lambda b, h, i: (b, h, 0)),
                  pl.BlockSpec(u.shape, lambda b, h, i: (0, 0))],
        out_specs=pl.BlockSpec((None, tg, SLAB), lambda b, h, i: (b, i, h)),
        out_shape=jax.ShapeDtypeStruct((b, t, COLS), BF16),
        compiler_params=pltpu.CompilerParams(
            dimension_semantics=("arbitrary", "arbitrary", "arbitrary"),
            vmem_limit_bytes=VMEM_LIMIT),
        name="sb_prompt",
    )(qb, kbt, vbt, u)


def _sb_sample_kernel(q_ref, kct_ref, vct_ref, kn_ref, vn_ref, u_ref, un_ref, o_ref, *, tk):
    t = q_ref.shape[0]
    tn = kn_ref.shape[0]
    n_tiles = kct_ref.shape[1] // tk
    qq = _two_maps(q_ref[...])
    u = u_ref[...]
    zeros = jnp.zeros((2 * t, SLAB), F32)
    a, c = _sb_weights(_dot_nt(qq, kn_ref[...]), un_ref[...], zeros,
                       _strictly_causal_two_heads(t, tn))
    acc = _dot(a, vn_ref[...])

    def tile(j, c, acc):
        a, c = _sb_weights(_dot(qq, kct_ref[:, _cols(j, tk)].astype(BF16)), u, c, None)
        return c, acc + _dot_nt(a, vct_ref[:, _cols(j, tk)].astype(BF16))

    acc = _sb_sweep_left(n_tiles - 1, c, acc, tile)
    o_ref[...] = _merge_heads(acc, t).astype(o_ref.dtype)


def _sb_sample(qb, kct, vct, kn, vn, u, un, tk):
    b, t, _ = qb.shape
    p = kct.shape[2]
    tn = kn.shape[1]
    slab = lambda rows: pl.BlockSpec((None, rows, SLAB), lambda b, h: (b, 0, h))
    slab_t = pl.BlockSpec((None, SLAB, p), lambda b, h: (b, h, 0))
    return pl.pallas_call(
        functools.partial(_sb_sample_kernel, tk=tk),
        grid=(b, N_SLAB_B),
        in_specs=[slab(t), slab_t, slab_t, slab(tn), slab(tn),
                  pl.BlockSpec(u.shape, lambda b, h: (0, 0)),
                  pl.BlockSpec(un.shape, lambda b, h: (0, 0))],
        out_specs=slab(t),
        out_shape=jax.ShapeDtypeStruct((b, t, COLS), BF16),
        compiler_params=pltpu.CompilerParams(
            dimension_semantics=("arbitrary", "arbitrary"), vmem_limit_bytes=VMEM_LIMIT),
        name="sb_sample",
    )(qb, kct, vct, kn, vn, u, un)


def _upper_sum_matrix(n):
    j = np.arange(n)[:, None]
    s = np.arange(n)[None, :]
    return jnp.asarray((j > s).astype(np.float32), dtype=BF16)


def _route_class(logits):
    lane = lax.broadcasted_iota(jnp.int32, logits.shape, 1).astype(F32)
    big = float(ROUTER_LANES)

    def first_argmax(vals):
        top = jnp.max(vals, axis=1, keepdims=True)
        return jnp.min(jnp.where(vals == top, lane, big), axis=1, keepdims=True)

    g_sel = first_argmax(jnp.where(lane < N_GROUPS, logits, NEG))
    first = N_GROUPS + g_sel * EXPERTS_PER_GROUP
    in_group = (lane >= first) & (lane < first + EXPERTS_PER_GROUP)
    el = jnp.where(in_group, logits, NEG)
    i1 = first_argmax(el)
    i2 = first_argmax(jnp.where(lane == i1, NEG, el))
    a = jnp.minimum(i1, i2) - first
    b = jnp.maximum(i1, i2) - first
    pair = a * (2 * EXPERTS_PER_GROUP - 1 - a) * 0.5 + (b - a - 1.0)
    return g_sel * N_PAIRS + pair


def _tail_kernel(x_ref, ya_ref, yb_ref, gmix_ref, wg_ref, bg_ref, wod_ref, wos_ref, wout_ref,
                 gffn_ref, wr_ref, br_ref, low_ref, x2_ref, route_ref, counts_ref, run_ref):
    @pl.when(pl.program_id(0) == 0)
    def _():
        run_ref[...] = jnp.zeros(run_ref.shape, F32)

    x = x_ref[...]
    d = x.shape[1]
    h = _rms(x, gmix_ref[...]).astype(BF16)
    gates = _sigmoid(_dot(h, wg_ref[...]) + bg_ref[...])
    mix = gates[:, :d] * _dot(ya_ref[...], wod_ref[...]) + gates[:, d:] * _dot(yb_ref[...], wos_ref[...])
    x2 = x + _dot(mix.astype(BF16), wout_ref[...])
    x2_ref[...] = x2.reshape(x2_ref.shape)
    h2 = _rms(x2, gffn_ref[...]).astype(BF16)
    logits = _dot(h2, wr_ref[...]) + br_ref[...]
    cls = _route_class(logits)
    lane = lax.broadcasted_iota(jnp.int32, logits.shape, 1).astype(F32)
    onehot = jnp.where(lane == cls, 1.0, 0.0)
    earlier = _dot(low_ref[...], onehot.astype(BF16)) + run_ref[...]
    rank = jnp.sum(earlier * onehot, axis=1, keepdims=True)
    route = jnp.where(lane == 0.0, cls, jnp.where(lane == 1.0, rank, 0.0))
    route_ref[...] = jnp.transpose(route)[:route_ref.shape[0], :]
    run = run_ref[...] + jnp.sum(onehot, axis=0, keepdims=True)
    run_ref[...] = run
    counts_ref[...] = run


def _tail(x2d, ya, yb, g_mix, wg, bg, wod, wos, wout, g_ffn, wr, br, tm):
    n, d = x2d.shape
    row = lambda i: (i, 0)
    const = lambda i: (0, 0)
    full = lambda a: pl.BlockSpec(a.shape, const)
    r = np.arange(tm)
    low =jnp.asarray((r[None, :] < r[:, None]).astype(np.float32), dtype=BF16)
    return pl.pallas_call(
        _tail_kernel,
        grid=(n // tm,),
        in_specs=[pl.BlockSpec((tm, d), row),
                  pl.BlockSpec((tm, COLS), row),
                  pl.BlockSpec((tm, COLS), row),
                  full(g_mix), full(wg), full(bg), full(wod), full(wos), full(wout),
                  full(g_ffn), full(wr), full(br), full(low)],
        out_specs=[pl.BlockSpec((tm, d // SLAB, SLAB), lambda i: (i, 0, 0)),
                   pl.BlockSpec((8, tm), lambda i: (0, i)),
                   pl.BlockSpec((1, ROUTER_LANES), const)],
        out_shape=[jax.ShapeDtypeStruct((n, d // SLAB, SLAB), F32),
                   jax.ShapeDtypeStruct((8, n), F32),
                   jax.ShapeDtypeStruct((1, ROUTER_LANES), F32)],
        scratch_shapes=[pltpu.VMEM((1, ROUTER_LANES), F32)],
        compiler_params=pltpu.CompilerParams(
            dimension_semantics=("arbitrary",), vmem_limit_bytes=VMEM_LIMIT),
        name="tail",
    )(x2d, ya, yb, g_mix, wg, bg, wod, wos, wout, g_ffn, wr, br, low)


def _class_experts_np():
    ea, eb = [], []
    for g in range(N_GROUPS):
        for a in range(EXPERTS_PER_GROUP):
            for b in range(a + 1, EXPERTS_PER_GROUP):
                ea.append(g * EXPERTS_PER_GROUP + a)
                eb.append(g * EXPERTS_PER_GROUP + b)
    return np.asarray(ea, np.int32), np.asarray(eb, np.int32)


def _dispatch_plan(route, counts, tm):
    n = route.shape[1]
    cls = route[0].astype(jnp.int32)
    rank = route[1].astype(jnp.int32)
    cnt = counts[0, :N_CLASSES].astype(jnp.int32)
    tiles = (cnt + tm - 1) // tm
    tile_end = jnp.cumsum(tiles)
    pos = (tile_end - tiles)[cls] * tm + rank
    n_tiles = n // tm + N_CLASSES
    t_idx = jnp.arange(n_tiles, dtype=jnp.int32)
    used = t_idx < tile_end[-1]
    tile_cls = jnp.sum((t_idx[:, None] >= tile_end[None, :]).astype(jnp.int32), axis=1)
    tile_cls = jnp.where(used, tile_cls, jnp.max(jnp.where(used, tile_cls, 0)))
    ea_np, eb_np = _class_experts_np()
    return pos, jnp.asarray(ea_np)[tile_cls], jnp.asarray(eb_np)[tile_cls], used.astype(jnp.int32)


COPY_UNROLL = 8


def _token_copies(n_tok, copy_of, wait_one):
    def issue(g, carry):
        for s in range(COPY_UNROLL):
            copy_of(g * COPY_UNROLL + s).start(priority=s % 2)
        return carry

    def drain(g, carry):
        for _ in range(COPY_UNROLL):
            wait_one()
        return carry

    lax.fori_loop(0, n_tok // COPY_UNROLL, issue, 0)
    lax.fori_loop(0, n_tok // COPY_UNROLL, drain, 0)


def _pos_spec(tm):
    return pl.BlockSpec((None, 1, tm), lambda i: (i, 0, 0), memory_space=pltpu.SMEM)


def _dispatch_kernel(pos_ref, x_ref, init_hbm, xs_hbm, sem):
    del init_hbm
    _token_copies(
        x_ref.shape[0],
        lambda r: pltpu.make_async_copy(x_ref.at[r], xs_hbm.at[pos_ref[0, r]], sem),
        lambda: pltpu.make_async_copy(x_ref.at[0], xs_hbm.at[0], sem).wait())


def _dispatch(pos, x2, n_slots, tm):
    n = x2.shape[0]
    tile = x2.shape[1:]
    return pl.pallas_call(
        _dispatch_kernel,
        grid=(n // tm,),
        in_specs=[_pos_spec(tm),
                  pl.BlockSpec((tm,) + tile, lambda i: (i, 0, 0)),
                  pl.BlockSpec(memory_space=pl.ANY)],
        out_specs=pl.BlockSpec(memory_space=pl.ANY),
        out_shape=jax.ShapeDtypeStruct((n_slots,) + tile, F32),
        scratch_shapes=[pltpu.SemaphoreType.DMA],
        input_output_aliases={2: 0},
        compiler_params=pltpu.CompilerParams(dimension_semantics=("arbitrary",)),
        name="moe_dispatch",
    )(pos.reshape(n // tm, 1, tm), x2, jnp.zeros((n_slots,) + tile, F32))


def _combine_kernel(pos_ref, ys_hbm, y_ref, buf_ref, sem):
    _token_copies(
        buf_ref.shape[0],
        lambda r: pltpu.make_async_copy(ys_hbm.at[pos_ref[0, r]], buf_ref.at[r], sem),
        lambda: pltpu.make_async_copy(ys_hbm.at[0], buf_ref.at[0], sem).wait())
    y_ref[...] = buf_ref[...].reshape(y_ref.shape)


def _combine(pos, ys, n, tm):
    tile = ys.shape[1:]
    d = tile[0] * tile[1]
    return pl.pallas_call(
        _combine_kernel,
        grid=(n // tm,),
        in_specs=[_pos_spec(tm), pl.BlockSpec(memory_space=pl.ANY)],
        out_specs=pl.BlockSpec((tm, d), lambda i: (i, 0)),
        out_shape=jax.ShapeDtypeStruct((n, d), F32),
        scratch_shapes=[pltpu.VMEM((tm,) + tile, F32), pltpu.SemaphoreType.DMA],
        compiler_params=pltpu.CompilerParams(dimension_semantics=("arbitrary",)),
        name="moe_combine",
    )(pos.reshape(n // tm, 1, tm), ys)


def _moe_kernel(ea_ref, eb_ref, used_ref, xs_ref, g_ref, wr_ref, br_ref,
                mga_ref, mua_ref, mda_ref, mgb_ref, mub_ref, mdb_ref, ys_ref):
    t = pl.program_id(0)

    @pl.when(used_ref[t] == 0)
    def _():
        ys_ref[...] = jnp.zeros(ys_ref.shape, F32)

    @pl.when(used_ref[t] != 0)
    def _():
        x = xs_ref[...].reshape(xs_ref.shape[0], -1)
        h2 = _rms(x, g_ref[...]).astype(BF16)
        logits = _dot(h2, wr_ref[...]) + br_ref[...]
        lane = lax.broadcasted_iota(jnp.int32, logits.shape, 1)
        pick = lambda e: jnp.sum(jnp.where(lane == N_GROUPS + e, logits, 0.0), axis=1, keepdims=True)
        la = pick(ea_ref[t])
        lb = pick(eb_ref[t])
        top = jnp.maximum(la, lb)
        pa = jnp.exp(la - top)
        pb = jnp.exp(lb - top)

        def expert(mg_ref, mu_ref, md_ref):
            g = _dot(h2, mg_ref[...])
            act = g * _sigmoid(g) * _dot(h2, mu_ref[...])
            return _dot(act.astype(BF16), md_ref[...])

        ya = expert(mga_ref, mua_ref, mda_ref)
        yb = expert(mgb_ref, mub_ref, mdb_ref)
        y = x + ((pa / (pa + pb)) * ya + (pb / (pa + pb)) * yb)
        ys_ref[...] = y.reshape(ys_ref.shape)


def _moe(ea, eb, used, xs, g_ffn, wr, br, mg, mu, md, tm):
    n_slots = xs.shape[0]
    tile = xs.shape[1:]
    d = tile[0] * tile[1]
    de = mg.shape[2]
    tok = pl.BlockSpec((tm,) + tile, lambda t, ea, eb, used: (t, 0, 0))
    const = lambda t, ea, eb, used: (0, 0)
    w_a = lambda shape: pl.BlockSpec((None,) + shape, lambda t, ea, eb, used: (ea[t], 0, 0))
    w_b = lambda shape: pl.BlockSpec((None,) + shape, lambda t, ea, eb, used: (eb[t], 0, 0))
    grid_spec = pltpu.PrefetchScalarGridSpec(
        num_scalar_prefetch=3,
        grid=(n_slots // tm,),
        in_specs=[tok,
                  pl.BlockSpec(g_ffn.shape, const),
                  pl.BlockSpec(wr.shape, const),
                  pl.BlockSpec(br.shape, const),
                  w_a((d, de)), w_a((d, de)), w_a((de, d)),
                  w_b((d, de)), w_b((d, de)), w_b((de, d))],
        out_specs=tok)
    return pl.pallas_call(
        _moe_kernel,
        grid_spec=grid_spec,
        out_shape=jax.ShapeDtypeStruct(xs.shape, F32),
        compiler_params=pltpu.CompilerParams(
            dimension_semantics=("arbitrary",), vmem_limit_bytes=VMEM_LIMIT),
        name="moe",
    )(ea, eb, used, xs, g_ffn, wr, br, mg, mu, md, mg, mu, md)


def _tile(n, pref):
    return pref if n % pref == 0 else n


def _feature_major(cache):
    b, p = cache.shape[:2]
    nd = cache.ndim
    return jnp.transpose(cache, (0,) + tuple(range(2, nd)) + (1,)).reshape(b, COLS, p)


def _token_major(xt, head_dims):
    b, _, t = xt.shape
    nd = len(head_dims)
    return jnp.transpose(xt.reshape((b,) + head_dims + (t,)), (0, nd + 1) + tuple(range(1, nd + 1)))


def kernel(x_prompt, x_sample, cache_diff_k, cache_diff_v, cache_sb_k, cache_sb_v, rel_bias_table, norm_mix_g, w_in, q_norm_g, k_norm_g, lambda_q1, lambda_k1, lambda_q2, lambda_k2, subln_g, w_o_diff, w_o_sb, w_branch_gate, b_branch_gate, w_out, norm_ffn_g, w_router_group, b_router_group, w_router_expert, b_router_expert, moe_w_gate, moe_w_up, moe_w_down):
    b, t, d = x_prompt.shape
    bs, ts, _ = x_sample.shape
    depth = norm_mix_g.shape[0]
    p = cache_diff_k.shape[2]
    tq = _tile(t, 512)
    tq_b = _tile(t, 256)
    tk_s = _tile(p, 256)
    tn = 128
    assert t % tq == 0 and t % tq_b == 0 and tq % CHUNK == 0 and p % CHUNK == 0
    assert ts <= CHUNK and ts <= tn
    assert p % tk_s == 0

    q_loc = np.arange(tq)
    bkt_prompt = np.concatenate([_bucket_tile_np(q_loc + tq, np.arange(tq) + tq),
                                 _bucket_tile_np(q_loc + tq, np.arange(tq))], axis=0)
    far = np.unique(_t5_bucket_np(-np.arange(tq + 1, 4 * tq)))
    assert far.size == 1
    far_bucket = int(far[0])
    q_s = p + np.arange(ts)
    k_s = np.concatenate([np.arange(p), p + np.arange(tn)])
    bkt_sample = _bucket_tile_np(q_s, k_s)
    bkt_sample[:, p + ts:] = -1
    bias_prompt = _bias_tiles(rel_bias_table, jnp.asarray(bkt_prompt)).reshape(H_A, 2, tq, tq)
    bias_sample = _bias_tiles(rel_bias_table, jnp.asarray(bkt_sample))
    bias_sc, bias_sn = bias_sample[:, :, :p], bias_sample[:, :, p:]

    gm = np.kron(np.eye(256 // DH), np.full((DH, DH), 1.0 / DH)).astype(np.float32)
    gm = jnp.asarray(gm, dtype=BF16)
    u_p = _upper_sum_matrix(tq_b)
    u_s = _upper_sum_matrix(tk_s)
    u_n = _upper_sum_matrix(tn)

    y_p = x_prompt
    y_s = x_sample.reshape(bs * ts, d)
    tm_proj = _tile(t, 512)
    tm_p = _tile(b * t, 512)
    tm_s = _tile(bs * ts, 128)
    tm_moe_p = _tile(b * t, 512)
    outs = [[] for _ in range(8)]
    for l in range(depth):
        lam_init = 0.8 - 0.6 * math.exp(-0.3 * l)
        w_in_b = w_in[l].astype(BF16)
        col = lambda g: w_in_b[:, g * COLS:(g + 1) * COLS]
        wn = jnp.concatenate([col(0), col(2), col(3)], axis=1)
        wt = jnp.concatenate([col(1), col(4), col(5)], axis=1).T
        g_mix = norm_mix_g[l].reshape(1, d)
        gq = jnp.tile(q_norm_g[l], COLS // DH).reshape(1, COLS)
        gk = jnp.tile(k_norm_g[l], COLS // DH).reshape(1, COLS)
        gkt = jnp.broadcast_to(gk.reshape(COLS, 1), (COLS, SLAB))
        lam_p = jnp.stack([lambda_q1[l], lambda_k1[l], lambda_q2[l], lambda_k2[l]])
        sg = subln_g[l].reshape(1, SLAB)
        wg = w_branch_gate[l].astype(BF16)
        bg = b_branch_gate[l].reshape(1, 2 * d)
        wod = w_o_diff[l].astype(BF16)
        wos = w_o_sb[l].astype(BF16)
        wout = w_out[l].astype(BF16)
        g_ffn = norm_ffn_g[l].reshape(1, d)
        n_r = N_GROUPS + N_EXPERTS
        wr = jnp.pad(jnp.concatenate([w_router_group[l], w_router_expert[l]], axis=1),
                     ((0, 0), (0, ROUTER_LANES - n_r)))
        wr = wr.astype(BF16)
        br = jnp.pad(jnp.concatenate([b_router_group[l], b_router_expert[l]]),
                     (0, ROUTER_LANES - n_r)).reshape(1, ROUTER_LANES)
        mg = moe_w_gate[l].astype(BF16)
        mu = moe_w_up[l].astype(BF16)
        md = moe_w_down[l].astype(BF16)

        def tail_and_moe(x2d, ya, yb, tm, tm_moe):
            n = x2d.shape[0]
            x2, route, counts = _tail(x2d, ya, yb, g_mix, wg, bg, wod, wos, wout, g_ffn, wr, br, tm)
            pos, ea, eb, used = _dispatch_plan(route, counts, tm_moe)
            xs = _dispatch(pos, x2, (n // tm_moe + N_CLASSES) * tm_moe, tm_moe)
            ys = _moe(ea, eb, used, xs, g_ffn, wr, br, mg, mu, md, tm_moe)
            return _combine(pos, ys, n, tm_moe)

        kat, va, kbt, vbt, qa_b, kat_b, va_b, qb_b, kbt_b, vbt_b = _project_t(
            y_p, g_mix, wn, wt, gq, gkt, gm, tm_proj)
        ya = _diff_prompt(rel_bias_table, qa_b, kat_b, va_b, bias_prompt, lam_p, sg,
                          tq, far_bucket, lam_init)
        yb = _sb_prompt(qb_b, kbt_b, vbt_b, u_p, tq_b)
        y_p = tail_and_moe(y_p.reshape(b * t, d), ya.reshape(b * t, COLS), yb.reshape(b * t, COLS),
                           tm_p, tm_moe_p).reshape(b, t, d)
        outs[0].append(_token_major(kat, (H_A, 2, DH)))
        outs[1].append(va)
        outs[2].append(_token_major(kbt, (H_B, DH)))
        outs[3].append(_token_major(vbt, (H_B, DH)))

        ka, va, kb, vb, qa_b, ka_b, va_b, qb_b, kb_b, vb_b = _project(y_s, g_mix, w_in_b, gq, gk, gm, tm_s)
        s3 = lambda a: a.reshape(bs, ts, COLS)
        padk = lambda a: jnp.pad(s3(a), ((0, 0), (0, tn - ts), (0, 0)))
        ya = _diff_sample(s3(qa_b), _feature_major(cache_diff_k[l]), cache_diff_v[l].reshape(bs, p, COLS),
                          padk(ka_b), padk(va_b), bias_sc, bias_sn, lam_p, sg, lam_init)
        yb = _sb_sample(s3(qb_b), _feature_major(cache_sb_k[l]), _feature_major(cache_sb_v[l]),
                        padk(kb_b), padk(vb_b), u_s, u_n, tk_s)
        y_s = tail_and_moe(y_s, ya.reshape(bs * ts, COLS), yb.reshape(bs * ts, COLS), tm_s, tm_s)
        outs[4].append(ka.reshape(bs, ts, H_A, 2, DH))
        outs[5].append(va.reshape(bs, ts, H_A, 2 * DH))
        outs[6].append(kb.reshape(bs, ts, H_B, DH))
        outs[7].append(vb.reshape(bs, ts, H_B, DH))

    return (y_p, y_s.reshape(bs, ts, d)) + tuple(jnp.stack(o) for o in outs)
```

```python
import functools
import math

import numpy as np
import jax
import jax.numpy as jnp
from jax import lax
from jax.experimental import pallas as pl
from jax.experimental.pallas import tpu as pltpu

F32 = jnp.float32
BF16 = jnp.bfloat16

EPS = 1e-6
CHUNK = 64
H_A = 4
DH = 64
H_B = 8
SLAB = 2 * DH
N_SLAB_A = H_A
N_SLAB_B = H_B // 2
COLS = H_A * SLAB
N_BUCKETS = 32
MAX_DISTANCE = 128
N_GROUPS = 4
EXPERTS_PER_GROUP = 4
N_EXPERTS = N_GROUPS * EXPERTS_PER_GROUP
N_PAIRS = EXPERTS_PER_GROUP * (EXPERTS_PER_GROUP - 1) // 2
N_CLASSES = N_GROUPS * N_PAIRS
ROUTER_LANES = 128
NEG = -1e30
SB_DEAD = -104.0
VMEM_LIMIT = 56 * 1024 * 1024
LOG2E = math.log2(math.e)
FAR_UNROLL = 8


def _rms(x, g):
    return x * lax.rsqrt(jnp.mean(x * x, axis=-1, keepdims=True) + EPS) * g


def _sigmoid(x):
    return 1.0 / (1.0 + jnp.exp(-x))


def _dot(a, b):
    return jnp.dot(a, b, preferred_element_type=F32)


def _dot_nt(a, b):
    return lax.dot_general(a, b, (((1,), (1,)), ((), ())), preferred_element_type=F32)


def _lane_tiles(x, n):
    return x if n == 1 else jnp.concatenate([x] * n, axis=1)


def _split_bf16(x):
    hi = x.astype(BF16)
    lo = (x - hi.astype(F32)).astype(BF16)
    return hi, lo


def _cols(j, width):
    return pl.ds(pl.multiple_of(j * width, width), width)


def _sub_head_norm(t, gain, gm):
    outs = []
    for s in range(0, COLS, 256):
        ts = t[:, s:s + 256]
        hi, lo = _split_bf16(ts * ts)
        outs.append(ts * lax.rsqrt(_dot(hi, gm) + _dot(lo, gm) + EPS))
    return jnp.concatenate(outs, axis=1) * gain


def _sub_head_norm_t(t, gain, gm):
    outs = []
    for s in range(0, COLS, 256):
        ts = t[s:s + 256, :]
        hi, lo = _split_bf16(ts * ts)
        outs.append(ts * lax.rsqrt(_dot(gm, hi) + _dot(gm, lo) + EPS))
    return jnp.concatenate(outs, axis=0) * gain


def _proj_kernel(x_ref, g_ref, w_ref, gq_ref, gk_ref, gm_ref,
                 ka_o, va_o, kb_o, vb_o, qa_b, ka_b, va_b, qb_b, kb_b, vb_b):
    h = _rms(x_ref[...], g_ref[...]).astype(BF16)
    proj = _dot(h, w_ref[...])
    gm = gm_ref[...]
    qa = _sub_head_norm(proj[:, 0 * COLS:1 * COLS], gq_ref[...], gm)
    ka = _sub_head_norm(proj[:, 1 * COLS:2 * COLS], gk_ref[...], gm)
    va = proj[:, 2 * COLS:3 * COLS]
    qb = proj[:, 3 * COLS:4 * COLS]
    kb = proj[:, 4 * COLS:5 * COLS]
    vb = proj[:, 5 * COLS:6 * COLS]
    ka_o[...] = ka
    va_o[...] = va
    kb_o[...] = kb
    vb_o[...] = vb
    scale = 1.0 / math.sqrt(DH)
    qa_b[...] = (qa * (scale * LOG2E)).astype(BF16)
    ka_b[...] = ka.astype(BF16)
    va_b[...] = va.astype(BF16)
    qb_b[...] = (qb * scale).astype(BF16)
    kb_b[...] = kb.astype(BF16)
    vb_b[...] = vb.astype(BF16)


def _project(x2d, g_mix, w_in_b, gq, gk, gm, tm):
    n, d = x2d.shape
    row = lambda i: (i, 0)
    const = lambda i: (0, 0)
    out_f = jax.ShapeDtypeStruct((n, COLS), F32)
    out_b = jax.ShapeDtypeStruct((n, COLS), BF16)
    blk = pl.BlockSpec((tm, COLS), row)
    return pl.pallas_call(
        _proj_kernel,
        grid=(n // tm,),
        in_specs=[pl.BlockSpec((tm, d), row),
                  pl.BlockSpec((1, d), const),
                  pl.BlockSpec(w_in_b.shape, const),
                  pl.BlockSpec((1, COLS), const),
                  pl.BlockSpec((1, COLS), const),
                  pl.BlockSpec(gm.shape, const)],
        out_specs=[blk] * 10,
        out_shape=[out_f] * 4 + [out_b] * 6,
        compiler_params=pltpu.CompilerParams(
            dimension_semantics=("arbitrary",), vmem_limit_bytes=VMEM_LIMIT),
        name="proj",
    )(x2d, g_mix, w_in_b, gq, gk, gm)


def _proj_t_kernel(x_ref, g_ref, wn_ref, wt_ref, gq_ref, gkt_ref, gm_ref,
                   kat_o, va_o, kbt_o, vbt_o, qa_b, kat_b, va_b, qb_b, kbt_b, vbt_b):
    h = _rms(x_ref[...], g_ref[...]).astype(BF16)
    tm = h.shape[0]
    nat = _dot(h, wn_ref[...])
    tr = _dot_nt(wt_ref[...], h)
    gm = gm_ref[...]
    qa = _sub_head_norm(nat[:, 0 * COLS:1 * COLS], gq_ref[...], gm)
    va = nat[:, 1 * COLS:2 * COLS]
    qb = nat[:, 2 * COLS:3 * COLS]
    kat = _sub_head_norm_t(tr[0 * COLS:1 * COLS], _lane_tiles(gkt_ref[...], tm // SLAB), gm)
    kbt = tr[1 * COLS:2 * COLS]
    vbt = tr[2 * COLS:3 * COLS]
    kat_o[...] = kat
    for hh in range(H_A):
        va_o[:, hh, :] = va[:, hh * SLAB:(hh + 1) * SLAB]
    kbt_o[...] = kbt
    vbt_o[...] = vbt
    scale = 1.0 / math.sqrt(DH)
    qa_b[...] = (qa * (scale * LOG2E)).astype(BF16)
    kat_b[...] = kat.astype(BF16)
    va_b[...] = va.astype(BF16)
    qb_b[...] = (qb * scale).astype(BF16)
    kbt_b[...] = kbt.astype(BF16)
    vbt_b[...] = vbt.astype(BF16)


def _project_t(x, g_mix, wn, wt, gq, gkt, gm, tm):
    b, t, d = x.shape
    const = lambda b, i: (0, 0)
    nat = pl.BlockSpec((None, tm, COLS), lambda b, i: (b, i, 0))
    fm = pl.BlockSpec((None, COLS, tm), lambda b, i: (b, 0, i))
    nat_s = lambda dt: jax.ShapeDtypeStruct((b, t, COLS), dt)
    fm_s = lambda dt: jax.ShapeDtypeStruct((b, COLS, t), dt)
    return pl.pallas_call(
        _proj_t_kernel,
        grid=(b, t // tm),
        in_specs=[pl.BlockSpec((None, tm, d), lambda b, i: (b, i, 0)),
                  pl.BlockSpec((1, d), const),
                  pl.BlockSpec(wn.shape, const),
                  pl.BlockSpec(wt.shape, const),
                  pl.BlockSpec((1, COLS), const),
                  pl.BlockSpec(gkt.shape, const),
                  pl.BlockSpec(gm.shape, const)],
        out_specs=[fm, pl.BlockSpec((None, tm, H_A, SLAB), lambda b, i: (b, i, 0, 0)), fm, fm,
                   nat, fm, nat, nat, fm, fm],
        out_shape=[fm_s(F32), jax.ShapeDtypeStruct((b, t, H_A, SLAB), F32), fm_s(F32), fm_s(F32),
                   nat_s(BF16), fm_s(BF16), nat_s(BF16), nat_s(BF16), fm_s(BF16), fm_s(BF16)],
        compiler_params=pltpu.CompilerParams(
            dimension_semantics=("arbitrary", "arbitrary"), vmem_limit_bytes=VMEM_LIMIT),
        name="proj_t",
    )(x, g_mix, wn, wt, gq, gkt, gm)


def _t5_bucket_np(rel):
    half = N_BUCKETS // 2
    max_exact = half // 2
    base = np.where(rel > 0, half, 0)
    n = np.abs(rel)
    nf = np.maximum(n, 1).astype(np.float64)
    large = max_exact + (np.log(nf / max_exact) / math.log(MAX_DISTANCE / max_exact)
                         * (half - max_exact)).astype(np.int32)
    large = np.minimum(large, half - 1)
    return (base + np.where(n < max_exact, n, large)).astype(np.int32)


def _bucket_tile_np(q_pos, k_pos):
    rel = k_pos[None, :] - q_pos[:, None]
    allowed = (k_pos[None, :] // CHUNK) <= (q_pos[:, None] // CHUNK)
    return np.where(allowed, _t5_bucket_np(rel), -1).astype(np.int32)


def _bias_kernel(tbl_ref, bkt_ref, o_ref):
    h = pl.program_id(0)
    bkt = bkt_ref[...]
    out = jnp.full(bkt.shape, NEG, F32)
    for b in range(N_BUCKETS):
        out = jnp.where(bkt == b, tbl_ref[b, h] * LOG2E, out)
    o_ref[...] = out


def _bias_tiles(rel_table, buckets):
    r, c = buckets.shape
    return pl.pallas_call(
        _bias_kernel,
        grid=(H_A,),
        in_specs=[pl.BlockSpec(memory_space=pltpu.SMEM),
                  pl.BlockSpec((r, c), lambda h: (0, 0))],
        out_specs=pl.BlockSpec((None, r, c), lambda h: (h, 0, 0)),
        out_shape=jax.ShapeDtypeStruct((H_A, r, c), F32),
        name="bias_tiles",
    )(rel_table, buckets)


def _two_maps(q):
    lane = lax.broadcasted_iota(jnp.int32, q.shape, 1)
    zero = jnp.zeros_like(q)
    return jnp.concatenate([jnp.where(lane < DH, q, zero), jnp.where(lane >= DH, q, zero)], axis=0)


def _diff_lambda(lam_ref, lam_init):
    p = lam_ref[...]
    s1 = jnp.sum(p[0:1] * p[1:2], axis=1, keepdims=True)
    s2 = jnp.sum(p[2:3] * p[3:4], axis=1, keepdims=True)
    return jnp.exp(s1) - jnp.exp(s2) + lam_init


def _diff_finish(acc, l, t, lam, sg, lam_init):
    o = acc[:t] / l[:t] - lam * (acc[t:] / l[t:])
    o = o * lax.rsqrt(jnp.mean(o * o, axis=-1, keepdims=True) + EPS) * sg
    return o * (1.0 - lam_init)


def _diff_prompt_kernel(tbl_ref, q_ref, kt_ref, v_ref, bias_ref, lam_ref, sg_ref, o_ref,
                        vx_ref, m_ref, l_ref, acc_ref, *, tq, tk, far_bucket, lam_init):
    h = pl.program_id(1)
    i = pl.program_id(2)

    @pl.when(i == 0)
    def _():
        vx_ref[:, :SLAB] = v_ref[...]
        vx_ref[:, SLAB:] = jnp.ones((vx_ref.shape[0], SLAB), vx_ref.dtype)

    qq = _two_maps(q_ref[...])
    m_ref[...] = jnp.full(m_ref.shape, NEG, F32)
    l_ref[...] = jnp.zeros(l_ref.shape, F32)
    acc_ref[...] = jnp.zeros(acc_ref.shape, F32)
    n_sub = tq // tk

    def update(jk, bias, first_row=0):
        n = tq - first_row
        segs = [(first_row, n), (tq + first_row, n)] if first_row else [(0, 2 * tq)]
        rd = lambda ref: jnp.concatenate([ref[a:a + k, :] for a, k in segs], axis=0)

        def wr(ref, val):
            for idx, (a, k) in enumerate(segs):
                ref[a:a + k, :] = val[idx * k:(idx + 1) * k]

        s = _dot(jnp.concatenate([qq[a:a + k] for a, k in segs], axis=0), kt_ref[:, _cols(jk, tk)])
        if bias is not None:
            s = s + jnp.concatenate([bias[first_row:], bias[first_row:]], axis=0)
        m_old = rd(m_ref)
        m_new = jnp.maximum(m_old, jnp.max(s, axis=1, keepdims=True))
        alpha = jnp.exp2(m_old - m_new)
        p = jnp.exp2(s - _lane_tiles(m_new, tk // SLAB))
        pv = _dot(p.astype(BF16), vx_ref[_cols(jk, tk), :])
        wr(l_ref, alpha * rd(l_ref) + pv[:, SLAB:])
        wr(acc_ref, alpha * rd(acc_ref) + pv[:, :SLAB])
        wr(m_ref, m_new)

    n_far = jnp.maximum(i - 1, 0)

    def far_tiles(first, count):
        for sub in range(count * n_sub):
            update(first * n_sub + sub, None)

    def far_body(j, carry):
        far_tiles(j * FAR_UNROLL, FAR_UNROLL)
        return carry

    lax.fori_loop(0, n_far // FAR_UNROLL, far_body, 0)
    done = n_far - n_far % FAR_UNROLL
    step = FAR_UNROLL // 2
    while step >= 1:
        take = (n_far % (2 * step)) >= step

        @pl.when(take)
        def _(done=done, step=step):
            far_tiles(done, step)

        done = done + jnp.where(take, step, 0)
        step //= 2
    m_ref[...] = m_ref[...] + tbl_ref[far_bucket, h] * LOG2E

    def biased_tile(j, kind):
        for sub in range(n_sub):
            first_row = sub * tk if kind == 0 else 0
            update(j * n_sub + sub, bias_ref[kind, :, sub * tk:(sub + 1) * tk], first_row)

    @pl.when(i >= 1)
    def _():
        biased_tile(i - 1, 1)
        biased_tile(i, 0)

    @pl.when(i == 0)
    def _():
        biased_tile(i, 0)

    lam = _diff_lambda(lam_ref, lam_init)
    o = _diff_finish(acc_ref[...], l_ref[...], tq, lam, sg_ref[...], lam_init)
    o_ref[...] = o.astype(o_ref.dtype)


def _diff_prompt(rel_table, qa, kat, va, bias, lam_p, sg, tq, far_bucket, lam_init):
    b, t, _ = qa.shape
    kern = functools.partial(_diff_prompt_kernel, tq=tq, tk=_tile(tq, 256), far_bucket=far_bucket,
                             lam_init=lam_init)
    return pl.pallas_call(
        kern,
        grid=(b, H_A, t // tq),
        in_specs=[pl.BlockSpec(memory_space=pltpu.SMEM),
                  pl.BlockSpec((None, tq, SLAB), lambda b, h, i: (b, i, h)),
                  pl.BlockSpec((None, SLAB, t), lambda b, h, i: (b, h, 0)),
                  pl.BlockSpec((None, t, SLAB), lambda b, h, i: (b, 0, h)),
                  pl.BlockSpec((None, 2, tq, tq), lambda b, h, i: (h, 0, 0, 0)),
                  pl.BlockSpec(lam_p.shape, lambda b, h, i: (0, 0)),
                  pl.BlockSpec((1, SLAB), lambda b, h, i: (0, 0))],
        out_specs=pl.BlockSpec((None, tq, SLAB), lambda b, h, i: (b, i, h)),
        out_shape=jax.ShapeDtypeStruct((b, t, COLS), BF16),
        scratch_shapes=[pltpu.VMEM((t, 2 * SLAB), BF16),
                        pltpu.VMEM((2 * tq, SLAB), F32),
                        pltpu.VMEM((2 * tq, SLAB), F32),
                        pltpu.VMEM((2 * tq, SLAB), F32)],
        compiler_params=pltpu.CompilerParams(
            dimension_semantics=("arbitrary", "arbitrary", "arbitrary"),
            vmem_limit_bytes=VMEM_LIMIT),
        name="diff_prompt",
    )(rel_table, qa, kat, va, bias, lam_p, sg)


def _diff_sample_kernel(q_ref, kct_ref, vc_ref, kn_ref, vn_ref, bc_ref, bn_ref, lam_ref, sg_ref,
                        o_ref, *, lam_init):
    qq = _two_maps(q_ref[...])
    t = q_ref.shape[0]
    bc = bc_ref[...]
    bn = bn_ref[...]
    s_c = _dot(qq, kct_ref[...].astype(BF16)) + jnp.concatenate([bc, bc], axis=0)
    s_n = _dot_nt(qq, kn_ref[...]) + jnp.concatenate([bn, bn], axis=0)
    m = jnp.maximum(jnp.max(s_c, axis=1, keepdims=True), jnp.max(s_n, axis=1, keepdims=True))
    p_c = jnp.exp2(s_c - m)
    p_n = jnp.exp2(s_n - m)
    l = jnp.sum(p_c, axis=1, keepdims=True) + jnp.sum(p_n, axis=1, keepdims=True)
    acc = _dot(p_c.astype(BF16), vc_ref[...].astype(BF16)) + _dot(p_n.astype(BF16), vn_ref[...])
    lam = _diff_lambda(lam_ref, lam_init)
    o = _diff_finish(acc, l, t, lam, sg_ref[...], lam_init)
    o_ref[...] = o.astype(o_ref.dtype)


def _diff_sample(qa, kct, vc, kn, vn, bias_c, bias_n, lam_p, sg, lam_init):
    b, t, _ = qa.shape
    p = vc.shape[1]
    tn = kn.shape[1]
    slab = lambda rows: pl.BlockSpec((None, rows, SLAB), lambda b, h: (b, 0, h))
    return pl.pallas_call(
        functools.partial(_diff_sample_kernel, lam_init=lam_init),
        grid=(b, H_A),
        in_specs=[slab(t),
                  pl.BlockSpec((None, SLAB, p), lambda b, h: (b, h, 0)),
                  slab(p), slab(tn), slab(tn),
                  pl.BlockSpec((None, t, p), lambda b, h: (h, 0, 0)),
                  pl.BlockSpec((None, t, tn), lambda b, h: (h, 0, 0)),
                  pl.BlockSpec(lam_p.shape, lambda b, h: (0, 0)),
                  pl.BlockSpec((1, SLAB), lambda b, h: (0, 0))],
        out_specs=slab(t),
        out_shape=jax.ShapeDtypeStruct((b, t, COLS), BF16),
        compiler_params=pltpu.CompilerParams(
            dimension_semantics=("arbitrary", "arbitrary"), vmem_limit_bytes=VMEM_LIMIT),
        name="diff_sample",
    )(qa, kct, vc, kn, vn, bias_c, bias_n, lam_p, sg)


def _sb_weights(z, u, c, allowed):
    sp = jnp.log(1.0 + jnp.exp(-jnp.abs(z)))
    log_beta = jnp.minimum(z, 0.0) - sp
    log_keep = -jnp.maximum(z, 0.0) - sp
    if allowed is not None:
        log_keep = jnp.where(allowed, log_keep, 0.0)
    hi, lo = _split_bf16(log_keep)
    later = _dot(hi, u) + _dot(lo, u)
    a = jnp.exp(log_beta + later + _lane_tiles(c, z.shape[1] // SLAB))
    if allowed is not None:
        a = jnp.where(allowed, a, 0.0)
    return a.astype(BF16), c + jnp.sum(log_keep, axis=1, keepdims=True)


def _strictly_causal_two_heads(t, n_keys):
    row = lax.broadcasted_iota(jnp.int32, (2 * t, n_keys), 0)
    col = lax.broadcasted_iota(jnp.int32, (2 * t, n_keys), 1)
    return col < jnp.where(row >= t, row - t, row)


def _merge_heads(acc, t):
    lane = lax.broadcasted_iota(jnp.int32, (t, SLAB), 1)
    return jnp.where(lane < DH, acc[:t], acc[t:])


def _sb_sweep_left(j0, c, acc, tile):
    def alive(c):
        return (jnp.max(c) > SB_DEAD).astype(jnp.int32)

    def cond(st):
        j, live, _, _ = st
        return (j >= 0) & (live > 0)

    def body(st):
        j, _, c, acc = st
        c, acc = tile(j, c, acc)
        return j - 1, alive(c), c, acc

    _, _, _, acc = lax.while_loop(cond, body, (j0, alive(c), c, acc))
    return acc


def _sb_prompt_kernel(q_ref, kt_ref, vt_ref, u_ref, o_ref, *, tq):
    n_q = q_ref.shape[0] // tq
    u = u_ref[...]
    mask = _strictly_causal_two_heads(tq, tq)
    zeros = jnp.zeros((2 * tq, SLAB), F32)

    def make_tile(qq):
        def tile(j, c, acc, mask=None, valid=None):
            a, c_new = _sb_weights(_dot(qq, kt_ref[:, _cols(j, tq)]), u, c, mask)
            if valid is not None:
                a = jnp.where(valid, a, jnp.zeros_like(a))
                c_new = jnp.where(valid, c_new, c)
            return c_new, acc + _dot_nt(a, vt_ref[:, _cols(j, tq)])
        return tile

    state = []
    for s in range(n_q):
        i = pl.program_id(2) * n_q + s
        tile = make_tile(_two_maps(q_ref[s * tq:(s + 1) * tq, :]))
        c, acc = tile(i, zeros, zeros, mask=mask)
        c, acc = tile(jnp.maximum(i - 1, 0), c, acc, valid=i >= 1)
        state.append((i, tile, c, acc))
    for s, (i, tile, c, acc) in enumerate(state):
        acc = _sb_sweep_left(i - 2, c, acc, tile)
        o_ref[s * tq:(s + 1) * tq, :] = _merge_heads(acc, tq).astype(o_ref.dtype)


def _sb_prompt(qb, kbt, vbt, u, tq):
    b, t, _ = qb.shape
    tg = _tile(t, 8 * tq)
    return pl.pallas_call(
        functools.partial(_sb_prompt_kernel, tq=tq),
        grid=(b, N_SLAB_B, t // tg),
        in_specs=[pl.BlockSpec((None, tg, SLAB), lambda b, h, i: (b, i, h)),
                  pl.BlockSpec((None, SLAB, t), lambda b, h, i: (b, h, 0)),
                  pl.BlockSpec((None, SLAB, t), lambda b, h, i: (b, h, 0)),
                  pl.BlockSpec(u.shape, lambda b, h, i: (0, 0))],
        out_specs=pl.BlockSpec((None, tg, SLAB), lambda b, h, i: (b, i, h)),
        out_shape=jax.ShapeDtypeStruct((b, t, COLS), BF16),
        compiler_params=pltpu.CompilerParams(
            dimension_semantics=("arbitrary", "arbitrary", "arbitrary"),
            vmem_limit_bytes=VMEM_LIMIT),
        name="sb_prompt",
    )(qb, kbt, vbt, u)


def _sb_sample_kernel(q_ref, kct_ref, vct_ref, kn_ref, vn_ref, u_ref, un_ref, o_ref, *, tk):
    t = q_ref.shape[0]
    tn = kn_ref.shape[0]
    n_tiles = kct_ref.shape[1] // tk
    qq = _two_maps(q_ref[...])
    u = u_ref[...]
    zeros = jnp.zeros((2 * t, SLAB), F32)
    a, c = _sb_weights(_dot_nt(qq, kn_ref[...]), un_ref[...], zeros,
                       _strictly_causal_two_heads(t, tn))
    acc = _dot(a, vn_ref[...])

    def tile(j, c, acc):
        a, c = _sb_weights(_dot(qq, kct_ref[:, _cols(j, tk)].astype(BF16)), u, c, None)
        return c, acc + _dot_nt(a, vct_ref[:, _cols(j, tk)].astype(BF16))

    acc = _sb_sweep_left(n_tiles - 1, c, acc, tile)
    o_ref[...] = _merge_heads(acc, t).astype(o_ref.dtype)


def _sb_sample(qb, kct, vct, kn, vn, u, un, tk):
    b, t, _ = qb.shape
    p = kct.shape[2]
    tn = kn.shape[1]
    slab = lambda rows: pl.BlockSpec((None, rows, SLAB), lambda b, h: (b, 0, h))
    slab_t = pl.BlockSpec((None, SLAB, p), lambda b, h: (b, h, 0))
    return pl.pallas_call(
        functools.partial(_sb_sample_kernel, tk=tk),
        grid=(b, N_SLAB_B),
        in_specs=[slab(t), slab_t, slab_t, slab(tn), slab(tn),
                  pl.BlockSpec(u.shape, lambda b, h: (0, 0)),
                  pl.BlockSpec(un.shape, lambda b, h: (0, 0))],
        out_specs=slab(t),
        out_shape=jax.ShapeDtypeStruct((b, t, COLS), BF16),
        compiler_params=pltpu.CompilerParams(
            dimension_semantics=("arbitrary", "arbitrary"), vmem_limit_bytes=VMEM_LIMIT),
        name="sb_sample",
    )(qb, kct, vct, kn, vn, u, un)


def _upper_sum_matrix(n):
    j = np.arange(n)[:, None]
    s = np.arange(n)[None, :]
    return jnp.asarray((j > s).astype(np.float32), dtype=BF16)


def _route_class(logits):
    lane = lax.broadcasted_iota(jnp.int32, logits.shape, 1).astype(F32)
    big = float(ROUTER_LANES)

    def first_argmax(vals):
        top = jnp.max(vals, axis=1, keepdims=True)
        return jnp.min(jnp.where(vals == top, lane, big), axis=1, keepdims=True)

    g_sel = first_argmax(jnp.where(lane < N_GROUPS, logits, NEG))
    first = N_GROUPS + g_sel * EXPERTS_PER_GROUP
    in_group = (lane >= first) & (lane < first + EXPERTS_PER_GROUP)
    el = jnp.where(in_group, logits, NEG)
    i1 = first_argmax(el)
    i2 = first_argmax(jnp.where(lane == i1, NEG, el))
    a = jnp.minimum(i1, i2) - first
    b = jnp.maximum(i1, i2) - first
    pair = a * (2 * EXPERTS_PER_GROUP - 1 - a) * 0.5 + (b - a - 1.0)
    return g_sel * N_PAIRS + pair


def _tail_kernel(x_ref, ya_ref, yb_ref, gmix_ref, wg_ref, bg_ref, wod_ref, wos_ref, wout_ref,
                 gffn_ref, wr_ref, br_ref, low_ref, x2_ref, route_ref, counts_ref, run_ref):
    @pl.when(pl.program_id(0) == 0)
    def _():
        run_ref[...] = jnp.zeros(run_ref.shape, F32)

    x = x_ref[...]
    d = x.shape[1]
    h = _rms(x, gmix_ref[...]).astype(BF16)
    gates = _sigmoid(_dot(h, wg_ref[...]) + bg_ref[...])
    mix = gates[:, :d] * _dot(ya_ref[...], wod_ref[...]) + gates[:, d:] * _dot(yb_ref[...], wos_ref[...])
    x2 = x + _dot(mix.astype(BF16), wout_ref[...])
    x2_ref[...] = x2.reshape(x2_ref.shape)
    h2 = _rms(x2, gffn_ref[...]).astype(BF16)
    logits = _dot(h2, wr_ref[...]) + br_ref[...]
    cls = _route_class(logits)
    lane = lax.broadcasted_iota(jnp.int32, logits.shape, 1).astype(F32)
    onehot = jnp.where(lane == cls, 1.0, 0.0)
    earlier = _dot(low_ref[...], onehot.astype(BF16)) + run_ref[...]
    rank = jnp.sum(earlier * onehot, axis=1, keepdims=True)
    route = jnp.where(lane == 0.0, cls, jnp.where(lane == 1.0, rank, 0.0))
    route_ref[...] = jnp.transpose(route)[:route_ref.shape[0], :]
    run = run_ref[...] + jnp.sum(onehot, axis=0, keepdims=True)
    run_ref[...] = run
    counts_ref[...] = run


def _tail(x2d, ya, yb, g_mix, wg, bg, wod, wos, wout, g_ffn, wr, br, tm):
    n, d = x2d.shape
    row = lambda i: (i, 0)
    const = lambda i: (0, 0)
    full = lambda a: pl.BlockSpec(a.shape, const)
    r = np.arange(tm)
    low =jnp.asarray((r[None, :] < r[:, None]).astype(np.float32), dtype=BF16)
    return pl.pallas_call(
        _tail_kernel,
        grid=(n // tm,),
        in_specs=[pl.BlockSpec((tm, d), row),
                  pl.BlockSpec((tm, COLS), row),
                  pl.BlockSpec((tm, COLS), row),
                  full(g_mix), full(wg), full(bg), full(wod), full(wos), full(wout),
                  full(g_ffn), full(wr), full(br), full(low)],
        out_specs=[pl.BlockSpec((tm, d // SLAB, SLAB), lambda i: (i, 0, 0)),
                   pl.BlockSpec((8, tm), lambda i: (0, i)),
                   pl.BlockSpec((1, ROUTER_LANES), const)],
        out_shape=[jax.ShapeDtypeStruct((n, d // SLAB, SLAB), F32),
                   jax.ShapeDtypeStruct((8, n), F32),
                   jax.ShapeDtypeStruct((1, ROUTER_LANES), F32)],
        scratch_shapes=[pltpu.VMEM((1, ROUTER_LANES), F32)],
        compiler_params=pltpu.CompilerParams(
            dimension_semantics=("arbitrary",), vmem_limit_bytes=VMEM_LIMIT),
        name="tail",
    )(x2d, ya, yb, g_mix, wg, bg, wod, wos, wout, g_ffn, wr, br, low)


def _class_experts_np():
    ea, eb = [], []
    for g in range(N_GROUPS):
        for a in range(EXPERTS_PER_GROUP):
            for b in range(a + 1, EXPERTS_PER_GROUP):
                ea.append(g * EXPERTS_PER_GROUP + a)
                eb.append(g * EXPERTS_PER_GROUP + b)
    return np.asarray(ea, np.int32), np.asarray(eb, np.int32)


def _dispatch_plan(route, counts, tm):
    n = route.shape[1]
    cls = route[0].astype(jnp.int32)
    rank = route[1].astype(jnp.int32)
    cnt = counts[0, :N_CLASSES].astype(jnp.int32)
    tiles = (cnt + tm - 1) // tm
    tile_end = jnp.cumsum(tiles)
    pos = (tile_end - tiles)[cls] * tm + rank
    n_tiles = n // tm + N_CLASSES
    t_idx = jnp.arange(n_tiles, dtype=jnp.int32)
    used = t_idx < tile_end[-1]
    tile_cls = jnp.sum((t_idx[:, None] >= tile_end[None, :]).astype(jnp.int32), axis=1)
    tile_cls = jnp.where(used, tile_cls, jnp.max(jnp.where(used, tile_cls, 0)))
    ea_np, eb_np = _class_experts_np()
    return pos, jnp.asarray(ea_np)[tile_cls], jnp.asarray(eb_np)[tile_cls], used.astype(jnp.int32)


COPY_UNROLL = 8


def _token_copies(n_tok, copy_of, wait_one):
    def issue(g, carry):
        for s in range(COPY_UNROLL):
            copy_of(g * COPY_UNROLL + s).start(priority=s % 2)
        return carry

    def drain(g, carry):
        for _ in range(COPY_UNROLL):
            wait_one()
        return carry

    lax.fori_loop(0, n_tok // COPY_UNROLL, issue, 0)
    lax.fori_loop(0, n_tok // COPY_UNROLL, drain, 0)


def _pos_spec(tm):
    return pl.BlockSpec((None, 1, tm), lambda i: (i, 0, 0), memory_space=pltpu.SMEM)


def _dispatch_kernel(pos_ref, x_ref, init_hbm, xs_hbm, sem):
    del init_hbm
    _token_copies(
        x_ref.shape[0],
        lambda r: pltpu.make_async_copy(x_ref.at[r], xs_hbm.at[pos_ref[0, r]], sem),
        lambda: pltpu.make_async_copy(x_ref.at[0], xs_hbm.at[0], sem).wait())


def _dispatch(pos, x2, n_slots, tm):
    n = x2.shape[0]
    tile = x2.shape[1:]
    return pl.pallas_call(
        _dispatch_kernel,
        grid=(n // tm,),
        in_specs=[_pos_spec(tm),
                  pl.BlockSpec((tm,) + tile, lambda i: (i, 0, 0)),
                  pl.BlockSpec(memory_space=pl.ANY)],
        out_specs=pl.BlockSpec(memory_space=pl.ANY),
        out_shape=jax.ShapeDtypeStruct((n_slots,) + tile, F32),
        scratch_shapes=[pltpu.SemaphoreType.DMA],
        input_output_aliases={2: 0},
        compiler_params=pltpu.CompilerParams(dimension_semantics=("arbitrary",)),
        name="moe_dispatch",
    )(pos.reshape(n // tm, 1, tm), x2, jnp.zeros((n_slots,) + tile, F32))


def _combine_kernel(pos_ref, ys_hbm, y_ref, buf_ref, sem):
    _token_copies(
        buf_ref.shape[0],
        lambda r: pltpu.make_async_copy(ys_hbm.at[pos_ref[0, r]], buf_ref.at[r], sem),
        lambda: pltpu.make_async_copy(ys_hbm.at[0], buf_ref.at[0], sem).wait())
    y_ref[...] = buf_ref[...].reshape(y_ref.shape)


def _combine(pos, ys, n, tm):
    tile = ys.shape[1:]
    d = tile[0] * tile[1]
    return pl.pallas_call(
        _combine_kernel,
        grid=(n // tm,),
        in_specs=[_pos_spec(tm), pl.BlockSpec(memory_space=pl.ANY)],
        out_specs=pl.BlockSpec((tm, d), lambda i: (i, 0)),
        out_shape=jax.ShapeDtypeStruct((n, d), F32),
        scratch_shapes=[pltpu.VMEM((tm,) + tile, F32), pltpu.SemaphoreType.DMA],
        compiler_params=pltpu.CompilerParams(dimension_semantics=("arbitrary",)),
        name="moe_combine",
    )(pos.reshape(n // tm, 1, tm), ys)


def _moe_kernel(ea_ref, eb_ref, used_ref, xs_ref, g_ref, wr_ref, br_ref,
                mga_ref, mua_ref, mda_ref, mgb_ref, mub_ref, mdb_ref, ys_ref):
    t = pl.program_id(0)

    @pl.when(used_ref[t] == 0)
    def _():
        ys_ref[...] = jnp.zeros(ys_ref.shape, F32)

    @pl.when(used_ref[t] != 0)
    def _():
        x = xs_ref[...].reshape(xs_ref.shape[0], -1)
        h2 = _rms(x, g_ref[...]).astype(BF16)
        logits = _dot(h2, wr_ref[...]) + br_ref[...]
        lane = lax.broadcasted_iota(jnp.int32, logits.shape, 1)
        pick = lambda e: jnp.sum(jnp.where(lane == N_GROUPS + e, logits, 0.0), axis=1, keepdims=True)
        la = pick(ea_ref[t])
        lb = pick(eb_ref[t])
        top = jnp.maximum(la, lb)
        pa = jnp.exp(la - top)
        pb = jnp.exp(lb - top)

        def expert(mg_ref, mu_ref, md_ref):
            g = _dot(h2, mg_ref[...])
            act = g * _sigmoid(g) * _dot(h2, mu_ref[...])
            return _dot(act.astype(BF16), md_ref[...])

        ya = expert(mga_ref, mua_ref, mda_ref)
        yb = expert(mgb_ref, mub_ref, mdb_ref)
        y = x + ((pa / (pa + pb)) * ya + (pb / (pa + pb)) * yb)
        ys_ref[...] = y.reshape(ys_ref.shape)


def _moe(ea, eb, used, xs, g_ffn, wr, br, mg, mu, md, tm):
    n_slots = xs.shape[0]
    tile = xs.shape[1:]
    d = tile[0] * tile[1]
    de = mg.shape[2]
    tok = pl.BlockSpec((tm,) + tile, lambda t, ea, eb, used: (t, 0, 0))
    const = lambda t, ea, eb, used: (0, 0)
    w_a = lambda shape: pl.BlockSpec((None,) + shape, lambda t, ea, eb, used: (ea[t], 0, 0))
    w_b = lambda shape: pl.BlockSpec((None,) + shape, lambda t, ea, eb, used: (eb[t], 0, 0))
    grid_spec = pltpu.PrefetchScalarGridSpec(
        num_scalar_prefetch=3,
        grid=(n_slots // tm,),
        in_specs=[tok,
                  pl.BlockSpec(g_ffn.shape, const),
                  pl.BlockSpec(wr.shape, const),
                  pl.BlockSpec(br.shape, const),
                  w_a((d, de)), w_a((d, de)), w_a((de, d)),
                  w_b((d, de)), w_b((d, de)), w_b((de, d))],
        out_specs=tok)
    return pl.pallas_call(
        _moe_kernel,
        grid_spec=grid_spec,
        out_shape=jax.ShapeDtypeStruct(xs.shape, F32),
        compiler_params=pltpu.CompilerParams(
            dimension_semantics=("arbitrary",), vmem_limit_bytes=VMEM_LIMIT),
        name="moe",
    )(ea, eb, used, xs, g_ffn, wr, br, mg, mu, md, mg, mu, md)


def _tile(n, pref):
    return pref if n % pref == 0 else n


def _feature_major(cache):
    b, p = cache.shape[:2]
    nd = cache.ndim
    return jnp.transpose(cache, (0,) + tuple(range(2, nd)) + (1,)).reshape(b, COLS, p)


def _token_major(xt, head_dims):
    b, _, t = xt.shape
    nd = len(head_dims)
    return jnp.transpose(xt.reshape((b,) + head_dims + (t,)), (0, nd + 1) + tuple(range(1, nd + 1)))


def kernel(x_prompt, x_sample, cache_diff_k, cache_diff_v, cache_sb_k, cache_sb_v, rel_bias_table, norm_mix_g, w_in, q_norm_g, k_norm_g, lambda_q1, lambda_k1, lambda_q2, lambda_k2, subln_g, w_o_diff, w_o_sb, w_branch_gate, b_branch_gate, w_out, norm_ffn_g, w_router_group, b_router_group, w_router_expert, b_router_expert, moe_w_gate, moe_w_up, moe_w_down):
    b, t, d = x_prompt.shape
    bs, ts, _ = x_sample.shape
    depth = norm_mix_g.shape[0]
    p = cache_diff_k.shape[2]
    tq = _tile(t, 512)
    tq_b = _tile(t, 256)
    tk_s = _tile(p, 256)
    tn = 128
    assert t % tq == 0 and t % tq_b == 0 and tq % CHUNK == 0 and p % CHUNK == 0
    assert ts <= CHUNK and ts <= tn
    assert p % tk_s == 0

    q_loc = np.arange(tq)
    bkt_prompt = np.concatenate([_bucket_tile_np(q_loc + tq, np.arange(tq) + tq),
                                 _bucket_tile_np(q_loc + tq, np.arange(tq))], axis=0)
    far = np.unique(_t5_bucket_np(-np.arange(tq + 1, 4 * tq)))
    assert far.size == 1
    far_bucket = int(far[0])
    q_s = p + np.arange(ts)
    k_s = np.concatenate([np.arange(p), p + np.arange(tn)])
    bkt_sample = _bucket_tile_np(q_s, k_s)
    bkt_sample[:, p + ts:] = -1
    bias_prompt = _bias_tiles(rel_bias_table, jnp.asarray(bkt_prompt)).reshape(H_A, 2, tq, tq)
    bias_sample = _bias_tiles(rel_bias_table, jnp.asarray(bkt_sample))
    bias_sc, bias_sn = bias_sample[:, :, :p], bias_sample[:, :, p:]

    gm = np.kron(np.eye(256 // DH), np.full((DH, DH), 1.0 / DH)).astype(np.float32)
    gm = jnp.asarray(gm, dtype=BF16)
    u_p = _upper_sum_matrix(tq_b)
    u_s = _upper_sum_matrix(tk_s)
    u_n = _upper_sum_matrix(tn)

    y_p = x_prompt
    y_s = x_sample.reshape(bs * ts, d)
    tm_proj = _tile(t, 512)
    tm_p = _tile(b * t, 512)
    tm_s = _tile(bs * ts, 128)
    tm_moe_p = _tile(b * t, 512)
    outs = [[] for _ in range(8)]
    for l in range(depth):
        lam_init = 0.8 - 0.6 * math.exp(-0.3 * l)
        w_in_b = w_in[l].astype(BF16)
        col = lambda g: w_in_b[:, g * COLS:(g + 1) * COLS]
        wn = jnp.concatenate([col(0), col(2), col(3)], axis=1)
        wt = jnp.concatenate([col(1), col(4), col(5)], axis=1).T
        g_mix = norm_mix_g[l].reshape(1, d)
        gq = jnp.tile(q_norm_g[l], COLS // DH).reshape(1, COLS)
        gk = jnp.tile(k_norm_g[l], COLS // DH).reshape(1, COLS)
        gkt = jnp.broadcast_to(gk.reshape(COLS, 1), (COLS, SLAB))
        lam_p = jnp.stack([lambda_q1[l], lambda_k1[l], lambda_q2[l], lambda_k2[l]])
        sg = subln_g[l].reshape(1, SLAB)
        wg = w_branch_gate[l].astype(BF16)
        bg = b_branch_gate[l].reshape(1, 2 * d)
        wod = w_o_diff[l].astype(BF16)
        wos = w_o_sb[l].astype(BF16)
        wout = w_out[l].astype(BF16)
        g_ffn = norm_ffn_g[l].reshape(1, d)
        n_r = N_GROUPS + N_EXPERTS
        wr = jnp.pad(jnp.concatenate([w_router_group[l], w_router_expert[l]], axis=1),
                     ((0, 0), (0, ROUTER_LANES - n_r)))
        wr = wr.astype(BF16)
        br = jnp.pad(jnp.concatenate([b_router_group[l], b_router_expert[l]]),
                     (0, ROUTER_LANES - n_r)).reshape(1, ROUTER_LANES)
        mg = moe_w_gate[l].astype(BF16)
        mu = moe_w_up[l].astype(BF16)
        md = moe_w_down[l].astype(BF16)

        def tail_and_moe(x2d, ya, yb, tm, tm_moe):
            n = x2d.shape[0]
            x2, route, counts = _tail(x2d, ya, yb, g_mix, wg, bg, wod, wos, wout, g_ffn, wr, br, tm)
            pos, ea, eb, used = _dispatch_plan(route, counts, tm_moe)
            xs = _dispatch(pos, x2, (n // tm_moe + N_CLASSES) * tm_moe, tm_moe)
            ys = _moe(ea, eb, used, xs, g_ffn, wr, br, mg, mu, md, tm_moe)
            return _combine(pos, ys, n, tm_moe)

        kat, va, kbt, vbt, qa_b, kat_b, va_b, qb_b, kbt_b, vbt_b = _project_t(
            y_p, g_mix, wn, wt, gq, gkt, gm, tm_proj)
        ya = _diff_prompt(rel_bias_table, qa_b, kat_b, va_b, bias_prompt, lam_p, sg,
                          tq, far_bucket, lam_init)
        yb = _sb_prompt(qb_b, kbt_b, vbt_b, u_p, tq_b)
        y_p = tail_and_moe(y_p.reshape(b * t, d), ya.reshape(b * t, COLS), yb.reshape(b * t, COLS),
                           tm_p, tm_moe_p).reshape(b, t, d)
        outs[0].append(_token_major(kat, (H_A, 2, DH)))
        outs[1].append(va)
        outs[2].append(_token_major(kbt, (H_B, DH)))
        outs[3].append(_token_major(vbt, (H_B, DH)))

        ka, va, kb, vb, qa_b, ka_b, va_b, qb_b, kb_b, vb_b = _project(y_s, g_mix, w_in_b, gq, gk, gm, tm_s)
        s3 = lambda a: a.reshape(bs, ts, COLS)
        padk = lambda a: jnp.pad(s3(a), ((0, 0), (0, tn - ts), (0, 0)))
        ya = _diff_sample(s3(qa_b), _feature_major(cache_diff_k[l]), cache_diff_v[l].reshape(bs, p, COLS),
                          padk(ka_b), padk(va_b), bias_sc, bias_sn, lam_p, sg, lam_init)
        yb = _sb_sample(s3(qb_b), _feature_major(cache_sb_k[l]), _feature_major(cache_sb_v[l]),
                        padk(kb_b), padk(vb_b), u_s, u_n, tk_s)
        y_s = tail_and_moe(y_s, ya.reshape(bs * ts, COLS), yb.reshape(bs * ts, COLS), tm_s, tm_s)
        outs[4].append(ka.reshape(bs, ts, H_A, 2, DH))
        outs[5].append(va.reshape(bs, ts, H_A, 2 * DH))
        outs[6].append(kb.reshape(bs, ts, H_B, DH))
        outs[7].append(vb.reshape(bs, ts, H_B, DH))

    return (y_p, y_s.reshape(bs, ts, d)) + tuple(jnp.stack(o) for o in outs)
```
